```python
import jax, jax.numpy as jnp
from jax import lax
import numpy as np

D_MODEL = 2048
BATCH = 2
SEQ = 4096
DEPTH = 2

GRID_W = 64
CTX_LEN = 256
MIX_WIDTH = D_MODEL
HEAD_DIM = 128
N_DIR = 2
GLA_WIDTH = MIX_WIDTH // 4
GLA_HEADS = GLA_WIDTH // HEAD_DIM
GLA_DV = HEAD_DIM
GLA_DK = HEAD_DIM // 2
GLA_RANK = 16
GLA_TAU = 16.0
HG_WIDTH = MIX_WIDTH // 4
HG_HEADS = HG_WIDTH // HEAD_DIM
HG_EXPAND = HEAD_DIM
HG_DV = HEAD_DIM
NA_WIDTH = MIX_WIDTH // 2
NA_HEADS = NA_WIDTH // HEAD_DIM
NA_DH = HEAD_DIM
WIN_ROWS = 8
WIN_COLS = 16
D_FF = 4 * D_MODEL
CHUNK = 64
ROPE_BASE = 10000.0
EPS = 1e-6
IN_SIZES = (GLA_HEADS * GLA_DK, GLA_HEADS * GLA_DK, GLA_WIDTH, GLA_WIDTH, N_DIR * GLA_RANK,
            HG_WIDTH, N_DIR * HG_WIDTH, HG_WIDTH, HG_WIDTH,
            NA_WIDTH, NA_WIDTH, NA_WIDTH)
IN_COLS = sum(IN_SIZES)

kernel_name = "hybrid_gla_hgrn2_natten_dit_block"


def rmsnorm(x, g):
    xf = x.astype(jnp.float32)
    y = xf * lax.rsqrt(jnp.mean(xf * xf, axis=-1, keepdims=True) + EPS)
    return (y * g.astype(jnp.float32)).astype(x.dtype)


def split_cols(p):
    return jnp.split(p, np.cumsum(IN_SIZES)[:-1].tolist(), axis=-1)


def to_heads(t, n_heads):
    b, t_len, _ = t.shape
    return t.reshape(b, t_len, n_heads, -1).transpose(0, 2, 1, 3)


def from_heads(t):
    b, h, t_len, d = t.shape
    return t.transpose(0, 2, 1, 3).reshape(b, t_len, h * d)


def axial_rope(x, row_pos, col_pos):
    half = x.shape[-1] // 2
    quarter = half // 2
    inv_freq = ROPE_BASE ** (-jnp.arange(quarter, dtype=jnp.float32) / quarter)
    xf = x.astype(jnp.float32)

    def rotate(xa, pos):
        ang = pos.astype(jnp.float32)[:, None] * inv_freq
        cos, sin = jnp.cos(ang), jnp.sin(ang)
        x1, x2 = xa[..., :quarter], xa[..., quarter:]
        return jnp.concatenate([x1 * cos - x2 * sin, x1 * sin + x2 * cos], axis=-1)

    return jnp.concatenate([rotate(xf[..., :half], row_pos), rotate(xf[..., half:], col_pos)], axis=-1).astype(x.dtype)


def chunk_gated_scan(q, k, v, log_a, s0):
    b, h, t_len, dk = q.shape
    dv = v.shape[-1]
    n = t_len // CHUNK

    def chunks(t):
        return jnp.moveaxis(t.astype(jnp.float32).reshape(b, h, n, CHUNK, t.shape[-1]), 2, 0)

    lower = jnp.tril(jnp.ones((CHUNK, CHUNK), dtype=bool))[:, :, None]

    def step(s, blk):
        qb, kb, vb, gb = blk
        cum = jnp.cumsum(gb, axis=2)
        rel = jnp.where(lower, cum[:, :, :, None, :] - cum[:, :, None, :, :], -jnp.inf)
        att = jnp.einsum('bhid,bhjd,bhijd->bhij', qb, kb, jnp.exp(rel))
        o = jnp.einsum('bhij,bhjv->bhiv', att, vb) + jnp.einsum('bhid,bhdv->bhiv', qb * jnp.exp(cum), s)
        cum_end = cum[:, :, -1:, :]
        s_new = jnp.exp(cum_end[:, :, 0, :, None]) * s + jnp.einsum('bhjd,bhjv->bhdv', kb * jnp.exp(cum_end - cum), vb)
        return s_new, o

    s_fin, o = lax.scan(step, s0, (chunks(q), chunks(k), chunks(v), chunks(log_a)))
    return jnp.moveaxis(o, 0, 2).reshape(b, h, t_len, dv), s_fin


def prefix_scan(ctx_in, lat_in, reverse):
    if reverse:
        ctx_in = tuple(t[:, :, ::-1] for t in ctx_in)
        lat_in = tuple(t[:, :, ::-1] for t in lat_in)
    b, h, _, dk = ctx_in[0].shape
    dv = ctx_in[2].shape[-1]
    s0 = jnp.zeros((b, h, dk, dv), jnp.float32)
    o_ctx, s_ctx = chunk_gated_scan(*ctx_in, s0)
    o_lat, _ = chunk_gated_scan(*lat_in, s_ctx)
    if reverse:
        o_ctx, o_lat = o_ctx[:, :, ::-1], o_lat[:, :, ::-1]
    return o_ctx, o_lat


def gated_readout(o, gate, norm_g):
    return (from_heads(rmsnorm(o, norm_g)) * jax.nn.silu(gate.astype(jnp.float32))).astype(gate.dtype)


def gla_mixer(p_ctx, p_lat, w_a2, b_a, norm_g, row_pos, col_pos, with_ctx_out):
    def prep(p, rotate):
        q, k, v, r, a_low = p
        q = to_heads(q, GLA_HEADS) * GLA_DK ** -0.5
        k = to_heads(k, GLA_HEADS)
        if rotate:
            q, k = axial_rope(q, row_pos, col_pos), axial_rope(k, row_pos, col_pos)
        v = to_heads(v, GLA_HEADS)
        log_a = tuple(
            to_heads(jax.nn.log_sigmoid((a_low[..., d * GLA_RANK:(d + 1) * GLA_RANK] @ w_a2[d] + b_a[d]).astype(jnp.float32)) / GLA_TAU, GLA_HEADS)
            for d in range(N_DIR))
        return q, k, v, r, log_a

    qc, kc, vc, rc, gc = prep(p_ctx, False)
    ql, kl, vl, rl, gl = prep(p_lat, True)
    oc_f, ol_f = prefix_scan((qc, kc, vc, gc[0]), (ql, kl, vl, gl[0]), reverse=False)
    oc_b, ol_b = prefix_scan((qc, kc, vc, gc[1]), (ql, kl, vl, gl[1]), reverse=True)
    y_lat = gated_readout(ol_f + ol_b, rl, norm_g)
    y_ctx = gated_readout(oc_f + oc_b, rc, norm_g) if with_ctx_out else None
    return y_lat, y_ctx


def hgrn2_mixer(p_ctx, p_lat, lower_bound, norm_g, with_ctx_out):
    def prep(p):
        q, f2, i, g = p
        q = jax.nn.silu(to_heads(q, HG_HEADS))
        v = to_heads(i, HG_HEADS)
        ks, log_f = [], []
        for d in range(N_DIR):
            logit = to_heads(f2[..., d * HG_WIDTH:(d + 1) * HG_WIDTH], HG_HEADS).astype(jnp.float32)
            lb = lower_bound[d].reshape(HG_HEADS, 1, HG_EXPAND)
            forget = lb + (1.0 - lb) * jax.nn.sigmoid(logit)
            log_f.append(jnp.log(forget))
            ks.append((1.0 - lb) * jax.nn.sigmoid(-logit))
        return q, v, g, ks, log_f

    qc, vc, gtc, kc, fc = prep(p_ctx)
    ql, vl, gtl, kl, fl = prep(p_lat)
    oc_f, ol_f = prefix_scan((qc, kc[0], vc, fc[0]), (ql, kl[0], vl, fl[0]), reverse=False)
    oc_b, ol_b = prefix_scan((qc, kc[1], vc, fc[1]), (ql, kl[1], vl, fl[1]), reverse=True)
    y_lat = gated_readout(ol_f + ol_b, gtl, norm_g)
    y_ctx = gated_readout(oc_f + oc_b, gtc, norm_g) if with_ctx_out else None
    return y_lat, y_ctx


def neighbourhood_attention(q, k, v, k_ctx, v_ctx, rpb):
    b, h, t_len, dh = q.shape
    rows = t_len // GRID_W
    kr = min(WIN_ROWS, rows)
    kc = WIN_COLS
    qg, kg, vg = (t.reshape(b, h, rows, GRID_W, dh) for t in (q, k, v))
    row_start = jnp.clip(jnp.arange(rows) - kr // 2, 0, rows - kr)
    col_start = jnp.clip(jnp.arange(GRID_W) - kc // 2, 0, GRID_W - kc)
    col_idx = col_start[:, None] + jnp.arange(kc)
    dc = col_idx - jnp.arange(GRID_W)[:, None] + WIN_COLS - 1
    scale = dh ** -0.5
    n_win = kr * kc

    def row_block(args):
        q_r, r0, r = args
        k_win = lax.dynamic_slice_in_dim(kg, r0, kr, axis=2)[:, :, :, col_idx]
        v_win = lax.dynamic_slice_in_dim(vg, r0, kr, axis=2)[:, :, :, col_idx]
        dr = r0 + jnp.arange(kr) - r + WIN_ROWS - 1
        bias = rpb[:, dr[None, :, None], dc[:, None, :]]
        s_win = jnp.einsum('bhqd,bhrqcd->bhqrc', q_r, k_win).astype(jnp.float32) * scale + bias.astype(jnp.float32)
        s_ctx = jnp.einsum('bhqd,bhkd->bhqk', q_r, k_ctx).astype(jnp.float32) * scale
        s = jnp.concatenate([s_win.reshape(b, h, GRID_W, n_win), s_ctx], axis=-1)
        p = jax.nn.softmax(s, axis=-1).astype(v.dtype)
        p_win = p[..., :n_win].reshape(b, h, GRID_W, kr, kc)
        return (jnp.einsum('bhqrc,bhrqcd->bhqd', p_win, v_win)
                + jnp.einsum('bhqk,bhkd->bhqd', p[..., n_win:], v_ctx))

    o = lax.map(row_block, (jnp.moveaxis(qg, 2, 0), row_start, jnp.arange(rows)))
    return jnp.moveaxis(o, 0, 2).reshape(b, h, t_len, dh)


def context_attention(q, k, v):
    s = jnp.einsum('bhqd,bhkd->bhqk', q, k).astype(jnp.float32) * q.shape[-1] ** -0.5
    return jnp.einsum('bhqk,bhkd->bhqd', jax.nn.softmax(s, axis=-1).astype(v.dtype), v)


def na_mixer(p_ctx, p_lat, q_norm, k_norm, rpb, with_ctx_out):
    def prep(p):
        q, k, v = p
        return (rmsnorm(to_heads(q, NA_HEADS), q_norm), rmsnorm(to_heads(k, NA_HEADS), k_norm), to_heads(v, NA_HEADS))

    qc, kc, vc = prep(p_ctx)
    ql, kl, vl = prep(p_lat)
    y_lat = from_heads(neighbourhood_attention(ql, kl, vl, kc, vc, rpb))
    y_ctx = from_heads(context_attention(qc, kc, vc)) if with_ctx_out else None
    return y_lat, y_ctx


def sqrelu_mlp(h, w1, w2):
    return jnp.square(jax.nn.relu(h @ w1)) @ w2


def setup_inputs(seed: int = 0) -> dict:
    key = jax.random.key(seed)
    ks = jax.random.split(key, 20)

    def nrm(k, shape, s):
        return jax.random.normal(k, shape, jnp.float32) * s

    return {
        "x": nrm(ks[0], (BATCH, SEQ, D_MODEL), 1.0),
        "c": nrm(ks[1], (BATCH, D_MODEL), 1.0),
        "ctx": nrm(ks[2], (BATCH, CTX_LEN, D_MODEL), 1.0),
        "c_ctx": nrm(ks[3], (D_MODEL,), 1.0),
        "w_mod": nrm(ks[4], (DEPTH, D_MODEL, 6 * D_MODEL), 0.5 * D_MODEL ** -0.5),
        "b_mod": nrm(ks[5], (DEPTH, 6 * D_MODEL), 0.02),
        "attn_norm": 1.0 + nrm(ks[6], (DEPTH, D_MODEL), 0.02),
        "w_in": nrm(ks[7], (DEPTH, D_MODEL, IN_COLS), D_MODEL ** -0.5),
        "gla_w_a2": nrm(ks[8], (DEPTH, N_DIR, GLA_RANK, GLA_HEADS * GLA_DK), GLA_RANK ** -0.5),
        "gla_b_a": nrm(ks[9], (DEPTH, N_DIR, GLA_HEADS * GLA_DK), 0.1),
        "gla_norm": 1.0 + nrm(ks[10], (DEPTH, GLA_DV), 0.02),
        "hg_lower_bounds": nrm(ks[11], (DEPTH, N_DIR, HG_WIDTH), 0.1),
        "hg_norm": 1.0 + nrm(ks[12], (DEPTH, HG_DV), 0.02),
        "na_q_norm": 1.0 + nrm(ks[13], (DEPTH, NA_DH), 0.02),
        "na_k_norm": 1.0 + nrm(ks[14], (DEPTH, NA_DH), 0.02),
        "na_rpb": nrm(ks[15], (DEPTH, NA_HEADS, 2 * WIN_ROWS - 1, 2 * WIN_COLS - 1), 0.1),
        "w_out": nrm(ks[16], (DEPTH, MIX_WIDTH, D_MODEL), MIX_WIDTH ** -0.5),
        "mlp_norm": 1.0 + nrm(ks[17], (DEPTH, D_MODEL), 0.02),
        "w_mlp1": nrm(ks[18], (DEPTH, D_MODEL, D_FF), D_MODEL ** -0.5),
        "w_mlp2": nrm(ks[19], (DEPTH, D_FF, D_MODEL), D_FF ** -0.5),
    }


def reference(x, c, ctx, c_ctx, w_mod, b_mod, attn_norm, w_in, gla_w_a2, gla_b_a, gla_norm,
              hg_lower_bounds, hg_norm, na_q_norm, na_k_norm, na_rpb, w_out, mlp_norm, w_mlp1, w_mlp2):
    t_len = x.shape[1]
    pos = jnp.arange(t_len)
    row_pos, col_pos = pos // GRID_W, pos % GRID_W
    lb_p = jax.nn.softmax(hg_lower_bounds.astype(jnp.float32), axis=0)
    lower_bounds = jnp.cumsum(lb_p, axis=0) - lb_p[0]

    for l in range(DEPTH):
        with_ctx_out = l < DEPTH - 1
        mod_lat = (jax.nn.silu(c) @ w_mod[l] + b_mod[l])[:, None, :]
        mod_ctx = jax.nn.silu(c_ctx) @ w_mod[l] + b_mod[l]
        sh_a, sc_a, g_a, sh_m, sc_m, g_m = jnp.split(mod_lat, 6, axis=-1)
        csh_a, csc_a, cg_a, csh_m, csc_m, cg_m = jnp.split(mod_ctx, 6, axis=-1)

        h_lat = rmsnorm(x, attn_norm[l]) * (1.0 + sc_a) + sh_a
        h_ctx = rmsnorm(ctx, attn_norm[l]) * (1.0 + csc_a) + csh_a
        p_lat = split_cols(h_lat @ w_in[l])
        p_ctx = split_cols(h_ctx @ w_in[l])

        ya_lat, ya_ctx = gla_mixer(p_ctx[0:5], p_lat[0:5], gla_w_a2[l], gla_b_a[l], gla_norm[l], row_pos, col_pos, with_ctx_out)
        yb_lat, yb_ctx = hgrn2_mixer(p_ctx[5:9], p_lat[5:9], lower_bounds[l], hg_norm[l], with_ctx_out)
        yc_lat, yc_ctx = na_mixer(p_ctx[9:12], p_lat[9:12], na_q_norm[l], na_k_norm[l], na_rpb[l], with_ctx_out)

        y_lat = jnp.concatenate([ya_lat, yb_lat, yc_lat], axis=-1)
        x = x + g_a * (y_lat @ w_out[l])
        x = x + g_m * sqrelu_mlp(rmsnorm(x, mlp_norm[l]) * (1.0 + sc_m) + sh_m, w_mlp1[l], w_mlp2[l])
        if with_ctx_out:
            y_ctx = jnp.concatenate([ya_ctx, yb_ctx, yc_ctx], axis=-1)
            ctx = ctx + cg_a * (y_ctx @ w_out[l])
            ctx = ctx + cg_m * sqrelu_mlp(rmsnorm(ctx, mlp_norm[l]) * (1.0 + csc_m) + csh_m, w_mlp1[l], w_mlp2[l])
    return x
```

```python
import functools

import numpy as np
import jax
import jax.numpy as jnp
from jax import lax
from jax.experimental import pallas as pl
from jax.experimental.pallas import tpu as pltpu

F32 = jnp.float32
BF16 = jnp.bfloat16

EPS = 1e-6
GRID_W = 64
HEAD_DIM = 128
GLA_HEADS = 4
GLA_DK = 64
GLA_RANK = 16
GLA_TAU = 16.0
HG_HEADS = 4
NA_HEADS = 8
WIN_ROWS = 8
WIN_COLS = 16
ROPE_BASE = 10000.0
NEG_BIG = -1e30

LANE = 128
ROW_BLK = 256
CHUNK = 128
N_LEVELS = 7
VMEM_LIMIT = 48 * 1024 * 1024

SCAN_COLS = 1024


def _dot(a, b):
    return jnp.dot(a, b, preferred_element_type=F32)


def _dot_nt(a, b):
    return lax.dot_general(a, b, (((1,), (1,)), ((), ())), preferred_element_type=F32)


def _dot_tn(a, b):
    return lax.dot_general(a, b, (((0,), (0,)), ((), ())), preferred_element_type=F32)


def _sigmoid(x):
    return 1.0 / (1.0 + jnp.exp(-x))


def _silu(x):
    return x * _sigmoid(x)


def _split3(x):
    hi = x.astype(BF16)
    r1 = x - hi.astype(F32)
    mid = r1.astype(BF16)
    lo = (r1 - mid.astype(F32)).astype(BF16)
    return jnp.concatenate([hi, mid, lo], axis=-1)


def _sum3(y, w):
    return y[:, 0:w] + y[:, w:2 * w] + y[:, 2 * w:3 * w]


def _mod_kernel(c_ref, w_ref, b_ref, o_ref):
    s = _silu(c_ref[...]).astype(BF16)
    o_ref[...] = _dot(s, w_ref[...].astype(BF16)) + b_ref[...]


def _modulation(cc, w_mod, b_mod):
    depth, d, n = w_mod.shape
    tn = 1024
    return pl.pallas_call(
        _mod_kernel,
        grid=(depth, n // tn),
        in_specs=[
            pl.BlockSpec((8, d), lambda l, j: (0, 0)),
            pl.BlockSpec((None, d, tn), lambda l, j: (l, 0, j)),
            pl.BlockSpec((None, 1, tn), lambda l, j: (l, 0, j)),
        ],
        out_specs=pl.BlockSpec((None, 8, tn), lambda l, j: (l, 0, j)),
        out_shape=jax.ShapeDtypeStruct((depth, 8, n), F32),
        compiler_params=pltpu.CompilerParams(vmem_limit_bytes=VMEM_LIMIT),
        name="modulation",
    )(cc, w_mod, b_mod.reshape(depth, 1, n))


def _norm_mod(x, g, shift, scale):
    y = x * lax.rsqrt(jnp.mean(x * x, axis=-1, keepdims=True) + EPS) * g
    return y * (1.0 + scale) + shift


def _inproj_kernel(x_ref, mod_ref, g_ref, w_ref, o_ref, h_ref):
    @pl.when(pl.program_id(1) == 0)
    def _():
        h = _norm_mod(x_ref[...], g_ref[...], mod_ref[0:1, :], mod_ref[1:2, :])
        h_ref[...] = h.astype(BF16)

    o_ref[...] = _dot(h_ref[...], w_ref[...])


def _in_projection(xu, modt, g, w, rows_per_mod):
    m, d = xu.shape
    n = w.shape[1]
    tm, tn = 512, n // 3
    per = rows_per_mod // tm
    return pl.pallas_call(
        _inproj_kernel,
        grid=(m // tm, n // tn),
        in_specs=[
            pl.BlockSpec((tm, d), lambda i, j: (i, 0)),
            pl.BlockSpec((None, 6, d), lambda i, j: (i // per, 0, 0)),
            pl.BlockSpec((1, d), lambda i, j: (0, 0)),
            pl.BlockSpec((d, tn), lambda i, j: (0, j)),
        ],
        out_specs=pl.BlockSpec((tm, tn), lambda i, j: (i, j)),
        out_shape=jax.ShapeDtypeStruct((m, n), F32),
        scratch_shapes=[pltpu.VMEM((tm, d), BF16)],
        compiler_params=pltpu.CompilerParams(vmem_limit_bytes=VMEM_LIMIT),
        name="in_projection",
    )(xu, modt, g.reshape(1, d), w)


def _scan_constants():
    c = CHUNK
    i = np.arange(c)[:, None]
    j = np.arange(c)[None, :]
    tri = np.stack([(j <= i), (j >= i)]).astype(np.float32)
    gather = np.zeros((2, N_LEVELS * c, c), np.float32)
    mask = np.zeros((2, N_LEVELS + 1, c, c), np.float32)
    for l in range(N_LEVELS):
        s = 1 << l
        base = (np.arange(c) // (2 * s)) * (2 * s)
        gather[0, l * c + np.arange(c), base + s - 1] = 1.0
        gather[1, l * c + np.arange(c), base + s] = 1.0
        level = ((i ^ j) >> l) == 1
        mask[0, l] = level & (i > j)
        mask[1, l] = level & (i < j)
    mask[:, N_LEVELS] = np.eye(c, dtype=np.float32)
    return tri, gather, mask


def _chunk_solve(q, k, v, g, st, tri, gather, mask_ref, is_rev):
    c, dk = q.shape
    cum = _sum3(_dot(tri, _split3(g)), dk)
    ref = _sum3(_dot(gather, _split3(cum)), dk)
    att = mask_ref[N_LEVELS] * _dot_nt(q.astype(BF16), k.astype(BF16))
    for l in range(N_LEVELS):
        w = jnp.exp(-jnp.abs(cum - ref[l * c:(l + 1) * c]))
        att = att + mask_ref[l] * _dot_nt((q * w).astype(BF16), (k * w).astype(BF16))
    cum_end = jnp.where(is_rev, cum[0:1, :], cum[c - 1:c, :])
    o = _dot(att.astype(BF16), v.astype(BF16))
    o = o + _dot_nt((q * jnp.exp(cum)).astype(BF16), st.astype(BF16))
    kd = k * jnp.exp(cum_end - cum)
    st_new = st * jnp.exp(cum_end) + _dot_tn(v.astype(BF16), kd.astype(BF16))
    return o, st_new


def _readout(o, gate, g):
    y = o * lax.rsqrt(jnp.mean(o * o, axis=-1, keepdims=True) + EPS) * g
    return y * _silu(gate)


def _scan_kernel(p_ref, al_ref, cos_ref, sin_ref, wg_ref, bg_ref, lb_ref, gn_ref, hn_ref,
                 tri_ref, gat_ref, mask_ref, ya_ref, yb_ref, oba_ref, obb_ref, sa_ref, sb_ref,
                 *, n_blk):
    t = pl.program_id(2)
    is_rev = t < n_blk
    tt = jnp.where(is_rev, t, t - n_blk)
    pos = jnp.where(tt == 0, n_blk - 1, jnp.where(is_rev, n_blk - 1 - tt, tt - 1))

    @pl.when(tt == 0)
    def _():
        sa_ref[...] = jnp.zeros_like(sa_ref)
        sb_ref[...] = jnp.zeros_like(sb_ref)

    tri = tri_ref[...]
    gather = gat_ref[...]
    lane = lax.broadcasted_iota(jnp.int32, (CHUNK, LANE), 1)
    first_half = (lane % 32) < 16
    lb = jnp.where(is_rev, lb_ref[1:2, :], lb_ref[0:1, :])

    for ci in range(ROW_BLK // CHUNK):
        off = pl.multiple_of(jnp.where(is_rev, ROW_BLK - CHUNK - ci * CHUNK, ci * CHUNK), CHUNK)
        rows = pl.ds(off, CHUNK)
        out_rows = pl.ds(pl.multiple_of(pos * ROW_BLK + off, CHUNK), CHUNK)

        qk = p_ref[rows, 0:128]
        swapped = jnp.where(first_half, pltpu.roll(qk, LANE - 16, 1), pltpu.roll(qk, 16, 1))
        qk = qk * cos_ref[rows, :] + swapped * sin_ref[rows, :]
        q = qk[:, 0:GLA_DK] * (GLA_DK ** -0.5)
        k = qk[:, GLA_DK:2 * GLA_DK]
        logit = _dot(al_ref[rows, :].astype(BF16), wg_ref[...]) + bg_ref[...]
        log_a = (jnp.minimum(logit, 0.0) - jnp.log(1.0 + jnp.exp(-jnp.abs(logit)))) / GLA_TAU
        g = jnp.where(is_rev, log_a[:, GLA_DK:2 * GLA_DK], log_a[:, 0:GLA_DK])
        o, st = _chunk_solve(q, k, p_ref[rows, 128:256], g, sa_ref[...], tri, gather, mask_ref, is_rev)
        sa_ref[...] = st

        @pl.when(is_rev)
        def _():
            oba_ref[out_rows, :] = o

        @pl.when(jnp.logical_not(is_rev))
        def _():
            y = _readout(o + oba_ref[out_rows, :], p_ref[rows, 256:384], gn_ref[...])
            ya_ref[rows, :] = y.astype(ya_ref.dtype)

        logit = jnp.where(is_rev, p_ref[rows, 640:768], p_ref[rows, 512:640])
        e = jnp.exp(-jnp.abs(logit))
        r = 1.0 / (1.0 + e)
        sig = jnp.where(logit >= 0, r, e * r)
        sig_neg = jnp.where(logit >= 0, e * r, r)
        g = jnp.log(lb + (1.0 - lb) * sig)
        k = (1.0 - lb) * sig_neg
        q = _silu(p_ref[rows, 384:512])
        o, st = _chunk_solve(q, k, p_ref[rows, 768:896], g, sb_ref[...], tri, gather, mask_ref, is_rev)
        sb_ref[...] = st

        @pl.when(is_rev)
        def _():
            obb_ref[out_rows, :] = o

        @pl.when(jnp.logical_not(is_rev))
        def _():
            y = _readout(o + obb_ref[out_rows, :], p_ref[rows, 896:1024], hn_ref[...])
            yb_ref[rows, :] = y.astype(yb_ref.dtype)


def _gated_scans(p, cos_t, sin_t, wg, bg, lb, gla_norm, hg_norm, consts, batch, n_lat_blk):
    m = p.shape[0]
    n_blk = n_lat_blk + 1
    a_low_blk = (p.shape[1] - LANE) // LANE
    tri, gather, mask = consts

    def seq_pos(t):
        is_rev = t < n_blk
        tt = jnp.where(is_rev, t, t - n_blk)
        return tt, jnp.where(tt == 0, n_blk - 1, jnp.where(is_rev, n_blk - 1 - tt, tt - 1))

    def row_blk(b, t):
        tt, pos = seq_pos(t)
        return jnp.where(tt == 0, batch * n_lat_blk + b, b * n_lat_blk + pos)

    def out_blk(b, t):
        return row_blk(b, jnp.maximum(t, n_blk))

    def direction(t):
        return jnp.where(t < n_blk, 1, 0)

    kern = functools.partial(_scan_kernel, n_blk=n_blk)
    return pl.pallas_call(
        kern,
        grid=(batch, GLA_HEADS, 2 * n_blk),
        in_specs=[
            pl.BlockSpec((ROW_BLK, SCAN_COLS), lambda b, h, t: (row_blk(b, t), h)),
            pl.BlockSpec((ROW_BLK, LANE), lambda b, h, t: (row_blk(b, t), a_low_blk)),
            pl.BlockSpec((ROW_BLK, LANE), lambda b, h, t: (seq_pos(t)[1], 0)),
            pl.BlockSpec((ROW_BLK, LANE), lambda b, h, t: (seq_pos(t)[1], 0)),
            pl.BlockSpec((None, LANE, LANE), lambda b, h, t: (h, 0, 0)),
            pl.BlockSpec((None, 1, LANE), lambda b, h, t: (h, 0, 0)),
            pl.BlockSpec((None, 2, LANE), lambda b, h, t: (h, 0, 0)),
            pl.BlockSpec((1, LANE), lambda b, h, t: (0, 0)),
            pl.BlockSpec((1, LANE), lambda b, h, t: (0, 0)),
            pl.BlockSpec((None, CHUNK, CHUNK), lambda b, h, t: (direction(t), 0, 0)),
            pl.BlockSpec((None, N_LEVELS * CHUNK, CHUNK), lambda b, h, t: (direction(t), 0, 0)),
            pl.BlockSpec((None, N_LEVELS + 1, CHUNK, CHUNK), lambda b, h, t: (direction(t), 0, 0, 0)),
        ],
        out_specs=[
            pl.BlockSpec((ROW_BLK, LANE), lambda b, h, t: (out_blk(b, t), h)),
            pl.BlockSpec((ROW_BLK, LANE), lambda b, h, t: (out_blk(b, t), h)),
        ],
        out_shape=[
            jax.ShapeDtypeStruct((m, GLA_HEADS * HEAD_DIM), BF16),
            jax.ShapeDtypeStruct((m, HG_HEADS * HEAD_DIM), BF16),
        ],
        scratch_shapes=[
            pltpu.VMEM((n_blk * ROW_BLK, HEAD_DIM), F32),
            pltpu.VMEM((n_blk * ROW_BLK, HEAD_DIM), F32),
            pltpu.VMEM((HEAD_DIM, GLA_DK), F32),
            pltpu.VMEM((HEAD_DIM, HEAD_DIM), F32),
        ],
        compiler_params=pltpu.CompilerParams(vmem_limit_bytes=VMEM_LIMIT),
        name="gated_scans",
    )(p, p, cos_t, sin_t, wg, bg, lb, gla_norm.reshape(1, -1), hg_norm.reshape(1, -1),
      tri.astype(BF16), gather.astype(BF16), mask)


def _head_norm(x, g):
    return x * lax.rsqrt(jnp.mean(x * x, axis=-1, keepdims=True) + EPS) * g


def _na_kernel(q_ref, kl_ref, vl_ref, kc_ref, vc_ref, qn_ref, kn_ref, bias_ref, y_ref,
               kls_ref, vls_ref, kcs_ref, vcs_ref, *, n_rows):
    t = pl.program_id(2)
    scale = HEAD_DIM ** -0.5

    @pl.when(t == 0)
    def _():
        kls_ref[...] = _head_norm(kl_ref[...], kn_ref[...]).astype(BF16)
        kcs_ref[...] = _head_norm(kc_ref[...], kn_ref[...]).astype(BF16)
        vls_ref[...] = vl_ref[...].astype(BF16)
        vcs_ref[...] = vc_ref[...].astype(BF16)

    q = _head_norm(q_ref[...], qn_ref[...]).astype(BF16)
    kc = kcs_ref[...]
    vc = vcs_ref[...]

    @pl.when(t == 0)
    def _():
        s = _dot_nt(q, kc) * scale
        e = jnp.exp(s - jnp.max(s, axis=-1, keepdims=True))
        o = _dot(e.astype(BF16), vc) / jnp.sum(e, axis=-1, keepdims=True)
        y_ref[...] = o.astype(y_ref.dtype)

    @pl.when(t > 0)
    def _():
        rows_per_blk = ROW_BLK // GRID_W
        for i in range(rows_per_blk):
            r = (t - 1) * rows_per_blk + i
            r0 = jnp.clip(r - WIN_ROWS // 2, 0, n_rows - WIN_ROWS)
            win = pl.ds(pl.multiple_of(r0 * GRID_W, GRID_W), WIN_ROWS * GRID_W)
            qr = q[i * GRID_W:(i + 1) * GRID_W]
            s_win = _dot_nt(qr, kls_ref[win, :]) * scale + bias_ref[r - r0]
            s_ctx = _dot_nt(qr, kc) * scale
            mx = jnp.maximum(jnp.max(s_win, axis=-1, keepdims=True), jnp.max(s_ctx, axis=-1, keepdims=True))
            e_win = jnp.exp(s_win - mx)
            e_ctx = jnp.exp(s_ctx - mx)
            den = jnp.sum(e_win, axis=-1, keepdims=True) + jnp.sum(e_ctx, axis=-1, keepdims=True)
            o = _dot(e_win.astype(BF16), vls_ref[win, :]) + _dot(e_ctx.astype(BF16), vc)
            y_ref[i * GRID_W:(i + 1) * GRID_W, :] = (o / den).astype(y_ref.dtype)


def _na_bias_table(rpb):
    d = np.arange(WIN_ROWS)[:, None]
    kk = np.arange(WIN_ROWS)[None, :]
    dr = kk - d + WIN_ROWS - 1
    c = np.arange(GRID_W)[:, None]
    kc = np.arange(GRID_W)[None, :]
    c0 = np.clip(c - WIN_COLS // 2, 0, GRID_W - WIN_COLS)
    valid = (kc >= c0) & (kc < c0 + WIN_COLS)
    dc = np.clip(kc - c + WIN_COLS - 1, 0, 2 * WIN_COLS - 2)
    tbl = rpb[:, dr[:, None, :, None], dc[None, :, None, :]]
    tbl = jnp.where(valid[None, None, :, None, :], tbl.astype(F32), NEG_BIG)
    return tbl.reshape(rpb.shape[0], WIN_ROWS, GRID_W, WIN_ROWS * GRID_W)


def _neighbourhood_attention(p, q_norm, k_norm, bias_tbl, batch, n_lat_blk, col0):
    m = p.shape[0]
    lat_rows = n_lat_blk * ROW_BLK
    qb, kb, vb = (col0 + i * NA_HEADS for i in range(3))
    ctx_blk = batch * n_lat_blk

    def q_blk(b, t):
        return jnp.where(t == 0, ctx_blk + b, b * n_lat_blk + t - 1)

    kern = functools.partial(_na_kernel, n_rows=lat_rows // GRID_W)
    return pl.pallas_call(
        kern,
        grid=(batch, NA_HEADS, n_lat_blk + 1),
        in_specs=[
            pl.BlockSpec((ROW_BLK, LANE), lambda b, h, t: (q_blk(b, t), qb + h)),
            pl.BlockSpec((lat_rows, LANE), lambda b, h, t: (b, kb + h)),
            pl.BlockSpec((lat_rows, LANE), lambda b, h, t: (b, vb + h)),
            pl.BlockSpec((ROW_BLK, LANE), lambda b, h, t: (ctx_blk + b, kb + h)),
            pl.BlockSpec((ROW_BLK, LANE), lambda b, h, t: (ctx_blk + b, vb + h)),
            pl.BlockSpec((1, LANE), lambda b, h, t: (0, 0)),
            pl.BlockSpec((1, LANE), lambda b, h, t: (0, 0)),
            pl.BlockSpec((None, WIN_ROWS, GRID_W, WIN_ROWS * GRID_W), lambda b, h, t: (h, 0, 0, 0)),
        ],
        out_specs=pl.BlockSpec((ROW_BLK, LANE), lambda b, h, t: (q_blk(b, t), h)),
        out_shape=jax.ShapeDtypeStruct((m, NA_HEADS * HEAD_DIM), BF16),
        scratch_shapes=[
            pltpu.VMEM((lat_rows, HEAD_DIM), BF16),
            pltpu.VMEM((lat_rows, HEAD_DIM), BF16),
            pltpu.VMEM((ROW_BLK, HEAD_DIM), BF16),
            pltpu.VMEM((ROW_BLK, HEAD_DIM), BF16),
        ],
        compiler_params=pltpu.CompilerParams(vmem_limit_bytes=VMEM_LIMIT),
        name="neighbourhood_attention",
    )(p, p, p, p, p, q_norm.reshape(1, -1), k_norm.reshape(1, -1), bias_tbl)


def _outproj_kernel(x_ref, ya_ref, yb_ref, yc_ref, mod_ref, wa_ref, wb_ref, wc_ref, o_ref):
    acc = _dot(ya_ref[...], wa_ref[...]) + _dot(yb_ref[...], wb_ref[...]) + _dot(yc_ref[...], wc_ref[...])
    o_ref[...] = x_ref[...] + mod_ref[2:3, :] * acc


def _out_projection(xu, ya, yb, yc, modt, w_out, rows_per_mod, n_rows):
    d = xu.shape[1]
    tm = 512
    per = rows_per_mod // tm
    wa, wb, wc = ya.shape[1], yb.shape[1], yc.shape[1]
    return pl.pallas_call(
        _outproj_kernel,
        grid=(n_rows // tm,),
        in_specs=[
            pl.BlockSpec((tm, d), lambda i: (i, 0)),
            pl.BlockSpec((tm, wa), lambda i: (i, 0)),
            pl.BlockSpec((tm, wb), lambda i: (i, 0)),
            pl.BlockSpec((tm, wc), lambda i: (i, 0)),
            pl.BlockSpec((None, 6, d), lambda i: (i // per, 0, 0)),
            pl.BlockSpec((wa, d), lambda i: (0, 0)),
            pl.BlockSpec((wb, d), lambda i: (wa // wb, 0)),
            pl.BlockSpec((wc, d), lambda i: ((wa + wb) // wc, 0)),
        ],
        out_specs=pl.BlockSpec((tm, d), lambda i: (i, 0)),
        out_shape=jax.ShapeDtypeStruct((n_rows, d), F32),
        compiler_params=pltpu.CompilerParams(vmem_limit_bytes=VMEM_LIMIT),
        name="out_projection",
    )(xu, ya, yb, yc, modt, w_out, w_out, w_out)


def _mlp_kernel(x_ref, mod_ref, g_ref, w1_ref, w2_ref, o_ref, h_ref, acc_ref):
    j = pl.program_id(1)

    @pl.when(j == 0)
    def _():
        h = _norm_mod(x_ref[...], g_ref[...], mod_ref[3:4, :], mod_ref[4:5, :])
        h_ref[...] = h.astype(BF16)
        acc_ref[...] = jnp.zeros_like(acc_ref)

    a = jnp.maximum(_dot(h_ref[...], w1_ref[...]), 0.0)
    acc_ref[...] += _dot((a * a).astype(BF16), w2_ref[...])

    @pl.when(j == pl.num_programs(1) - 1)
    def _():
        o_ref[...] = x_ref[...] + mod_ref[5:6, :] * acc_ref[...]


def _mlp(xu, modt, g, w1, w2, rows_per_mod):
    m, d = xu.shape
    ff = w1.shape[1]
    tm, tf = 512, 1024
    per = rows_per_mod // tm
    return pl.pallas_call(
        _mlp_kernel,
        grid=(m // tm, ff // tf),
        in_specs=[
            pl.BlockSpec((tm, d), lambda i, j: (i, 0)),
            pl.BlockSpec((None, 6, d), lambda i, j: (i // per, 0, 0)),
            pl.BlockSpec((1, d), lambda i, j: (0, 0)),
            pl.BlockSpec((d, tf), lambda i, j: (0, j)),
            pl.BlockSpec((tf, d), lambda i, j: (j, 0)),
        ],
        out_specs=pl.BlockSpec((tm, d), lambda i, j: (i, 0)),
        out_shape=jax.ShapeDtypeStruct((m, d), F32),
        scratch_shapes=[pltpu.VMEM((tm, d), BF16), pltpu.VMEM((tm, d), F32)],
        compiler_params=pltpu.CompilerParams(vmem_limit_bytes=VMEM_LIMIT),
        name="mlp",
    )(xu, modt, g.reshape(1, d), w1, w2)


def _permute_w_in(w_in):
    d = w_in.shape[-2]
    gw = GLA_HEADS * HEAD_DIM
    hw = HG_HEADS * HEAD_DIM
    nw = NA_HEADS * HEAD_DIM
    sizes = (GLA_HEADS * GLA_DK, GLA_HEADS * GLA_DK, gw, gw, 2 * GLA_RANK, hw, 2 * hw, hw, hw, nw, nw, nw)
    off = np.concatenate([[0], np.cumsum(sizes)])
    gq, gk, gv, gr, al, hq, hf, hi, hg, nq = (int(o) for o in off[:10])
    parts = []
    for h in range(GLA_HEADS):
        for start, width in ((gq + h * GLA_DK, GLA_DK), (gk + h * GLA_DK, GLA_DK),
                             (gv + h * HEAD_DIM, HEAD_DIM), (gr + h * HEAD_DIM, HEAD_DIM),
                             (hq + h * HEAD_DIM, HEAD_DIM), (hf + h * HEAD_DIM, HEAD_DIM),
                             (hf + hw + h * HEAD_DIM, HEAD_DIM), (hi + h * HEAD_DIM, HEAD_DIM),
                             (hg + h * HEAD_DIM, HEAD_DIM)):
            parts.append(w_in[..., start:start + width])
    parts.append(w_in[..., nq:nq + 3 * nw])
    parts.append(w_in[..., al:al + 2 * GLA_RANK])
    parts.append(jnp.zeros(w_in.shape[:-1] + (LANE - 2 * GLA_RANK,), w_in.dtype))
    return jnp.concatenate(parts, axis=-1).astype(BF16)


def _gla_gate_weights(w_a2, b_a):
    wg = jnp.zeros((GLA_HEADS, LANE, 2 * GLA_DK), F32)
    for dd in range(2):
        blk = w_a2[dd].reshape(GLA_RANK, GLA_HEADS, GLA_DK).transpose(1, 0, 2)
        wg = wg.at[:, dd * GLA_RANK:(dd + 1) * GLA_RANK, dd * GLA_DK:(dd + 1) * GLA_DK].set(blk)
    bg = b_a.reshape(2, GLA_HEADS, GLA_DK).transpose(1, 0, 2).reshape(GLA_HEADS, 1, 2 * GLA_DK)
    return wg.astype(BF16), bg.astype(F32)


def _rope_tables(seq, n_ctx):
    quarter = GLA_DK // 4
    inv_freq = ROPE_BASE ** (-jnp.arange(quarter, dtype=F32) / quarter)
    pos = jnp.arange(seq)
    lane = np.arange(LANE)
    is_col = (lane % GLA_DK) // (GLA_DK // 2) == 1
    first = (lane % (GLA_DK // 2)) < quarter
    p = jnp.where(is_col[None, :], (pos % GRID_W)[:, None], (pos // GRID_W)[:, None]).astype(F32)
    ang = p * inv_freq[lane % quarter][None, :]
    cos = jnp.cos(ang)
    sin = jnp.where(first[None, :], -jnp.sin(ang), jnp.sin(ang))
    cos = jnp.concatenate([cos, jnp.ones((n_ctx, LANE), F32)], axis=0)
    sin = jnp.concatenate([sin, jnp.zeros((n_ctx, LANE), F32)], axis=0)
    return cos, sin


def kernel(x, c, ctx, c_ctx, w_mod, b_mod, attn_norm, w_in, gla_w_a2, gla_b_a, gla_norm, hg_lower_bounds, hg_norm, na_q_norm, na_k_norm, na_rpb, w_out, mlp_norm, w_mlp1, w_mlp2):
    batch, seq, d = x.shape
    n_ctx = ctx.shape[1]
    depth = w_mod.shape[0]
    assert seq % ROW_BLK == 0 and n_ctx == ROW_BLK and batch == 2 and d % LANE == 0
    n_lat_blk = seq // ROW_BLK
    lat_rows = batch * seq

    xu = jnp.concatenate([x.reshape(lat_rows, d), ctx.reshape(batch * n_ctx, d)], axis=0)

    cc = jnp.zeros((8, d), F32).at[0:batch].set(c).at[batch].set(c_ctx)
    mod = _modulation(cc, w_mod, b_mod)
    modt = mod[:, 0:batch + 1].reshape(depth, batch + 1, 6, d)

    w_in_p = _permute_w_in(w_in)
    w_out_b = w_out.astype(BF16)
    w1_b = w_mlp1.astype(BF16)
    w2_b = w_mlp2.astype(BF16)
    lb_p = jax.nn.softmax(hg_lower_bounds.astype(F32), axis=0)
    lower = jnp.cumsum(lb_p, axis=0) - lb_p[0]
    lower = lower.reshape(depth, 2, HG_HEADS, HEAD_DIM).transpose(0, 2, 1, 3)
    cos_t, sin_t = _rope_tables(seq, n_ctx)
    consts = tuple(jnp.asarray(a) for a in _scan_constants())
    na_col0 = GLA_HEADS * SCAN_COLS // LANE

    for l in range(depth):
        last = l == depth - 1
        p = _in_projection(xu, modt[l], attn_norm[l], w_in_p[l], seq)
        wg, bg = _gla_gate_weights(gla_w_a2[l], gla_b_a[l])
        ya, yb = _gated_scans(p, cos_t, sin_t, wg, bg, lower[l], gla_norm[l], hg_norm[l], consts,
                              batch, n_lat_blk)
        yc = _neighbourhood_attention(p, na_q_norm[l], na_k_norm[l], _na_bias_table(na_rpb[l]),
                                      batch, n_lat_blk, na_col0)
        n_rows = lat_rows if last else xu.shape[0]
        xu = _out_projection(xu, ya, yb, yc, modt[l], w_out_b[l], seq, n_rows)
        xu = _mlp(xu, modt[l], mlp_norm[l], w1_b[l], w2_b[l], seq)
    return xu[:lat_rows].reshape(batch, seq, d)
```

```python
import functools

import numpy as np
import jax
import jax.numpy as jnp
from jax import lax
from jax.experimental import pallas as pl
from jax.experimental.pallas import tpu as pltpu

F32 = jnp.float32
BF16 = jnp.bfloat16

EPS = 1e-6
GRID_W = 64
HEAD_DIM = 128
GLA_HEADS = 4
GLA_DK = 64
GLA_RANK = 16
GLA_TAU = 16.0
HG_HEADS = 4
NA_HEADS = 8
WIN_ROWS = 8
WIN_COLS = 16
ROPE_BASE = 10000.0
NEG_BIG = -1e30

LANE = 128
ROW_BLK = 256
CHUNK = 128
N_LEVELS = 7
VMEM_LIMIT = 48 * 1024 * 1024

SCAN_COLS = 1024


def _dot(a, b):
    return jnp.dot(a, b, preferred_element_type=F32)


def _dot_nt(a, b):
    return lax.dot_general(a, b, (((1,), (1,)), ((), ())), preferred_element_type=F32)


def _dot_tn(a, b):
    return lax.dot_general(a, b, (((0,), (0,)), ((), ())), preferred_element_type=F32)


def _sigmoid(x):
    return 1.0 / (1.0 + jnp.exp(-x))


def _silu(x):
    return x * _sigmoid(x)


def _split3(x):
    hi = x.astype(BF16)
    r1 = x - hi.astype(F32)
    mid = r1.astype(BF16)
    lo = (r1 - mid.astype(F32)).astype(BF16)
    return jnp.concatenate([hi, mid, lo], axis=-1)


def _sum3(y, w):
    return y[:, 0:w] + y[:, w:2 * w] + y[:, 2 * w:3 * w]


def _mod_kernel(c_ref, w_ref, b_ref, o_ref):
    s = _silu(c_ref[...]).astype(BF16)
    o_ref[...] = _dot(s, w_ref[...].astype(BF16)) + b_ref[...]


def _modulation(cc, w_mod, b_mod):
    depth, d, n = w_mod.shape
    tn = 1024
    return pl.pallas_call(
        _mod_kernel,
        grid=(depth, n // tn),
        in_specs=[
            pl.BlockSpec((8, d), lambda l, j: (0, 0)),
            pl.BlockSpec((None, d, tn), lambda l, j: (l, 0, j)),
            pl.BlockSpec((None, 1, tn), lambda l, j: (l, 0, j)),
        ],
        out_specs=pl.BlockSpec((None, 8, tn), lambda l, j: (l, 0, j)),
        out_shape=jax.ShapeDtypeStruct((depth, 8, n), F32),
        compiler_params=pltpu.CompilerParams(vmem_limit_bytes=VMEM_LIMIT),
        name="modulation",
    )(cc, w_mod, b_mod.reshape(depth, 1, n))


def _norm_mod(x, g, shift, scale):
    y = x * lax.rsqrt(jnp.mean(x * x, axis=-1, keepdims=True) + EPS) * g
    return y * (1.0 + scale) + shift


def _inproj_kernel(x_ref, mod_ref, g_ref, w_ref, o_ref, h_ref):
    @pl.when(pl.program_id(1) == 0)
    def _():
        h = _norm_mod(x_ref[...], g_ref[...], mod_ref[0:1, :], mod_ref[1:2, :])
        h_ref[...] = h.astype(BF16)

    o_ref[...] = _dot(h_ref[...], w_ref[...])


def _in_projection(xu, modt, g, w, rows_per_mod):
    m, d = xu.shape
    n = w.shape[1]
    tm, tn = 512, n // 3
    per = rows_per_mod // tm
    return pl.pallas_call(
        _inproj_kernel,
        grid=(m // tm, n // tn),
        in_specs=[
            pl.BlockSpec((tm, d), lambda i, j: (i, 0)),
            pl.BlockSpec((None, 6, d), lambda i, j: (i // per, 0, 0)),
            pl.BlockSpec((1, d), lambda i, j: (0, 0)),
            pl.BlockSpec((d, tn), lambda i, j: (0, j)),
        ],
        out_specs=pl.BlockSpec((tm, tn), lambda i, j: (i, j)),
        out_shape=jax.ShapeDtypeStruct((m, n), F32),
        scratch_shapes=[pltpu.VMEM((tm, d), BF16)],
        compiler_params=pltpu.CompilerParams(vmem_limit_bytes=VMEM_LIMIT),
        name="in_projection",
    )(xu, modt, g.reshape(1, d), w)


def _scan_constants(reverse):
    c = CHUNK
    i = np.arange(c)[:, None]
    j = np.arange(c)[None, :]
    tri = (j >= i) if reverse else (j <= i)
    mask = np.zeros((N_LEVELS + 1, c, c), np.float32)
    for l in range(N_LEVELS):
        level = ((i ^ j) >> l) == 1
        mask[l] = level & ((i < j) if reverse else (i > j))
    mask[N_LEVELS] = np.eye(c, dtype=np.float32)
    return jnp.asarray(tri, BF16), jnp.asarray(np.concatenate([mask, mask], axis=1))


def _level_exponents(g, cum, scr_ref, reverse):
    c = CHUNK
    scr_ref[...] = cum
    row = lax.broadcasted_iota(jnp.int32, (c, LANE), 0)
    up = pltpu.roll(g, c - 1, 0)
    dn = pltpu.roll(g, 1, 0)
    m2 = row & 1
    m4 = row & 3
    if reverse:
        e0 = jnp.where(m2 == 0, g, 0.0)
        e1 = jnp.where(m4 == 0, g + up, jnp.where(m4 == 1, g, jnp.where(m4 == 2, 0.0, dn)))
    else:
        e0 = jnp.where(m2 == 1, g, 0.0)
        e1 = jnp.where(m4 == 0, up, jnp.where(m4 == 1, 0.0, jnp.where(m4 == 2, g, g + dn)))
    out = [e0, e1]
    for l in range(2, N_LEVELS):
        s = 1 << l
        pieces = []
        for blk in range(c // (2 * s)):
            lo = blk * 2 * s
            r = lo + (s if reverse else s - 1)
            pieces.append(-jnp.abs(cum[lo:lo + 2 * s] - scr_ref[r:r + 1, :]))
        out.append(pieces[0] if len(pieces) == 1 else jnp.concatenate(pieces, axis=0))
    return out


def _pair_weights(q, k, exps, mask_ref, lane_masks):
    if lane_masks is None:
        n = CHUNK

        def lhs(x):
            return x.astype(BF16)
    else:
        n = 2 * CHUNK

        def lhs(x):
            return jnp.concatenate([jnp.where(m, x, 0.0).astype(BF16) for m in lane_masks], axis=0)

    att = mask_ref[N_LEVELS, 0:n, :] * _dot_nt(lhs(q), k.astype(BF16))
    for l, e in enumerate(exps):
        w = jnp.exp(e)
        att = att + mask_ref[l, 0:n, :] * _dot_nt(lhs(q * w), (k * w).astype(BF16))
    return att


def _readout(o, gate, g):
    y = o * lax.rsqrt(jnp.mean(o * o, axis=-1, keepdims=True) + EPS) * g
    return y * _silu(gate)


def _scan_kernel(*refs, reverse):
    if reverse:
        (p_ref, al_ref, cos_ref, sin_ref, wg_ref, bg_ref, lb_ref, tri_ref, mask_ref,
         o_ref, scr_ref, sg_ref, sh_ref) = refs
    else:
        (p_ref, al_ref, cos_ref, sin_ref, wg_ref, bg_ref, lb_ref, tri_ref, mask_ref,
         orev_ref, gn_ref, hn_ref, ya_ref, yb_ref, scr_ref, sg_ref, sh_ref) = refs
    c = CHUNK

    @pl.when(pl.program_id(2) == 0)
    def _():
        sg_ref[...] = jnp.zeros_like(sg_ref)
        sh_ref[...] = jnp.zeros_like(sh_ref)

    lane = lax.broadcasted_iota(jnp.int32, (c, LANE), 1)
    first_half = (lane % (GLA_DK // 2)) < GLA_DK // 4
    head_a = lane < GLA_DK
    lane_masks = (head_a, jnp.logical_not(head_a))
    state_mask = jnp.concatenate([head_a, jnp.logical_not(head_a)], axis=0)
    f_col = 1280 if reverse else 1024
    end_row = 0 if reverse else c - 1
    n_chunks = ROW_BLK // c
    offsets = [(n_chunks - 1 - ci) * c if reverse else ci * c for ci in range(n_chunks)]

    def rope(x, rows):
        swapped = jnp.where(first_half, pltpu.roll(x, LANE - GLA_DK // 4, 1), pltpu.roll(x, GLA_DK // 4, 1))
        return x * cos_ref[rows, :] + swapped * sin_ref[rows, :]

    chains = []
    for off in offsets:
        rows = slice(off, off + c)
        q = rope(p_ref[rows, 0:128], rows) * (GLA_DK ** -0.5)
        k = rope(p_ref[rows, 128:256], rows)
        logit = _dot(al_ref[rows, :].astype(BF16), wg_ref[...]) + bg_ref[...]
        g = (jnp.minimum(logit, 0.0) - jnp.log(1.0 + jnp.exp(-jnp.abs(logit)))) / GLA_TAU
        chains.append((q, k, g))
        for hh in range(2):
            lb = lb_ref[:, hh * LANE:(hh + 1) * LANE]
            logit = p_ref[rows, f_col + hh * LANE:f_col + (hh + 1) * LANE]
            e = jnp.exp(-jnp.abs(logit))
            r = 1.0 / (1.0 + e)
            sig = jnp.where(logit >= 0, r, e * r)
            sig_neg = jnp.where(logit >= 0, e * r, r)
            g = jnp.log(lb + (1.0 - lb) * sig)
            k = (1.0 - lb) * sig_neg
            q = _silu(p_ref[rows, 768 + hh * LANE:768 + (hh + 1) * LANE])
            chains.append((q, k, g))

    g3 = jnp.concatenate([_split3(g) for (_, _, g) in chains], axis=-1)
    cum_all = _dot(tri_ref[...], g3)
    cums = [_sum3(cum_all[:, 3 * LANE * n:3 * LANE * (n + 1)], LANE) for n in range(len(chains))]

    atts = []
    for n, ((q, k, g), cum) in enumerate(zip(chains, cums)):
        exps = _level_exponents(g, cum, scr_ref.at[n], reverse)
        atts.append(_pair_weights(q, k, exps, mask_ref, lane_masks if n % 3 == 0 else None))

    for ci, off in enumerate(offsets):
        rows = slice(off, off + c)
        (q, k, _), cum, att = chains[3 * ci], cums[3 * ci], atts[3 * ci].astype(BF16)
        v = p_ref[rows, 256:512].astype(BF16)
        cum_end = cum[end_row:end_row + 1, :]
        st = sg_ref[...]
        inter = _dot_nt((q * jnp.exp(cum)).astype(BF16), st.astype(BF16))
        o_gla = jnp.concatenate([_dot(att[0:c], v[:, 0:LANE]), _dot(att[c:2 * c], v[:, LANE:2 * LANE])],
                                axis=-1) + inter
        kd = (k * jnp.exp(cum_end - cum)).astype(BF16)
        sg_ref[...] = st * jnp.exp(cum_end) + jnp.where(state_mask, _dot_tn(v, kd), 0.0)
        o_hg = []
        for hh in range(2):
            (q, k, _), cum, att = chains[3 * ci + 1 + hh], cums[3 * ci + 1 + hh], atts[3 * ci + 1 + hh]
            v = p_ref[rows, 1536 + hh * LANE:1536 + (hh + 1) * LANE].astype(BF16)
            cum_end = cum[end_row:end_row + 1, :]
            st = sh_ref[hh]
            o_hg.append(_dot(att.astype(BF16), v) + _dot_nt((q * jnp.exp(cum)).astype(BF16), st.astype(BF16)))
            kd = (k * jnp.exp(cum_end - cum)).astype(BF16)
            sh_ref[hh] = st * jnp.exp(cum_end) + _dot_tn(v, kd)
        o_all = jnp.concatenate([o_gla] + o_hg, axis=-1)
        if reverse:
            o_ref[rows, :] = o_all
        else:
            o_all = o_all + orev_ref[rows, :]
            for hh in range(2):
                sl = slice(hh * LANE, (hh + 1) * LANE)
                y = _readout(o_all[:, sl], p_ref[rows, 512 + hh * LANE:512 + (hh + 1) * LANE], gn_ref[...])
                ya_ref[rows, sl] = y.astype(ya_ref.dtype)
                y = _readout(o_all[:, 2 * LANE + hh * LANE:2 * LANE + (hh + 1) * LANE],
                             p_ref[rows, 1792 + hh * LANE:1792 + (hh + 1) * LANE], hn_ref[...])
                yb_ref[rows, sl] = y.astype(yb_ref.dtype)


def _gated_scans(p, cos_t, sin_t, wg, bg, lb, gla_norm, hg_norm, batch, n_lat_blk):
    assert CHUNK == HEAD_DIM == LANE and GLA_HEADS == HG_HEADS and GLA_HEADS % 2 == 0
    m = p.shape[0]
    n_pairs = GLA_HEADS // 2
    n_blk = n_lat_blk + 1
    a_low_blk = (p.shape[1] - LANE) // LANE
    pair_cols = 2 * SCAN_COLS

    def run(reverse, extra_in, extra_specs, out_shape, out_specs):
        def pos(t):
            lat = (n_lat_blk - t) if reverse else (t - 1)
            return jnp.where(t == 0, n_lat_blk, lat)

        def row_blk(b, t):
            return jnp.where(t == 0, batch * n_lat_blk + b, b * n_lat_blk + pos(t))

        d = 1 if reverse else 0
        tri, mask = _scan_constants(reverse)
        in_specs = [
            pl.BlockSpec((ROW_BLK, pair_cols), lambda b, h, t: (row_blk(b, t), h)),
            pl.BlockSpec((ROW_BLK, LANE), lambda b, h, t: (row_blk(b, t), a_low_blk)),
            pl.BlockSpec((ROW_BLK, LANE), lambda b, h, t: (pos(t), 0)),
            pl.BlockSpec((ROW_BLK, LANE), lambda b, h, t: (pos(t), 0)),
            pl.BlockSpec((None, None, LANE, LANE), lambda b, h, t: (d, h, 0, 0)),
            pl.BlockSpec((None, None, 1, LANE), lambda b, h, t: (d, h, 0, 0)),
            pl.BlockSpec((None, None, 1, 2 * LANE), lambda b, h, t: (d, h, 0, 0)),
            pl.BlockSpec((CHUNK, CHUNK), lambda b, h, t: (0, 0)),
            pl.BlockSpec((N_LEVELS + 1, 2 * CHUNK, CHUNK), lambda b, h, t: (0, 0, 0)),
        ] + [s(row_blk) for s in extra_specs]
        return pl.pallas_call(
            functools.partial(_scan_kernel, reverse=reverse),
            grid=(batch, n_pairs, n_blk),
            in_specs=in_specs,
            out_specs=[s(row_blk) for s in out_specs],
            out_shape=out_shape,
            scratch_shapes=[
                pltpu.VMEM((3 * ROW_BLK // CHUNK, CHUNK, LANE), F32),
                pltpu.VMEM((2 * HEAD_DIM, LANE), F32),
                pltpu.VMEM((2, HEAD_DIM, HEAD_DIM), F32),
            ],
            compiler_params=pltpu.CompilerParams(vmem_limit_bytes=VMEM_LIMIT),
            name="gated_scan_rev" if reverse else "gated_scan_fwd",
        )(p, p, cos_t, sin_t, wg, bg, lb, tri, mask, *extra_in)

    def rows_spec(width):
        return lambda row_blk: pl.BlockSpec((ROW_BLK, width), lambda b, h, t: (row_blk(b, t), h))

    def const_spec(row_blk):
        return pl.BlockSpec((1, LANE), lambda b, h, t: (0, 0))

    (o_rev,) = run(True, (), (), [jax.ShapeDtypeStruct((m, 2 * n_pairs * 2 * HEAD_DIM), F32)],
                   [rows_spec(4 * LANE)])
    return run(False, (o_rev, gla_norm.reshape(1, -1), hg_norm.reshape(1, -1)),
               (rows_spec(4 * LANE), const_spec, const_spec),
               [jax.ShapeDtypeStruct((m, GLA_HEADS * HEAD_DIM), BF16),
                jax.ShapeDtypeStruct((m, HG_HEADS * HEAD_DIM), BF16)],
               [rows_spec(2 * LANE), rows_spec(2 * LANE)])


def _head_norm(x, g):
    return x * lax.rsqrt(jnp.mean(x * x, axis=-1, keepdims=True) + EPS) * g


def _na_kernel(q_ref, kl_ref, vl_ref, kc_ref, vc_ref, qn_ref, kn_ref, bias_ref, y_ref,
               kls_ref, vls_ref, kcs_ref, vcs_ref, *, n_rows):
    t = pl.program_id(2)
    scale = HEAD_DIM ** -0.5

    @pl.when(t == 0)
    def _():
        kls_ref[...] = _head_norm(kl_ref[...], kn_ref[...]).astype(BF16)
        kcs_ref[...] = _head_norm(kc_ref[...], kn_ref[...]).astype(BF16)
        vls_ref[...] = vl_ref[...].astype(BF16)
        vcs_ref[...] = vc_ref[...].astype(BF16)

    q = _head_norm(q_ref[...], qn_ref[...]).astype(BF16)
    kc = kcs_ref[...]
    vc = vcs_ref[...]

    @pl.when(t == 0)
    def _():
        s = _dot_nt(q, kc) * scale
        e = jnp.exp(s - jnp.max(s, axis=-1, keepdims=True))
        o = _dot(e.astype(BF16), vc) / jnp.sum(e, axis=-1, keepdims=True)
        y_ref[...] = o.astype(y_ref.dtype)

    @pl.when(t > 0)
    def _():
        rows_per_blk = ROW_BLK // GRID_W
        for i in range(rows_per_blk):
            r = (t - 1) * rows_per_blk + i
            r0 = jnp.clip(r - WIN_ROWS // 2, 0, n_rows - WIN_ROWS)
            win = pl.ds(pl.multiple_of(r0 * GRID_W, GRID_W), WIN_ROWS * GRID_W)
            qr = q[i * GRID_W:(i + 1) * GRID_W]
            s_win = _dot_nt(qr, kls_ref[win, :]) * scale + bias_ref[r - r0]
            s_ctx = _dot_nt(qr, kc) * scale
            mx = jnp.maximum(jnp.max(s_win, axis=-1, keepdims=True), jnp.max(s_ctx, axis=-1, keepdims=True))
            e_win = jnp.exp(s_win - mx)
            e_ctx = jnp.exp(s_ctx - mx)
            den = jnp.sum(e_win, axis=-1, keepdims=True) + jnp.sum(e_ctx, axis=-1, keepdims=True)
            o = _dot(e_win.astype(BF16), vls_ref[win, :]) + _dot(e_ctx.astype(BF16), vc)
            y_ref[i * GRID_W:(i + 1) * GRID_W, :] = (o / den).astype(y_ref.dtype)


def _na_bias_table(rpb):
    n_heads = rpb.shape[0]
    c = np.arange(GRID_W)[:, None]
    kc = np.arange(GRID_W)[None, :]
    c0 = np.clip(c - WIN_COLS // 2, 0, GRID_W - WIN_COLS)
    valid = (kc >= c0) & (kc < c0 + WIN_COLS)
    pad = GRID_W - WIN_COLS
    padded = jnp.pad(rpb.astype(F32), ((0, 0), (0, 0), (pad, pad)))
    col = jnp.stack([padded[:, :, GRID_W - 1 - ci:2 * GRID_W - 1 - ci] for ci in range(GRID_W)], axis=2)
    col = jnp.where(valid[None, None], col, NEG_BIG)
    tbl = jnp.stack([col[:, WIN_ROWS - 1 - dl:2 * WIN_ROWS - 1 - dl] for dl in range(WIN_ROWS)], axis=1)
    tbl = tbl.transpose(0, 1, 3, 2, 4)
    return tbl.reshape(n_heads, WIN_ROWS, GRID_W, WIN_ROWS * GRID_W)


def _neighbourhood_attention(p, q_norm, k_norm, bias_tbl, batch, n_lat_blk, col0):
    m = p.shape[0]
    lat_rows = n_lat_blk * ROW_BLK
    qb, kb, vb = (col0 + i * NA_HEADS for i in range(3))
    ctx_blk = batch * n_lat_blk

    def q_blk(b, t):
        return jnp.where(t == 0, ctx_blk + b, b * n_lat_blk + t - 1)

    kern = functools.partial(_na_kernel, n_rows=lat_rows // GRID_W)
    return pl.pallas_call(
        kern,
        grid=(batch, NA_HEADS, n_lat_blk + 1),
        in_specs=[
            pl.BlockSpec((ROW_BLK, LANE), lambda b, h, t: (q_blk(b, t), qb + h)),
            pl.BlockSpec((lat_rows, LANE), lambda b, h, t: (b, kb + h)),
            pl.BlockSpec((lat_rows, LANE), lambda b, h, t: (b, vb + h)),
            pl.BlockSpec((ROW_BLK, LANE), lambda b, h, t: (ctx_blk + b, kb + h)),
            pl.BlockSpec((ROW_BLK, LANE), lambda b, h, t: (ctx_blk + b, vb + h)),
            pl.BlockSpec((1, LANE), lambda b, h, t: (0, 0)),
            pl.BlockSpec((1, LANE), lambda b, h, t: (0, 0)),
            pl.BlockSpec((None, WIN_ROWS, GRID_W, WIN_ROWS * GRID_W), lambda b, h, t: (h, 0, 0, 0)),
        ],
        out_specs=pl.BlockSpec((ROW_BLK, LANE), lambda b, h, t: (q_blk(b, t), h)),
        out_shape=jax.ShapeDtypeStruct((m, NA_HEADS * HEAD_DIM), BF16),
        scratch_shapes=[
            pltpu.VMEM((lat_rows, HEAD_DIM), BF16),
            pltpu.VMEM((lat_rows, HEAD_DIM), BF16),
            pltpu.VMEM((ROW_BLK, HEAD_DIM), BF16),
            pltpu.VMEM((ROW_BLK, HEAD_DIM), BF16),
        ],
        compiler_params=pltpu.CompilerParams(vmem_limit_bytes=VMEM_LIMIT),
        name="neighbourhood_attention",
    )(p, p, p, p, p, q_norm.reshape(1, -1), k_norm.reshape(1, -1), bias_tbl)


def _outproj_kernel(x_ref, ya_ref, yb_ref, yc_ref, mod_ref, wa_ref, wb_ref, wc_ref, o_ref):
    acc = _dot(ya_ref[...], wa_ref[...]) + _dot(yb_ref[...], wb_ref[...]) + _dot(yc_ref[...], wc_ref[...])
    o_ref[...] = x_ref[...] + mod_ref[2:3, :] * acc


def _out_projection(xu, ya, yb, yc, modt, w_out, rows_per_mod, n_rows):
    d = xu.shape[1]
    tm = 512
    per = rows_per_mod // tm
    wa, wb, wc = ya.shape[1], yb.shape[1], yc.shape[1]
    return pl.pallas_call(
        _outproj_kernel,
        grid=(n_rows // tm,),
        in_specs=[
            pl.BlockSpec((tm, d), lambda i: (i, 0)),
            pl.BlockSpec((tm, wa), lambda i: (i, 0)),
            pl.BlockSpec((tm, wb), lambda i: (i, 0)),
            pl.BlockSpec((tm, wc), lambda i: (i, 0)),
            pl.BlockSpec((None, 6, d), lambda i: (i // per, 0, 0)),
            pl.BlockSpec((wa, d), lambda i: (0, 0)),
            pl.BlockSpec((wb, d), lambda i: (wa // wb, 0)),
            pl.BlockSpec((wc, d), lambda i: ((wa + wb) // wc, 0)),
        ],
        out_specs=pl.BlockSpec((tm, d), lambda i: (i, 0)),
        out_shape=jax.ShapeDtypeStruct((n_rows, d), F32),
        compiler_params=pltpu.CompilerParams(vmem_limit_bytes=VMEM_LIMIT),
        name="out_projection",
    )(xu, ya, yb, yc, modt, w_out, w_out, w_out)


def _mlp_kernel(x_ref, mod_ref, g_ref, w1_ref, w2_ref, o_ref, h_ref, acc_ref):
    j = pl.program_id(1)

    @pl.when(j == 0)
    def _():
        h = _norm_mod(x_ref[...], g_ref[...], mod_ref[3:4, :], mod_ref[4:5, :])
        h_ref[...] = h.astype(BF16)
        acc_ref[...] = jnp.zeros_like(acc_ref)

    a = jnp.maximum(_dot(h_ref[...], w1_ref[...]), 0.0)
    acc_ref[...] += _dot((a * a).astype(BF16), w2_ref[...])

    @pl.when(j == pl.num_programs(1) - 1)
    def _():
        o_ref[...] = x_ref[...] + mod_ref[5:6, :] * acc_ref[...]


def _mlp(xu, modt, g, w1, w2, rows_per_mod):
    m, d = xu.shape
    ff = w1.shape[1]
    tm, tf = 512, 1024
    per = rows_per_mod // tm
    return pl.pallas_call(
        _mlp_kernel,
        grid=(m // tm, ff // tf),
        in_specs=[
            pl.BlockSpec((tm, d), lambda i, j: (i, 0)),
            pl.BlockSpec((None, 6, d), lambda i, j: (i // per, 0, 0)),
            pl.BlockSpec((1, d), lambda i, j: (0, 0)),
            pl.BlockSpec((d, tf), lambda i, j: (0, j)),
            pl.BlockSpec((tf, d), lambda i, j: (j, 0)),
        ],
        out_specs=pl.BlockSpec((tm, d), lambda i, j: (i, 0)),
        out_shape=jax.ShapeDtypeStruct((m, d), F32),
        scratch_shapes=[pltpu.VMEM((tm, d), BF16), pltpu.VMEM((tm, d), F32)],
        compiler_params=pltpu.CompilerParams(vmem_limit_bytes=VMEM_LIMIT),
        name="mlp",
    )(xu, modt, g.reshape(1, d), w1, w2)


def _permute_w_in(w_in):
    gw = GLA_HEADS * HEAD_DIM
    hw = HG_HEADS * HEAD_DIM
    nw = NA_HEADS * HEAD_DIM
    sizes = (GLA_HEADS * GLA_DK, GLA_HEADS * GLA_DK, gw, gw, 2 * GLA_RANK, hw, 2 * hw, hw, hw, nw, nw, nw)
    off = np.concatenate([[0], np.cumsum(sizes)])
    gq, gk, gv, gr, al, hq, hf, hi, hg, nq = (int(o) for o in off[:10])
    parts = []
    for pr in range(GLA_HEADS // 2):
        for start, width in ((gq + pr * 2 * GLA_DK, 2 * GLA_DK), (gk + pr * 2 * GLA_DK, 2 * GLA_DK),
                             (gv + pr * 2 * HEAD_DIM, 2 * HEAD_DIM), (gr + pr * 2 * HEAD_DIM, 2 * HEAD_DIM),
                             (hq + pr * 2 * HEAD_DIM, 2 * HEAD_DIM), (hf + pr * 2 * HEAD_DIM, 2 * HEAD_DIM),
                             (hf + hw + pr * 2 * HEAD_DIM, 2 * HEAD_DIM), (hi + pr * 2 * HEAD_DIM, 2 * HEAD_DIM),
                             (hg + pr * 2 * HEAD_DIM, 2 * HEAD_DIM)):
            parts.append(w_in[..., start:start + width])
    parts.append(w_in[..., nq:nq + 3 * nw])
    parts.append(w_in[..., al:al + 2 * GLA_RANK])
    parts.append(jnp.zeros(w_in.shape[:-1] + (LANE - 2 * GLA_RANK,), w_in.dtype))
    return jnp.concatenate(parts, axis=-1).astype(BF16)


def _gla_gate_weights(w_a2, b_a):
    n_pairs = GLA_HEADS // 2
    wg = jnp.zeros((2, n_pairs, LANE, 2 * GLA_DK), F32)
    for dd in range(2):
        blk = w_a2[dd].reshape(GLA_RANK, n_pairs, 2 * GLA_DK).transpose(1, 0, 2)
        wg = wg.at[dd, :, dd * GLA_RANK:(dd + 1) * GLA_RANK, :].set(blk)
    bg = b_a.reshape(2, n_pairs, 1, 2 * GLA_DK)
    return wg.astype(BF16), bg.astype(F32)


def _rope_tables(seq, n_ctx):
    quarter = GLA_DK // 4
    inv_freq = ROPE_BASE ** (-jnp.arange(quarter, dtype=F32) / quarter)
    pos = jnp.arange(seq)
    lane = np.arange(LANE)
    is_col = (lane % GLA_DK) // (GLA_DK // 2) == 1
    first = (lane % (GLA_DK // 2)) < quarter
    p = jnp.where(is_col[None, :], (pos % GRID_W)[:, None], (pos // GRID_W)[:, None]).astype(F32)
    ang = p * inv_freq[lane % quarter][None, :]
    cos = jnp.cos(ang)
    sin = jnp.where(first[None, :], -jnp.sin(ang), jnp.sin(ang))
    cos = jnp.concatenate([cos, jnp.ones((n_ctx, LANE), F32)], axis=0)
    sin = jnp.concatenate([sin, jnp.zeros((n_ctx, LANE), F32)], axis=0)
    return cos, sin


def kernel(x, c, ctx, c_ctx, w_mod, b_mod, attn_norm, w_in, gla_w_a2, gla_b_a, gla_norm, hg_lower_bounds, hg_norm, na_q_norm, na_k_norm, na_rpb, w_out, mlp_norm, w_mlp1, w_mlp2):
    batch, seq, d = x.shape
    n_ctx = ctx.shape[1]
    depth = w_mod.shape[0]
    assert seq % ROW_BLK == 0 and n_ctx == ROW_BLK and batch == 2 and d % LANE == 0
    n_lat_blk = seq // ROW_BLK
    lat_rows = batch * seq

    xu = jnp.concatenate([x.reshape(lat_rows, d), ctx.reshape(batch * n_ctx, d)], axis=0)

    cc = jnp.zeros((8, d), F32).at[0:batch].set(c).at[batch].set(c_ctx)
    mod = _modulation(cc, w_mod, b_mod)
    modt = mod[:, 0:batch + 1].reshape(depth, batch + 1, 6, d)

    w_in_p = _permute_w_in(w_in)
    w_out_b = w_out.astype(BF16)
    w1_b = w_mlp1.astype(BF16)
    w2_b = w_mlp2.astype(BF16)
    lb_p = jax.nn.softmax(hg_lower_bounds.astype(F32), axis=0)
    lower = jnp.cumsum(lb_p, axis=0) - lb_p[0]
    lower = lower.reshape(depth, 2, HG_HEADS // 2, 1, 2 * HEAD_DIM)
    cos_t, sin_t = _rope_tables(seq, n_ctx)
    na_col0 = GLA_HEADS * SCAN_COLS // LANE

    for l in range(depth):
        last = l == depth - 1
        p = _in_projection(xu, modt[l], attn_norm[l], w_in_p[l], seq)
        wg, bg = _gla_gate_weights(gla_w_a2[l], gla_b_a[l])
        ya, yb = _gated_scans(p, cos_t, sin_t, wg, bg, lower[l], gla_norm[l], hg_norm[l], batch, n_lat_blk)
        yc = _neighbourhood_attention(p, na_q_norm[l], na_k_norm[l], _na_bias_table(na_rpb[l]),
                                      batch, n_lat_blk, na_col0)
        n_rows = lat_rows if last else xu.shape[0]
        xu = _out_projection(xu, ya, yb, yc, modt[l], w_out_b[l], seq, n_rows)
        xu = _mlp(xu, modt[l], mlp_norm[l], w1_b[l], w2_b[l], seq)
    return xu[:lat_rows].reshape(batch, seq, d)
```

```python
import functools

import numpy as np
import jax
import jax.numpy as jnp
from jax import lax
from jax.experimental import pallas as pl
from jax.experimental.pallas import tpu as pltpu

F32 = jnp.float32
BF16 = jnp.bfloat16

EPS = 1e-6
GRID_W = 64
HEAD_DIM = 128
GLA_HEADS = 4
GLA_DK = 64
GLA_RANK = 16
GLA_TAU = 16.0
HG_HEADS = 4
NA_HEADS = 8
WIN_ROWS = 8
WIN_COLS = 16
ROPE_BASE = 10000.0
NEG_BIG = -1e30

LANE = 128
ROW_BLK = 256
CHUNK = 128
N_LEVELS = 7
TOK_BLK = 512
VMEM_LIMIT = 48 * 1024 * 1024
VMEM_LIMIT_BIG = 56 * 1024 * 1024


def _projection_layout():
    gw, hw, nw = GLA_HEADS * HEAD_DIM, HG_HEADS * HEAD_DIM, NA_HEADS * HEAD_DIM
    sections = (("gla_q", GLA_HEADS * GLA_DK), ("gla_k", GLA_HEADS * GLA_DK), ("gla_v", gw), ("gla_r", gw),
                ("a_low", 2 * GLA_RANK), ("hg_q", hw), ("hg_f", 2 * hw), ("hg_i", hw), ("hg_g", hw),
                ("na_q", nw), ("na_k", nw), ("na_v", nw))
    src, off = {}, 0
    for name, width in sections:
        src[name] = (off, width)
        off += width
    dst, off = {}, 0
    for name, width in sections:
        if name != "a_low":
            dst[name] = off
            off += width
    dst["a_low"] = off
    return src, dst, off + LANE


P_SRC, P_COLS, P_WIDTH = _projection_layout()


def _dot(a, b):
    return jnp.dot(a, b, preferred_element_type=F32)


def _dot_nt(a, b):
    return lax.dot_general(a, b, (((1,), (1,)), ((), ())), preferred_element_type=F32)


def _dot_tn(a, b):
    return lax.dot_general(a, b, (((0,), (0,)), ((), ())), preferred_element_type=F32)


def _sigmoid(x):
    return 1.0 / (1.0 + jnp.exp(-x))


def _silu(x):
    return x * _sigmoid(x)


def _split3(x):
    hi = x.astype(BF16)
    r1 = x - hi.astype(F32)
    mid = r1.astype(BF16)
    lo = (r1 - mid.astype(F32)).astype(BF16)
    return jnp.concatenate([hi, mid, lo], axis=-1)


def _sum3(y, w):
    return y[:, 0:w] + y[:, w:2 * w] + y[:, 2 * w:3 * w]


def _mod_kernel(c_ref, w_ref, b_ref, o_ref):
    s = _silu(c_ref[...]).astype(BF16)
    o_ref[...] = _dot(s, w_ref[...].astype(BF16)) + b_ref[...]


def _modulation(cc, w_mod, b_mod):
    depth, d, n = w_mod.shape
    tn = 1024
    return pl.pallas_call(
        _mod_kernel,
        grid=(depth, n // tn),
        in_specs=[
            pl.BlockSpec((8, d), lambda l, j: (0, 0)),
            pl.BlockSpec((None, d, tn), lambda l, j: (l, 0, j)),
            pl.BlockSpec((None, 1, tn), lambda l, j: (l, 0, j)),
        ],
        out_specs=pl.BlockSpec((None, 8, tn), lambda l, j: (l, 0, j)),
        out_shape=jax.ShapeDtypeStruct((depth, 8, n), F32),
        compiler_params=pltpu.CompilerParams(vmem_limit_bytes=VMEM_LIMIT),
        name="modulation",
    )(cc, w_mod, b_mod.reshape(depth, 1, n))


def _norm_mod(x, g, shift, scale):
    y = x * lax.rsqrt(jnp.mean(x * x, axis=-1, keepdims=True) + EPS) * g
    return y * (1.0 + scale) + shift


class _Tokens:
    def __init__(self, lat, ctx, n_lat, ctx_blk):
        self.lat, self.ctx, self.n_lat, self.ctx_blk = lat, ctx, n_lat, ctx_blk

    def specs(self, d):
        n_lat, ctx_blk = self.n_lat, self.ctx_blk
        return [pl.BlockSpec((TOK_BLK, d), lambda i, *_: (jnp.minimum(i, n_lat - 1), 0)),
                pl.BlockSpec((TOK_BLK, d), lambda i, *_: (ctx_blk, 0))]


def _inproj_kernel(xl_ref, xc_ref, mod_ref, g_ref, w_ref, o_ref, h_ref, *, n_lat):
    i = pl.program_id(0)
    first = pl.program_id(1) == 0

    def fill(x_ref):
        h = _norm_mod(x_ref[...], g_ref[...], mod_ref[0:1, :], mod_ref[1:2, :])
        h_ref[...] = h.astype(BF16)

    pl.when(first & (i < n_lat))(lambda: fill(xl_ref))
    pl.when(first & (i >= n_lat))(lambda: fill(xc_ref))
    o_ref[...] = _dot(h_ref[...], w_ref[...])


def _in_projection(tok, modt, g, w, rows_per_mod):
    d, n = w.shape
    tn = n // 3
    per = rows_per_mod // TOK_BLK
    n_blk = tok.n_lat + 1
    return pl.pallas_call(
        functools.partial(_inproj_kernel, n_lat=tok.n_lat),
        grid=(n_blk, n // tn),
        in_specs=tok.specs(d) + [
            pl.BlockSpec((None, 6, d), lambda i, j: (i // per, 0, 0)),
            pl.BlockSpec((1, d), lambda i, j: (0, 0)),
            pl.BlockSpec((d, tn), lambda i, j: (0, j)),
        ],
        out_specs=pl.BlockSpec((TOK_BLK, tn), lambda i, j: (i, j)),
        out_shape=jax.ShapeDtypeStruct((n_blk * TOK_BLK, n), F32),
        scratch_shapes=[pltpu.VMEM((TOK_BLK, d), BF16)],
        compiler_params=pltpu.CompilerParams(vmem_limit_bytes=VMEM_LIMIT_BIG),
        name="in_projection",
    )(tok.lat, tok.ctx, modt, g.reshape(1, d), w)


def _scan_constants(reverse):
    c = CHUNK
    i = np.arange(c)[:, None]
    j = np.arange(c)[None, :]
    tri = (j >= i) if reverse else (j <= i)
    mask = np.zeros((N_LEVELS + 1, c, c), np.float32)
    for l in range(N_LEVELS):
        level = ((i ^ j) >> l) == 1
        mask[l] = level & ((i < j) if reverse else (i > j))
    mask[N_LEVELS] = np.eye(c, dtype=np.float32)
    return jnp.asarray(tri, BF16), jnp.asarray(np.concatenate([mask, mask], axis=1))


def _level_exponents(g, cum, scr_ref, reverse):
    c = CHUNK
    scr_ref[...] = cum
    row = lax.broadcasted_iota(jnp.int32, (c, LANE), 0)
    up = pltpu.roll(g, c - 1, 0)
    dn = pltpu.roll(g, 1, 0)
    m2 = row & 1
    m4 = row & 3
    if reverse:
        e0 = jnp.where(m2 == 0, g, 0.0)
        e1 = jnp.where(m4 == 0, g + up, jnp.where(m4 == 1, g, jnp.where(m4 == 2, 0.0, dn)))
    else:
        e0 = jnp.where(m2 == 1, g, 0.0)
        e1 = jnp.where(m4 == 0, up, jnp.where(m4 == 1, 0.0, jnp.where(m4 == 2, g, g + dn)))
    out = [e0, e1]
    for l in range(2, N_LEVELS):
        s = 1 << l
        pieces = []
        for blk in range(c // (2 * s)):
            lo = blk * 2 * s
            r = lo + (s if reverse else s - 1)
            pieces.append(-jnp.abs(cum[lo:lo + 2 * s] - scr_ref[r:r + 1, :]))
        out.append(pieces[0] if len(pieces) == 1 else jnp.concatenate(pieces, axis=0))
    return out


def _pair_weights(q, k, exps, mask_ref, lane_masks):
    if lane_masks is None:
        n = CHUNK

        def lhs(x):
            return x.astype(BF16)
    else:
        n = 2 * CHUNK

        def lhs(x):
            return jnp.concatenate([jnp.where(m, x, 0.0).astype(BF16) for m in lane_masks], axis=0)

    att = mask_ref[N_LEVELS, 0:n, :] * _dot_nt(lhs(q), k.astype(BF16))
    for l, e in enumerate(exps):
        w = jnp.exp(e)
        att = att + mask_ref[l, 0:n, :] * _dot_nt(lhs(q * w), (k * w).astype(BF16))
    return att


def _readout(o, gate, g):
    y = o * lax.rsqrt(jnp.mean(o * o, axis=-1, keepdims=True) + EPS) * g
    return y * _silu(gate)


def _scan_kernel(*refs, reverse):
    (gq_ref, gk_ref, gv_ref, hq_ref, hf_ref, hi_ref, al_ref, cos_ref, sin_ref, wg_ref, bg_ref, lb_ref,
     tri_ref, mask_ref) = refs[:14]
    if reverse:
        o_ref, scr_ref, sg_ref, sh_ref = refs[14:]
    else:
        gr_ref, hg_ref, orev_ref, gn_ref, hn_ref, ya_ref, yb_ref, scr_ref, sg_ref, sh_ref = refs[14:]
    c = CHUNK

    @pl.when(pl.program_id(2) == 0)
    def _():
        sg_ref[...] = jnp.zeros_like(sg_ref)
        sh_ref[...] = jnp.zeros_like(sh_ref)

    lane = lax.broadcasted_iota(jnp.int32, (c, LANE), 1)
    first_half = (lane % (GLA_DK // 2)) < GLA_DK // 4
    head_a = lane < GLA_DK
    lane_masks = (head_a, jnp.logical_not(head_a))
    state_mask = jnp.concatenate([head_a, jnp.logical_not(head_a)], axis=0)
    end_row = 0 if reverse else c - 1
    n_chunks = ROW_BLK // c
    offsets = [(n_chunks - 1 - ci) * c if reverse else ci * c for ci in range(n_chunks)]

    def rope(x, rows):
        swapped = jnp.where(first_half, pltpu.roll(x, LANE - GLA_DK // 4, 1), pltpu.roll(x, GLA_DK // 4, 1))
        return x * cos_ref[rows, :] + swapped * sin_ref[rows, :]

    chains = []
    for off in offsets:
        rows = slice(off, off + c)
        q = rope(gq_ref[rows, :], rows) * (GLA_DK ** -0.5)
        k = rope(gk_ref[rows, :], rows)
        logit = _dot(al_ref[rows, :].astype(BF16), wg_ref[...]) + bg_ref[...]
        g = (jnp.minimum(logit, 0.0) - jnp.log(1.0 + jnp.exp(-jnp.abs(logit)))) / GLA_TAU
        chains.append((q, k, g))
        for hh in range(2):
            head = slice(hh * LANE, (hh + 1) * LANE)
            lb = lb_ref[:, head]
            logit = hf_ref[rows, head]
            e = jnp.exp(-jnp.abs(logit))
            r = 1.0 / (1.0 + e)
            sig = jnp.where(logit >= 0, r, e * r)
            sig_neg = jnp.where(logit >= 0, e * r, r)
            g = jnp.log(lb + (1.0 - lb) * sig)
            k = (1.0 - lb) * sig_neg
            q = _silu(hq_ref[rows, head])
            chains.append((q, k, g))

    g3 = jnp.concatenate([_split3(g) for (_, _, g) in chains], axis=-1)
    cum_all = _dot(tri_ref[...], g3)
    cums = [_sum3(cum_all[:, 3 * LANE * n:3 * LANE * (n + 1)], LANE) for n in range(len(chains))]

    atts = []
    for n, ((q, k, g), cum) in enumerate(zip(chains, cums)):
        exps = _level_exponents(g, cum, scr_ref.at[n], reverse)
        atts.append(_pair_weights(q, k, exps, mask_ref, lane_masks if n % 3 == 0 else None))

    for ci, off in enumerate(offsets):
        rows = slice(off, off + c)
        (q, k, _), cum, att = chains[3 * ci], cums[3 * ci], atts[3 * ci].astype(BF16)
        v = gv_ref[rows, :].astype(BF16)
        cum_end = cum[end_row:end_row + 1, :]
        st = sg_ref[...]
        inter = _dot_nt((q * jnp.exp(cum)).astype(BF16), st.astype(BF16))
        o_gla = jnp.concatenate([_dot(att[0:c], v[:, 0:LANE]), _dot(att[c:2 * c], v[:, LANE:2 * LANE])],
                                axis=-1) + inter
        kd = (k * jnp.exp(cum_end - cum)).astype(BF16)
        sg_ref[...] = st * jnp.exp(cum_end) + jnp.where(state_mask, _dot_tn(v, kd), 0.0)
        o_hg = []
        for hh in range(2):
            (q, k, _), cum, att = chains[3 * ci + 1 + hh], cums[3 * ci + 1 + hh], atts[3 * ci + 1 + hh]
            v = hi_ref[rows, hh * LANE:(hh + 1) * LANE].astype(BF16)
            cum_end = cum[end_row:end_row + 1, :]
            st = sh_ref[hh]
            o_hg.append(_dot(att.astype(BF16), v) + _dot_nt((q * jnp.exp(cum)).astype(BF16), st.astype(BF16)))
            kd = (k * jnp.exp(cum_end - cum)).astype(BF16)
            sh_ref[hh] = st * jnp.exp(cum_end) + _dot_tn(v, kd)
        o_all = jnp.concatenate([o_gla] + o_hg, axis=-1)
        if reverse:
            o_ref[rows, :] = o_all
        else:
            o_all = o_all + orev_ref[rows, :]
            for hh in range(2):
                sl = slice(hh * LANE, (hh + 1) * LANE)
                y = _readout(o_all[:, sl], gr_ref[rows, sl], gn_ref[...])
                ya_ref[rows, sl] = y.astype(ya_ref.dtype)
                y = _readout(o_all[:, 2 * LANE + hh * LANE:2 * LANE + (hh + 1) * LANE], hg_ref[rows, sl],
                             hn_ref[...])
                yb_ref[rows, sl] = y.astype(yb_ref.dtype)


def _gated_scans(p, cos_t, sin_t, wg, bg, lb, gla_norm, hg_norm, batch, n_lat_blk):
    assert CHUNK == HEAD_DIM == LANE and GLA_HEADS == HG_HEADS and GLA_HEADS % 2 == 0
    m = p.shape[0]
    n_pairs = GLA_HEADS // 2
    n_blk = n_lat_blk + 1
    col = P_COLS

    def run(reverse, extra_in, extra_specs, out_shape, out_specs):
        def pos(t):
            lat = (n_lat_blk - t) if reverse else (t - 1)
            return jnp.where(t == 0, n_lat_blk, lat)

        def row_blk(b, t):
            return jnp.where(t == 0, batch * n_lat_blk + b, b * n_lat_blk + pos(t))

        def cols(start, width):
            return pl.BlockSpec((ROW_BLK, width), lambda b, h, t: (row_blk(b, t), start // width + h))

        d = 1 if reverse else 0
        tri, mask = _scan_constants(reverse)
        hw = HG_HEADS * HEAD_DIM
        in_specs = [
            cols(col["gla_q"], 2 * GLA_DK), cols(col["gla_k"], 2 * GLA_DK), cols(col["gla_v"], 2 * HEAD_DIM),
            cols(col["hg_q"], 2 * HEAD_DIM), cols(col["hg_f"] + d * hw, 2 * HEAD_DIM),
            cols(col["hg_i"], 2 * HEAD_DIM),
            pl.BlockSpec((ROW_BLK, LANE), lambda b, h, t: (row_blk(b, t), col["a_low"] // LANE)),
            pl.BlockSpec((ROW_BLK, LANE), lambda b, h, t: (pos(t), 0)),
            pl.BlockSpec((ROW_BLK, LANE), lambda b, h, t: (pos(t), 0)),
            pl.BlockSpec((None, None, LANE, LANE), lambda b, h, t: (d, h, 0, 0)),
            pl.BlockSpec((None, None, 1, LANE), lambda b, h, t: (d, h, 0, 0)),
            pl.BlockSpec((None, None, 1, 2 * LANE), lambda b, h, t: (d, h, 0, 0)),
            pl.BlockSpec((CHUNK, CHUNK), lambda b, h, t: (0, 0)),
            pl.BlockSpec((N_LEVELS + 1, 2 * CHUNK, CHUNK), lambda b, h, t: (0, 0, 0)),
        ] + [s(row_blk, cols) for s in extra_specs]
        return pl.pallas_call(
            functools.partial(_scan_kernel, reverse=reverse),
            grid=(batch, n_pairs, n_blk),
            in_specs=in_specs,
            out_specs=[s(row_blk, cols) for s in out_specs],
            out_shape=out_shape,
            scratch_shapes=[
                pltpu.VMEM((3 * ROW_BLK // CHUNK, CHUNK, LANE), F32),
                pltpu.VMEM((2 * HEAD_DIM, LANE), F32),
                pltpu.VMEM((2, HEAD_DIM, HEAD_DIM), F32),
            ],
            compiler_params=pltpu.CompilerParams(vmem_limit_bytes=VMEM_LIMIT),
            name="gated_scan_rev" if reverse else "gated_scan_fwd",
        )(p, p, p, p, p, p, p, cos_t, sin_t, wg, bg, lb, tri, mask, *extra_in)

    def rows_spec(width):
        return lambda row_blk, cols: pl.BlockSpec((ROW_BLK, width), lambda b, h, t: (row_blk(b, t), h))

    def section_spec(name):
        return lambda row_blk, cols: cols(col[name], 2 * HEAD_DIM)

    def const_spec(row_blk, cols):
        return pl.BlockSpec((1, LANE), lambda b, h, t: (0, 0))

    (o_rev,) = run(True, (), (), [jax.ShapeDtypeStruct((m, 2 * n_pairs * 2 * HEAD_DIM), F32)],
                   [rows_spec(4 * LANE)])
    return run(False, (p, p, o_rev, gla_norm.reshape(1, -1), hg_norm.reshape(1, -1)),
               (section_spec("gla_r"), section_spec("hg_g"), rows_spec(4 * LANE), const_spec, const_spec),
               [jax.ShapeDtypeStruct((m, GLA_HEADS * HEAD_DIM), BF16),
                jax.ShapeDtypeStruct((m, HG_HEADS * HEAD_DIM), BF16)],
               [rows_spec(2 * LANE), rows_spec(2 * LANE)])


def _head_norm(x, g):
    return x * lax.rsqrt(jnp.mean(x * x, axis=-1, keepdims=True) + EPS) * g


def _na_kernel(q_ref, kl_ref, vl_ref, kc_ref, vc_ref, qn_ref, kn_ref, bias_ref, y_ref,
               kls_ref, vls_ref, kcs_ref, vcs_ref, *, n_rows):
    t = pl.program_id(2)
    scale = HEAD_DIM ** -0.5

    @pl.when(t == 0)
    def _():
        kls_ref[...] = _head_norm(kl_ref[...], kn_ref[...]).astype(BF16)
        kcs_ref[...] = _head_norm(kc_ref[...], kn_ref[...]).astype(BF16)
        vls_ref[...] = vl_ref[...].astype(BF16)
        vcs_ref[...] = vc_ref[...].astype(BF16)

    q = _head_norm(q_ref[...], qn_ref[...]).astype(BF16)
    kc = kcs_ref[...]
    vc = vcs_ref[...]

    @pl.when(t == 0)
    def _():
        s = _dot_nt(q, kc) * scale
        e = jnp.exp(s - jnp.max(s, axis=-1, keepdims=True))
        o = _dot(e.astype(BF16), vc) / jnp.sum(e, axis=-1, keepdims=True)
        y_ref[...] = o.astype(y_ref.dtype)

    @pl.when(t > 0)
    def _():
        n = t - 1
        base = _na_span_base(n, n_rows)
        variant = jnp.where(n == 0, 0, jnp.where(n == n_rows // NA_Q_ROWS - 1, 2, 1))
        span = pl.ds(pl.multiple_of(base * GRID_W, GRID_W), NA_SPAN_ROWS * GRID_W)
        s_win = _dot_nt(q, kls_ref[span, :]) * scale + bias_ref[variant]
        s_ctx = _dot_nt(q, kc) * scale
        mx = jnp.maximum(jnp.max(s_win, axis=-1, keepdims=True), jnp.max(s_ctx, axis=-1, keepdims=True))
        e_win = jnp.exp(s_win - mx)
        e_ctx = jnp.exp(s_ctx - mx)
        den = jnp.sum(e_win, axis=-1, keepdims=True) + jnp.sum(e_ctx, axis=-1, keepdims=True)
        o = _dot(e_win.astype(BF16), vls_ref[span, :]) + _dot(e_ctx.astype(BF16), vc)
        y_ref[...] = (o / den).astype(y_ref.dtype)


NA_Q_ROWS = ROW_BLK // GRID_W
NA_SPAN_ROWS = 12


def _na_span_base(n, n_rows, xp=jnp):
    return xp.clip(n * NA_Q_ROWS - WIN_ROWS // 2, 0, n_rows - NA_SPAN_ROWS)


def _na_bias_table(rpb, n_rows):
    c = np.arange(GRID_W)[:, None]
    kc = np.arange(GRID_W)[None, :]
    c0 = np.clip(c - WIN_COLS // 2, 0, GRID_W - WIN_COLS)
    col_valid = (kc >= c0) & (kc < c0 + WIN_COLS)
    pad = GRID_W - WIN_COLS
    padded = jnp.pad(rpb.astype(F32), ((0, 0), (0, 0), (pad, pad)))
    col = jnp.stack([padded[:, :, GRID_W - 1 - ci:2 * GRID_W - 1 - ci] for ci in range(GRID_W)], axis=2)
    col = jnp.where(col_valid[None, None], col, NEG_BIG)
    masked = jnp.full_like(col[:, 0], NEG_BIG)

    n_blocks = n_rows // NA_Q_ROWS
    assert n_rows % NA_Q_ROWS == 0 and n_blocks >= 4

    def row_offsets(n):
        base = int(_na_span_base(n, n_rows, np))
        out = []
        for i in range(NA_Q_ROWS):
            r = n * NA_Q_ROWS + i
            r0 = min(max(r - WIN_ROWS // 2, 0), n_rows - WIN_ROWS)
            out.append([(base + kk - r + WIN_ROWS - 1) if r0 <= base + kk < r0 + WIN_ROWS else None
                        for kk in range(NA_SPAN_ROWS)])
        return out

    variants = [row_offsets(0), row_offsets(1), row_offsets(n_blocks - 1)]
    assert all(row_offsets(n) == variants[1] for n in range(1, n_blocks - 1))
    blocks = [jnp.concatenate([masked if dr is None else col[:, dr] for dr in row], axis=-1)
              for var in variants for row in var]
    tbl = jnp.stack(blocks, axis=1)
    return tbl.reshape(rpb.shape[0], len(variants), ROW_BLK, NA_SPAN_ROWS * GRID_W)


def _neighbourhood_attention(p, q_norm, k_norm, bias_tbl, batch, n_lat_blk):
    m = p.shape[0]
    lat_rows = n_lat_blk * ROW_BLK
    qb, kb, vb = (P_COLS[name] // LANE for name in ("na_q", "na_k", "na_v"))
    ctx_blk = batch * n_lat_blk

    def q_blk(b, t):
        return jnp.where(t == 0, ctx_blk + b, b * n_lat_blk + t - 1)

    kern = functools.partial(_na_kernel, n_rows=lat_rows // GRID_W)
    return pl.pallas_call(
        kern,
        grid=(batch, NA_HEADS, n_lat_blk + 1),
        in_specs=[
            pl.BlockSpec((ROW_BLK, LANE), lambda b, h, t: (q_blk(b, t), qb + h)),
            pl.BlockSpec((lat_rows, LANE), lambda b, h, t: (b, kb + h)),
            pl.BlockSpec((lat_rows, LANE), lambda b, h, t: (b, vb + h)),
            pl.BlockSpec((ROW_BLK, LANE), lambda b, h, t: (ctx_blk + b, kb + h)),
            pl.BlockSpec((ROW_BLK, LANE), lambda b, h, t: (ctx_blk + b, vb + h)),
            pl.BlockSpec((1, LANE), lambda b, h, t: (0, 0)),
            pl.BlockSpec((1, LANE), lambda b, h, t: (0, 0)),
            pl.BlockSpec((None, 3, ROW_BLK, NA_SPAN_ROWS * GRID_W), lambda b, h, t: (h, 0, 0, 0)),
        ],
        out_specs=pl.BlockSpec((ROW_BLK, LANE), lambda b, h, t: (q_blk(b, t), h)),
        out_shape=jax.ShapeDtypeStruct((m, NA_HEADS * HEAD_DIM), BF16),
        scratch_shapes=[
            pltpu.VMEM((lat_rows, HEAD_DIM), BF16),
            pltpu.VMEM((lat_rows, HEAD_DIM), BF16),
            pltpu.VMEM((ROW_BLK, HEAD_DIM), BF16),
            pltpu.VMEM((ROW_BLK, HEAD_DIM), BF16),
        ],
        compiler_params=pltpu.CompilerParams(vmem_limit_bytes=VMEM_LIMIT),
        name="neighbourhood_attention",
    )(p, p, p, p, p, q_norm.reshape(1, -1), k_norm.reshape(1, -1), bias_tbl)


def _outproj_kernel(xl_ref, xc_ref, ya_ref, yb_ref, yc_ref, mod_ref, wa_ref, wb_ref, wc_ref, o_ref, *, n_lat):
    i = pl.program_id(0)
    acc = _dot(ya_ref[...], wa_ref[...]) + _dot(yb_ref[...], wb_ref[...]) + _dot(yc_ref[...], wc_ref[...])
    upd = mod_ref[2:3, :] * acc

    def emit(x_ref):
        o_ref[...] = x_ref[...] + upd

    pl.when(i < n_lat)(lambda: emit(xl_ref))
    pl.when(i >= n_lat)(lambda: emit(xc_ref))


def _out_projection(tok, ya, yb, yc, modt, w_out, rows_per_mod, with_ctx):
    d = w_out.shape[1]
    tm = TOK_BLK
    per = rows_per_mod // tm
    n_blk = tok.n_lat + (1 if with_ctx else 0)
    wa, wb, wc = ya.shape[1], yb.shape[1], yc.shape[1]
    return pl.pallas_call(
        functools.partial(_outproj_kernel, n_lat=tok.n_lat),
        grid=(n_blk,),
        in_specs=tok.specs(d) + [
            pl.BlockSpec((tm, wa), lambda i: (i, 0)),
            pl.BlockSpec((tm, wb), lambda i: (i, 0)),
            pl.BlockSpec((tm, wc), lambda i: (i, 0)),
            pl.BlockSpec((None, 6, d), lambda i: (i // per, 0, 0)),
            pl.BlockSpec((wa, d), lambda i: (0, 0)),
            pl.BlockSpec((wb, d), lambda i: (wa // wb, 0)),
            pl.BlockSpec((wc, d), lambda i: ((wa + wb) // wc, 0)),
        ],
        out_specs=pl.BlockSpec((tm, d), lambda i: (i, 0)),
        out_shape=jax.ShapeDtypeStruct((n_blk * tm, d), F32),
        compiler_params=pltpu.CompilerParams(vmem_limit_bytes=VMEM_LIMIT),
        name="out_projection",
    )(tok.lat, tok.ctx, ya, yb, yc, modt, w_out, w_out, w_out)


def _mlp_kernel(x_ref, mod_ref, g_ref, w1_ref, w2_ref, o_ref, h_ref, acc_ref):
    j = pl.program_id(1)

    @pl.when(j == 0)
    def _():
        h = _norm_mod(x_ref[...], g_ref[...], mod_ref[3:4, :], mod_ref[4:5, :])
        h_ref[...] = h.astype(BF16)
        acc_ref[...] = jnp.zeros_like(acc_ref)

    a = jnp.maximum(_dot(h_ref[...], w1_ref[...]), 0.0)
    acc_ref[...] += _dot((a * a).astype(BF16), w2_ref[...])

    @pl.when(j == pl.num_programs(1) - 1)
    def _():
        o_ref[...] = x_ref[...] + mod_ref[5:6, :] * acc_ref[...]


def _mlp(xu, modt, g, w1, w2, rows_per_mod):
    m, d = xu.shape
    ff = w1.shape[1]
    tm, tf = 512, 1024
    per = rows_per_mod // tm
    return pl.pallas_call(
        _mlp_kernel,
        grid=(m // tm, ff // tf),
        in_specs=[
            pl.BlockSpec((tm, d), lambda i, j: (i, 0)),
            pl.BlockSpec((None, 6, d), lambda i, j: (i // per, 0, 0)),
            pl.BlockSpec((1, d), lambda i, j: (0, 0)),
            pl.BlockSpec((d, tf), lambda i, j: (0, j)),
            pl.BlockSpec((tf, d), lambda i, j: (j, 0)),
        ],
        out_specs=pl.BlockSpec((tm, d), lambda i, j: (i, 0)),
        out_shape=jax.ShapeDtypeStruct((m, d), F32),
        scratch_shapes=[pltpu.VMEM((tm, d), BF16), pltpu.VMEM((tm, d), F32)],
        compiler_params=pltpu.CompilerParams(vmem_limit_bytes=VMEM_LIMIT),
        name="mlp",
    )(xu, modt, g.reshape(1, d), w1, w2)


def _permute_w_in(w_in):
    al, aw = P_SRC["a_low"]
    parts = [w_in[:, :al], w_in[:, al + aw:], w_in[:, al:al + aw],
             jnp.zeros((w_in.shape[0], LANE - aw), w_in.dtype)]
    out = jnp.concatenate([q.astype(BF16) for q in parts], axis=-1)
    assert out.shape[1] == P_WIDTH
    return out


def _gla_gate_weights(w_a2, b_a):
    n_pairs = GLA_HEADS // 2
    wg = jnp.zeros((2, n_pairs, LANE, 2 * GLA_DK), F32)
    for dd in range(2):
        blk = w_a2[dd].reshape(GLA_RANK, n_pairs, 2 * GLA_DK).transpose(1, 0, 2)
        wg = wg.at[dd, :, dd * GLA_RANK:(dd + 1) * GLA_RANK, :].set(blk)
    bg = b_a.reshape(2, n_pairs, 1, 2 * GLA_DK)
    return wg.astype(BF16), bg.astype(F32)


def _rope_tables(seq, n_ctx):
    quarter = GLA_DK // 4
    inv_freq = ROPE_BASE ** (-np.arange(quarter, dtype=np.float64) / quarter)
    pos = np.arange(seq)
    lane = np.arange(LANE)
    is_col = (lane % GLA_DK) // (GLA_DK // 2) == 1
    first = (lane % (GLA_DK // 2)) < quarter
    p = np.where(is_col[None, :], (pos % GRID_W)[:, None], (pos // GRID_W)[:, None]).astype(np.float64)
    ang = p * inv_freq[lane % quarter][None, :]
    cos = np.cos(ang)
    sin = np.where(first[None, :], -np.sin(ang), np.sin(ang))
    cos = np.concatenate([cos, np.ones((n_ctx, LANE))], axis=0)
    sin = np.concatenate([sin, np.zeros((n_ctx, LANE))], axis=0)
    return jnp.asarray(cos, F32), jnp.asarray(sin, F32)


def kernel(x, c, ctx, c_ctx, w_mod, b_mod, attn_norm, w_in, gla_w_a2, gla_b_a, gla_norm, hg_lower_bounds, hg_norm, na_q_norm, na_k_norm, na_rpb, w_out, mlp_norm, w_mlp1, w_mlp2):
    batch, seq, d = x.shape
    n_ctx = ctx.shape[1]
    depth = w_mod.shape[0]
    assert seq % ROW_BLK == 0 and n_ctx == ROW_BLK and batch * n_ctx == TOK_BLK and d % LANE == 0
    assert seq % TOK_BLK == 0 and seq % GRID_W == 0
    n_lat_blk = seq // ROW_BLK
    lat_rows = batch * seq
    n_lat_tok = lat_rows // TOK_BLK

    tok = _Tokens(x.reshape(lat_rows, d), ctx.reshape(batch * n_ctx, d), n_lat_tok, 0)

    cc = jnp.zeros((8, d), F32).at[0:batch].set(c).at[batch].set(c_ctx)
    mod = _modulation(cc, w_mod, b_mod)
    modt = mod[:, 0:batch + 1].reshape(depth, batch + 1, 6, d)

    lb_p = jax.nn.softmax(hg_lower_bounds.astype(F32), axis=0)
    lower = jnp.cumsum(lb_p, axis=0) - lb_p[0]
    lower = lower.reshape(depth, 2, HG_HEADS // 2, 1, 2 * HEAD_DIM)
    cos_t, sin_t = _rope_tables(seq, n_ctx)

    for l in range(depth):
        last = l == depth - 1
        p = _in_projection(tok, modt[l], attn_norm[l], _permute_w_in(w_in[l]), seq)
        wg, bg = _gla_gate_weights(gla_w_a2[l], gla_b_a[l])
        ya, yb = _gated_scans(p, cos_t, sin_t, wg, bg, lower[l], gla_norm[l], hg_norm[l], batch, n_lat_blk)
        yc = _neighbourhood_attention(p, na_q_norm[l], na_k_norm[l], _na_bias_table(na_rpb[l], seq // GRID_W),
                                      batch, n_lat_blk)
        xu = _out_projection(tok, ya, yb, yc, modt[l], w_out[l].astype(BF16), seq, with_ctx=not last)
        xu = _mlp(xu, modt[l], mlp_norm[l], w_mlp1[l].astype(BF16), w_mlp2[l].astype(BF16), seq)
        tok = _Tokens(xu, xu, n_lat_tok, n_lat_tok)
    return xu.reshape(batch, seq, d)
```

```python
import functools

import numpy as np
import jax
import jax.numpy as jnp
from jax import lax
from jax.experimental import pallas as pl
from jax.experimental.pallas import tpu as pltpu

F32 = jnp.float32
BF16 = jnp.bfloat16

EPS = 1e-6
GRID_W = 64
HEAD_DIM = 128
GLA_HEADS = 4
GLA_DK = 64
GLA_RANK = 16
GLA_TAU = 16.0
HG_HEADS = 4
NA_HEADS = 8
WIN_ROWS = 8
WIN_COLS = 16
ROPE_BASE = 10000.0
NEG_BIG = -1e30

LANE = 128
ROW_BLK = 256
CHUNK = 128
N_LEVELS = 7
TOK_BLK = 512
VMEM_LIMIT = 48 * 1024 * 1024
VMEM_LIMIT_BIG = 56 * 1024 * 1024


def _projection_layout():
    gw, hw, nw = GLA_HEADS * HEAD_DIM, HG_HEADS * HEAD_DIM, NA_HEADS * HEAD_DIM
    sections = (("gla_q", GLA_HEADS * GLA_DK), ("gla_k", GLA_HEADS * GLA_DK), ("gla_v", gw), ("gla_r", gw),
                ("a_low", 2 * GLA_RANK), ("hg_q", hw), ("hg_f", 2 * hw), ("hg_i", hw), ("hg_g", hw),
                ("na_q", nw), ("na_k", nw), ("na_v", nw))
    src, off = {}, 0
    for name, width in sections:
        src[name] = (off, width)
        off += width
    dst, off = {}, 0
    for name, width in sections:
        if name != "a_low":
            dst[name] = off
            off += width
    dst["a_low"] = off
    return src, dst, off + LANE


P_SRC, P_COLS, P_WIDTH = _projection_layout()


def _dot(a, b):
    return jnp.dot(a, b, preferred_element_type=F32)


def _dot_nt(a, b):
    return lax.dot_general(a, b, (((1,), (1,)), ((), ())), preferred_element_type=F32)


def _dot_tn(a, b):
    return lax.dot_general(a, b, (((0,), (0,)), ((), ())), preferred_element_type=F32)


def _sigmoid(x):
    return 1.0 / (1.0 + jnp.exp(-x))


def _silu(x):
    return x * _sigmoid(x)


def _split3(x):
    hi = x.astype(BF16)
    r1 = x - hi.astype(F32)
    mid = r1.astype(BF16)
    lo = (r1 - mid.astype(F32)).astype(BF16)
    return jnp.concatenate([hi, mid, lo], axis=-1)


def _sum3(y, w):
    return y[:, 0:w] + y[:, w:2 * w] + y[:, 2 * w:3 * w]


def _mod_kernel(c_ref, w_ref, b_ref, o_ref):
    s = _silu(c_ref[...]).astype(BF16)
    o_ref[...] = _dot(s, w_ref[...].astype(BF16)) + b_ref[...]


def _modulation(cc, w_mod, b_mod):
    depth, d, n = w_mod.shape
    tn = 1024
    return pl.pallas_call(
        _mod_kernel,
        grid=(depth, n // tn),
        in_specs=[
            pl.BlockSpec((8, d), lambda l, j: (0, 0)),
            pl.BlockSpec((None, d, tn), lambda l, j: (l, 0, j)),
            pl.BlockSpec((None, 1, tn), lambda l, j: (l, 0, j)),
        ],
        out_specs=pl.BlockSpec((None, 8, tn), lambda l, j: (l, 0, j)),
        out_shape=jax.ShapeDtypeStruct((depth, 8, n), F32),
        compiler_params=pltpu.CompilerParams(vmem_limit_bytes=VMEM_LIMIT),
        name="modulation",
    )(cc, w_mod, b_mod.reshape(depth, 1, n))


def _norm_mod(x, g, shift, scale):
    y = x * lax.rsqrt(jnp.mean(x * x, axis=-1, keepdims=True) + EPS) * g
    return y * (1.0 + scale) + shift


class _Tokens:
    def __init__(self, lat, ctx, n_lat, ctx_blk):
        self.lat, self.ctx, self.n_lat, self.ctx_blk = lat, ctx, n_lat, ctx_blk

    def specs(self, d):
        n_lat, ctx_blk = self.n_lat, self.ctx_blk
        return [pl.BlockSpec((TOK_BLK, d), lambda i, *_: (jnp.minimum(i, n_lat - 1), 0)),
                pl.BlockSpec((TOK_BLK, d), lambda i, *_: (ctx_blk, 0))]


def _first_norm_kernel(xl_ref, xc_ref, mod_ref, g_ref, h_ref, *, n_lat):
    i = pl.program_id(0)

    def emit(x_ref):
        h_ref[...] = _norm_mod(x_ref[...], g_ref[...], mod_ref[0:1, :], mod_ref[1:2, :]).astype(h_ref.dtype)

    pl.when(i < n_lat)(lambda: emit(xl_ref))
    pl.when(i >= n_lat)(lambda: emit(xc_ref))


def _first_norm(tok, modt, g, rows_per_mod):
    d = tok.lat.shape[1]
    per = rows_per_mod // TOK_BLK
    n_blk = tok.n_lat + 1
    return pl.pallas_call(
        functools.partial(_first_norm_kernel, n_lat=tok.n_lat),
        grid=(n_blk,),
        in_specs=tok.specs(d) + [
            pl.BlockSpec((None, 6, d), lambda i: (i // per, 0, 0)),
            pl.BlockSpec((1, d), lambda i: (0, 0)),
        ],
        out_specs=pl.BlockSpec((TOK_BLK, d), lambda i: (i, 0)),
        out_shape=jax.ShapeDtypeStruct((n_blk * TOK_BLK, d), BF16),
        compiler_params=pltpu.CompilerParams(vmem_limit_bytes=VMEM_LIMIT),
        name="first_norm",
    )(tok.lat, tok.ctx, modt, g.reshape(1, d))


def _inproj_kernel(h_ref, w_ref, o_ref):
    o_ref[...] = _dot(h_ref[...], w_ref[...])


def _in_projection(h, w, layer):
    m, d = h.shape
    n = w.shape[2]
    tn = n // 3
    return pl.pallas_call(
        _inproj_kernel,
        grid=(n // tn, m // TOK_BLK),
        in_specs=[
            pl.BlockSpec((TOK_BLK, d), lambda j, i: (i, 0)),
            pl.BlockSpec((None, d, tn), lambda j, i: (layer, 0, j)),
        ],
        out_specs=pl.BlockSpec((TOK_BLK, tn), lambda j, i: (i, j)),
        out_shape=jax.ShapeDtypeStruct((m, n), F32),
        compiler_params=pltpu.CompilerParams(vmem_limit_bytes=VMEM_LIMIT),
        name="in_projection",
    )(h, w)


def _scan_constants(reverse):
    c = CHUNK
    i = np.arange(c)[:, None]
    j = np.arange(c)[None, :]
    tri = (j >= i) if reverse else (j <= i)
    mask = np.zeros((N_LEVELS + 1, c, c), np.float32)
    for l in range(N_LEVELS):
        level = ((i ^ j) >> l) == 1
        mask[l] = level & ((i < j) if reverse else (i > j))
    mask[N_LEVELS] = np.eye(c, dtype=np.float32)
    return jnp.asarray(tri, BF16), jnp.asarray(np.concatenate([mask, mask], axis=1))


def _level_exponents(g, cum, scr_ref, reverse):
    c = CHUNK
    scr_ref[...] = cum
    row = lax.broadcasted_iota(jnp.int32, (c, LANE), 0)
    up = pltpu.roll(g, c - 1, 0)
    dn = pltpu.roll(g, 1, 0)
    m2 = row & 1
    m4 = row & 3
    if reverse:
        e0 = jnp.where(m2 == 0, g, 0.0)
        e1 = jnp.where(m4 == 0, g + up, jnp.where(m4 == 1, g, jnp.where(m4 == 2, 0.0, dn)))
    else:
        e0 = jnp.where(m2 == 1, g, 0.0)
        e1 = jnp.where(m4 == 0, up, jnp.where(m4 == 1, 0.0, jnp.where(m4 == 2, g, g + dn)))
    out = [e0, e1]
    for l in range(2, N_LEVELS):
        s = 1 << l
        pieces = []
        for blk in range(c // (2 * s)):
            lo = blk * 2 * s
            r = lo + (s if reverse else s - 1)
            pieces.append(-jnp.abs(cum[lo:lo + 2 * s] - scr_ref[r:r + 1, :]))
        out.append(pieces[0] if len(pieces) == 1 else jnp.concatenate(pieces, axis=0))
    return out


def _pair_weights(q, k, exps, mask_ref, lane_masks):
    if lane_masks is None:
        n = CHUNK

        def lhs(x):
            return x.astype(BF16)
    else:
        n = 2 * CHUNK

        def lhs(x):
            return jnp.concatenate([jnp.where(m, x, 0.0).astype(BF16) for m in lane_masks], axis=0)

    att = mask_ref[N_LEVELS, 0:n, :] * _dot_nt(lhs(q), k.astype(BF16))
    for l, e in enumerate(exps):
        w = jnp.exp(e)
        att = att + mask_ref[l, 0:n, :] * _dot_nt(lhs(q * w), (k * w).astype(BF16))
    return att


def _readout(o, gate, g):
    y = o * lax.rsqrt(jnp.mean(o * o, axis=-1, keepdims=True) + EPS) * g
    return y * _silu(gate)


def _scan_kernel(*refs, reverse):
    (gq_ref, gk_ref, gv_ref, hq_ref, hf_ref, hi_ref, al_ref, cos_ref, sin_ref, wg_ref, bg_ref, lb_ref,
     tri_ref, mask_ref) = refs[:14]
    if reverse:
        o_ref, scr_ref, sg_ref, sh_ref = refs[14:]
    else:
        gr_ref, hg_ref, orev_ref, gn_ref, hn_ref, ya_ref, yb_ref, scr_ref, sg_ref, sh_ref = refs[14:]
    c = CHUNK

    @pl.when(pl.program_id(2) == 0)
    def _():
        sg_ref[...] = jnp.zeros_like(sg_ref)
        sh_ref[...] = jnp.zeros_like(sh_ref)

    lane = lax.broadcasted_iota(jnp.int32, (c, LANE), 1)
    first_half = (lane % (GLA_DK // 2)) < GLA_DK // 4
    head_a = lane < GLA_DK
    lane_masks = (head_a, jnp.logical_not(head_a))
    state_mask = jnp.concatenate([head_a, jnp.logical_not(head_a)], axis=0)
    end_row = 0 if reverse else c - 1
    n_chunks = ROW_BLK // c
    offsets = [(n_chunks - 1 - ci) * c if reverse else ci * c for ci in range(n_chunks)]

    def rope(x, rows):
        swapped = jnp.where(first_half, pltpu.roll(x, LANE - GLA_DK // 4, 1), pltpu.roll(x, GLA_DK // 4, 1))
        return x * cos_ref[rows, :] + swapped * sin_ref[rows, :]

    chains = []
    for off in offsets:
        rows = slice(off, off + c)
        q = rope(gq_ref[rows, :], rows) * (GLA_DK ** -0.5)
        k = rope(gk_ref[rows, :], rows)
        logit = _dot(al_ref[rows, :].astype(BF16), wg_ref[...]) + bg_ref[...]
        g = (jnp.minimum(logit, 0.0) - jnp.log(1.0 + jnp.exp(-jnp.abs(logit)))) / GLA_TAU
        chains.append((q, k, g))
        for hh in range(2):
            head = slice(hh * LANE, (hh + 1) * LANE)
            lb = lb_ref[:, head]
            logit = hf_ref[rows, head]
            e = jnp.exp(-jnp.abs(logit))
            r = 1.0 / (1.0 + e)
            sig = jnp.where(logit >= 0, r, e * r)
            sig_neg = jnp.where(logit >= 0, e * r, r)
            g = jnp.log(lb + (1.0 - lb) * sig)
            k = (1.0 - lb) * sig_neg
            q = _silu(hq_ref[rows, head])
            chains.append((q, k, g))

    g3 = jnp.concatenate([_split3(g) for (_, _, g) in chains], axis=-1)
    cum_all = _dot(tri_ref[...], g3)
    cums = [_sum3(cum_all[:, 3 * LANE * n:3 * LANE * (n + 1)], LANE) for n in range(len(chains))]

    atts = []
    for n, ((q, k, g), cum) in enumerate(zip(chains, cums)):
        exps = _level_exponents(g, cum, scr_ref.at[n], reverse)
        atts.append(_pair_weights(q, k, exps, mask_ref, lane_masks if n % 3 == 0 else None))

    for ci, off in enumerate(offsets):
        rows = slice(off, off + c)
        (q, k, _), cum, att = chains[3 * ci], cums[3 * ci], atts[3 * ci].astype(BF16)
        v = gv_ref[rows, :].astype(BF16)
        cum_end = cum[end_row:end_row + 1, :]
        st = sg_ref[...]
        inter = _dot_nt((q * jnp.exp(cum)).astype(BF16), st.astype(BF16))
        o_gla = jnp.concatenate([_dot(att[0:c], v[:, 0:LANE]), _dot(att[c:2 * c], v[:, LANE:2 * LANE])],
                                axis=-1) + inter
        kd = (k * jnp.exp(cum_end - cum)).astype(BF16)
        sg_ref[...] = st * jnp.exp(cum_end) + jnp.where(state_mask, _dot_tn(v, kd), 0.0)
        o_hg = []
        for hh in range(2):
            (q, k, _), cum, att = chains[3 * ci + 1 + hh], cums[3 * ci + 1 + hh], atts[3 * ci + 1 + hh]
            v = hi_ref[rows, hh * LANE:(hh + 1) * LANE].astype(BF16)
            cum_end = cum[end_row:end_row + 1, :]
            st = sh_ref[hh]
            o_hg.append(_dot(att.astype(BF16), v) + _dot_nt((q * jnp.exp(cum)).astype(BF16), st.astype(BF16)))
            kd = (k * jnp.exp(cum_end - cum)).astype(BF16)
            sh_ref[hh] = st * jnp.exp(cum_end) + _dot_tn(v, kd)
        o_all = jnp.concatenate([o_gla] + o_hg, axis=-1)
        if reverse:
            o_ref[rows, :] = o_all
        else:
            o_all = o_all + orev_ref[rows, :]
            for hh in range(2):
                sl = slice(hh * LANE, (hh + 1) * LANE)
                y = _readout(o_all[:, sl], gr_ref[rows, sl], gn_ref[...])
                ya_ref[rows, sl] = y.astype(ya_ref.dtype)
                y = _readout(o_all[:, 2 * LANE + hh * LANE:2 * LANE + (hh + 1) * LANE], hg_ref[rows, sl],
                             hn_ref[...])
                yb_ref[rows, sl] = y.astype(yb_ref.dtype)


def _gated_scans(p, cos_t, sin_t, wg, bg, lb, gla_norm, hg_norm, batch, n_lat_blk):
    assert CHUNK == HEAD_DIM == LANE and GLA_HEADS == HG_HEADS and GLA_HEADS % 2 == 0
    m = p.shape[0]
    n_pairs = GLA_HEADS // 2
    n_blk = n_lat_blk + 1
    col = P_COLS

    def run(reverse, extra_in, extra_specs, out_shape, out_specs):
        def pos(t):
            lat = (n_lat_blk - t) if reverse else (t - 1)
            return jnp.where(t == 0, n_lat_blk, lat)

        def row_blk(b, t):
            return jnp.where(t == 0, batch * n_lat_blk + b, b * n_lat_blk + pos(t))

        def cols(start, width):
            return pl.BlockSpec((ROW_BLK, width), lambda b, h, t: (row_blk(b, t), start // width + h))

        d = 1 if reverse else 0
        tri, mask = _scan_constants(reverse)
        hw = HG_HEADS * HEAD_DIM
        in_specs = [
            cols(col["gla_q"], 2 * GLA_DK), cols(col["gla_k"], 2 * GLA_DK), cols(col["gla_v"], 2 * HEAD_DIM),
            cols(col["hg_q"], 2 * HEAD_DIM), cols(col["hg_f"] + d * hw, 2 * HEAD_DIM),
            cols(col["hg_i"], 2 * HEAD_DIM),
            pl.BlockSpec((ROW_BLK, LANE), lambda b, h, t: (row_blk(b, t), col["a_low"] // LANE)),
            pl.BlockSpec((ROW_BLK, LANE), lambda b, h, t: (pos(t), 0)),
            pl.BlockSpec((ROW_BLK, LANE), lambda b, h, t: (pos(t), 0)),
            pl.BlockSpec((None, None, LANE, LANE), lambda b, h, t: (d, h, 0, 0)),
            pl.BlockSpec((None, None, 1, LANE), lambda b, h, t: (d, h, 0, 0)),
            pl.BlockSpec((None, None, 1, 2 * LANE), lambda b, h, t: (d, h, 0, 0)),
            pl.BlockSpec((CHUNK, CHUNK), lambda b, h, t: (0, 0)),
            pl.BlockSpec((N_LEVELS + 1, 2 * CHUNK, CHUNK), lambda b, h, t: (0, 0, 0)),
        ] + [s(row_blk, cols) for s in extra_specs]
        return pl.pallas_call(
            functools.partial(_scan_kernel, reverse=reverse),
            grid=(batch, n_pairs, n_blk),
            in_specs=in_specs,
            out_specs=[s(row_blk, cols) for s in out_specs],
            out_shape=out_shape,
            scratch_shapes=[
                pltpu.VMEM((3 * ROW_BLK // CHUNK, CHUNK, LANE), F32),
                pltpu.VMEM((2 * HEAD_DIM, LANE), F32),
                pltpu.VMEM((2, HEAD_DIM, HEAD_DIM), F32),
            ],
            compiler_params=pltpu.CompilerParams(vmem_limit_bytes=VMEM_LIMIT),
            name="gated_scan_rev" if reverse else "gated_scan_fwd",
        )(p, p, p, p, p, p, p, cos_t, sin_t, wg, bg, lb, tri, mask, *extra_in)

    def rows_spec(width):
        return lambda row_blk, cols: pl.BlockSpec((ROW_BLK, width), lambda b, h, t: (row_blk(b, t), h))

    def section_spec(name):
        return lambda row_blk, cols: cols(col[name], 2 * HEAD_DIM)

    def const_spec(row_blk, cols):
        return pl.BlockSpec((1, LANE), lambda b, h, t: (0, 0))

    (o_rev,) = run(True, (), (), [jax.ShapeDtypeStruct((m, 2 * n_pairs * 2 * HEAD_DIM), F32)],
                   [rows_spec(4 * LANE)])
    return run(False, (p, p, o_rev, gla_norm.reshape(1, -1), hg_norm.reshape(1, -1)),
               (section_spec("gla_r"), section_spec("hg_g"), rows_spec(4 * LANE), const_spec, const_spec),
               [jax.ShapeDtypeStruct((m, GLA_HEADS * HEAD_DIM), BF16),
                jax.ShapeDtypeStruct((m, HG_HEADS * HEAD_DIM), BF16)],
               [rows_spec(2 * LANE), rows_spec(2 * LANE)])


def _head_norm(x, g):
    return x * lax.rsqrt(jnp.mean(x * x, axis=-1, keepdims=True) + EPS) * g


def _na_kernel(q_ref, kl_ref, vl_ref, kc_ref, vc_ref, qn_ref, kn_ref, bias_ref, y_ref,
               kls_ref, vls_ref, kcs_ref, vcs_ref, *, n_rows):
    t = pl.program_id(2)
    scale = HEAD_DIM ** -0.5

    @pl.when(t == 0)
    def _():
        kls_ref[...] = _head_norm(kl_ref[...], kn_ref[...]).astype(BF16)
        kcs_ref[...] = _head_norm(kc_ref[...], kn_ref[...]).astype(BF16)
        vls_ref[...] = vl_ref[...].astype(BF16)
        vcs_ref[...] = vc_ref[...].astype(BF16)

    q = _head_norm(q_ref[...], qn_ref[...]).astype(BF16)
    kc = kcs_ref[...]
    vc = vcs_ref[...]

    @pl.when(t == 0)
    def _():
        s = _dot_nt(q, kc) * scale
        e = jnp.exp(s - jnp.max(s, axis=-1, keepdims=True))
        o = _dot(e.astype(BF16), vc) / jnp.sum(e, axis=-1, keepdims=True)
        y_ref[...] = o.astype(y_ref.dtype)

    @pl.when(t > 0)
    def _():
        n = t - 1
        base = _na_span_base(n, n_rows)
        variant = jnp.where(n == 0, 0, jnp.where(n == n_rows // NA_Q_ROWS - 1, 2, 1))
        span = pl.ds(pl.multiple_of(base * GRID_W, GRID_W), NA_SPAN_ROWS * GRID_W)
        s_win = _dot_nt(q, kls_ref[span, :]) * scale + bias_ref[variant]
        s_ctx = _dot_nt(q, kc) * scale
        mx = jnp.maximum(jnp.max(s_win, axis=-1, keepdims=True), jnp.max(s_ctx, axis=-1, keepdims=True))
        e_win = jnp.exp(s_win - mx)
        e_ctx = jnp.exp(s_ctx - mx)
        den = jnp.sum(e_win, axis=-1, keepdims=True) + jnp.sum(e_ctx, axis=-1, keepdims=True)
        o = _dot(e_win.astype(BF16), vls_ref[span, :]) + _dot(e_ctx.astype(BF16), vc)
        y_ref[...] = (o / den).astype(y_ref.dtype)


NA_Q_ROWS = ROW_BLK // GRID_W
NA_SPAN_ROWS = 12


def _na_span_base(n, n_rows, xp=jnp):
    return xp.clip(n * NA_Q_ROWS - WIN_ROWS // 2, 0, n_rows - NA_SPAN_ROWS)


def _na_bias_table(rpb, n_rows):
    n_heads, n_dr, n_dc = rpb.shape
    c = np.arange(GRID_W)[:, None]
    kc = np.arange(GRID_W)[None, :]
    c0 = np.clip(c - WIN_COLS // 2, 0, GRID_W - WIN_COLS)
    col_valid = (kc >= c0) & (kc < c0 + WIN_COLS)
    pick_dc = (np.arange(n_dc)[:, None, None] == (kc - c + WIN_COLS - 1)[None]) & col_valid[None]
    col = jnp.einsum("hrb,bcq->hrcq", rpb.astype(F32), jnp.asarray(pick_dc, F32), precision=lax.Precision.HIGHEST)
    col = col + jnp.asarray(np.where(col_valid, 0.0, NEG_BIG), F32)
    col = jnp.concatenate([col, jnp.full((n_heads, 1, GRID_W, GRID_W), NEG_BIG, F32)], axis=1)

    n_blocks = n_rows // NA_Q_ROWS
    assert n_rows % NA_Q_ROWS == 0 and n_blocks >= 4

    def row_offsets(n):
        base = int(_na_span_base(n, n_rows, np))
        out = []
        for i in range(NA_Q_ROWS):
            r = n * NA_Q_ROWS + i
            r0 = min(max(r - WIN_ROWS // 2, 0), n_rows - WIN_ROWS)
            out.append([(base + kk - r + WIN_ROWS - 1) if r0 <= base + kk < r0 + WIN_ROWS else None
                        for kk in range(NA_SPAN_ROWS)])
        return out

    variants = [row_offsets(0), row_offsets(1), row_offsets(n_blocks - 1)]
    assert all(row_offsets(n) == variants[1] for n in range(1, n_blocks - 1))
    dr_of = np.array([[[n_dr if dr is None else dr for dr in row] for row in var] for var in variants])
    pick_dr = (dr_of[..., None] == np.arange(n_dr + 1)).astype(np.float32)
    tbl = jnp.einsum("vikr,hrcq->hvickq", jnp.asarray(pick_dr), col, precision=lax.Precision.HIGHEST)
    return tbl.reshape(n_heads, len(variants), ROW_BLK, NA_SPAN_ROWS * GRID_W)


def _neighbourhood_attention(p, q_norm, k_norm, bias_tbl, layer, batch, n_lat_blk):
    m = p.shape[0]
    lat_rows = n_lat_blk * ROW_BLK
    qb, kb, vb = (P_COLS[name] // LANE for name in ("na_q", "na_k", "na_v"))
    ctx_blk = batch * n_lat_blk

    def q_blk(b, t):
        return jnp.where(t == 0, ctx_blk + b, b * n_lat_blk + t - 1)

    kern = functools.partial(_na_kernel, n_rows=lat_rows // GRID_W)
    return pl.pallas_call(
        kern,
        grid=(batch, NA_HEADS, n_lat_blk + 1),
        in_specs=[
            pl.BlockSpec((ROW_BLK, LANE), lambda b, h, t: (q_blk(b, t), qb + h)),
            pl.BlockSpec((lat_rows, LANE), lambda b, h, t: (b, kb + h)),
            pl.BlockSpec((lat_rows, LANE), lambda b, h, t: (b, vb + h)),
            pl.BlockSpec((ROW_BLK, LANE), lambda b, h, t: (ctx_blk + b, kb + h)),
            pl.BlockSpec((ROW_BLK, LANE), lambda b, h, t: (ctx_blk + b, vb + h)),
            pl.BlockSpec((1, LANE), lambda b, h, t: (0, 0)),
            pl.BlockSpec((1, LANE), lambda b, h, t: (0, 0)),
            pl.BlockSpec((None, 3, ROW_BLK, NA_SPAN_ROWS * GRID_W),
                         lambda b, h, t: (layer * NA_HEADS + h, 0, 0, 0)),
        ],
        out_specs=pl.BlockSpec((ROW_BLK, LANE), lambda b, h, t: (q_blk(b, t), h)),
        out_shape=jax.ShapeDtypeStruct((m, NA_HEADS * HEAD_DIM), BF16),
        scratch_shapes=[
            pltpu.VMEM((lat_rows, HEAD_DIM), BF16),
            pltpu.VMEM((lat_rows, HEAD_DIM), BF16),
            pltpu.VMEM((ROW_BLK, HEAD_DIM), BF16),
            pltpu.VMEM((ROW_BLK, HEAD_DIM), BF16),
        ],
        compiler_params=pltpu.CompilerParams(vmem_limit_bytes=VMEM_LIMIT),
        name="neighbourhood_attention",
    )(p, p, p, p, p, q_norm.reshape(1, -1), k_norm.reshape(1, -1), bias_tbl)


def _outproj_kernel(xl_ref, xc_ref, ya_ref, yb_ref, yc_ref, mod_ref, g_ref, wa_ref, wb_ref, wc_ref,
                    o_ref, h_ref, *, n_lat):
    i = pl.program_id(0)
    acc = _dot(ya_ref[...], wa_ref[...]) + _dot(yb_ref[...], wb_ref[...]) + _dot(yc_ref[...], wc_ref[...])
    upd = mod_ref[2:3, :] * acc

    def emit(x_ref):
        x1 = x_ref[...] + upd
        o_ref[...] = x1
        h_ref[...] = _norm_mod(x1, g_ref[...], mod_ref[3:4, :], mod_ref[4:5, :]).astype(h_ref.dtype)

    pl.when(i < n_lat)(lambda: emit(xl_ref))
    pl.when(i >= n_lat)(lambda: emit(xc_ref))


def _out_projection(tok, ya, yb, yc, modt, g_mlp, w_out, layer, rows_per_mod, with_ctx):
    d = w_out.shape[2]
    tm = TOK_BLK
    per = rows_per_mod // tm
    n_blk = tok.n_lat + (1 if with_ctx else 0)
    wa, wb, wc = ya.shape[1], yb.shape[1], yc.shape[1]
    return pl.pallas_call(
        functools.partial(_outproj_kernel, n_lat=tok.n_lat),
        grid=(n_blk,),
        in_specs=tok.specs(d) + [
            pl.BlockSpec((tm, wa), lambda i: (i, 0)),
            pl.BlockSpec((tm, wb), lambda i: (i, 0)),
            pl.BlockSpec((tm, wc), lambda i: (i, 0)),
            pl.BlockSpec((None, 6, d), lambda i: (i // per, 0, 0)),
            pl.BlockSpec((1, d), lambda i: (0, 0)),
            pl.BlockSpec((None, wa, d), lambda i: (layer, 0, 0)),
            pl.BlockSpec((None, wb, d), lambda i: (layer, wa // wb, 0)),
            pl.BlockSpec((None, wc, d), lambda i: (layer, (wa + wb) // wc, 0)),
        ],
        out_specs=[pl.BlockSpec((tm, d), lambda i: (i, 0)), pl.BlockSpec((tm, d), lambda i: (i, 0))],
        out_shape=[jax.ShapeDtypeStruct((n_blk * tm, d), F32), jax.ShapeDtypeStruct((n_blk * tm, d), BF16)],
        compiler_params=pltpu.CompilerParams(vmem_limit_bytes=VMEM_LIMIT),
        name="out_projection",
    )(tok.lat, tok.ctx, ya, yb, yc, modt, g_mlp.reshape(1, d), w_out, w_out, w_out)


def _mlp_kernel(*refs, emit_next):
    if emit_next:
        h_ref, x_ref, mod_ref, w1_ref, w2_ref, gn_ref, modn_ref, o_ref, hn_ref, acc_ref = refs
    else:
        h_ref, x_ref, mod_ref, w1_ref, w2_ref, o_ref, acc_ref = refs
    j = pl.program_id(1)

    @pl.when(j == 0)
    def _():
        acc_ref[...] = jnp.zeros_like(acc_ref)

    a = jnp.maximum(_dot(h_ref[...], w1_ref[...]), 0.0)
    acc_ref[...] += _dot((a * a).astype(BF16), w2_ref[...])

    @pl.when(j == pl.num_programs(1) - 1)
    def _():
        x2 = x_ref[...] + mod_ref[5:6, :] * acc_ref[...]
        o_ref[...] = x2
        if emit_next:
            hn_ref[...] = _norm_mod(x2, gn_ref[...], modn_ref[0:1, :], modn_ref[1:2, :]).astype(hn_ref.dtype)


def _mlp(h, x1, modt, w1, w2, layer, rows_per_mod, next_norm):
    m, d = x1.shape
    ff = w1.shape[2]
    tm, tf = TOK_BLK, 1024
    per = rows_per_mod // tm
    emit_next = next_norm is not None

    def rows(i, j):
        return (i, 0)

    def mod_rows(i, j):
        return (i // per, 0, 0)

    in_specs = [
        pl.BlockSpec((tm, d), rows),
        pl.BlockSpec((tm, d), rows),
        pl.BlockSpec((None, 6, d), mod_rows),
        pl.BlockSpec((None, d, tf), lambda i, j: (layer, 0, j)),
        pl.BlockSpec((None, tf, d), lambda i, j: (layer, j, 0)),
    ]
    args = [h, x1, modt, w1, w2]
    out_specs = [pl.BlockSpec((tm, d), rows)]
    out_shape = [jax.ShapeDtypeStruct((m, d), F32)]
    if emit_next:
        g_next, modt_next = next_norm
        in_specs += [pl.BlockSpec((1, d), lambda i, j: (0, 0)), pl.BlockSpec((None, 6, d), mod_rows)]
        args += [g_next.reshape(1, d), modt_next]
        out_specs.append(pl.BlockSpec((tm, d), rows))
        out_shape.append(jax.ShapeDtypeStruct((m, d), BF16))
    return pl.pallas_call(
        functools.partial(_mlp_kernel, emit_next=emit_next),
        grid=(m // tm, ff // tf),
        in_specs=in_specs,
        out_specs=out_specs,
        out_shape=out_shape,
        scratch_shapes=[pltpu.VMEM((tm, d), F32)],
        compiler_params=pltpu.CompilerParams(vmem_limit_bytes=VMEM_LIMIT_BIG),
        name="mlp",
    )(*args)


def _relayout_kernel(w_ref, o_ref):
    al, aw = P_SRC["a_low"]
    x = w_ref[...]
    pad = jnp.zeros((x.shape[0], LANE - aw), x.dtype)
    o_ref[...] = jnp.concatenate([x[:, :al], x[:, al + aw:], x[:, al:al + aw], pad], axis=1).astype(o_ref.dtype)


def _permute_w_in(w_in):
    depth, d, n = w_in.shape
    tr = 256
    return pl.pallas_call(
        _relayout_kernel,
        grid=(depth, d // tr),
        in_specs=[pl.BlockSpec((None, tr, n), lambda l, i: (l, i, 0))],
        out_specs=pl.BlockSpec((None, tr, P_WIDTH), lambda l, i: (l, i, 0)),
        out_shape=jax.ShapeDtypeStruct((depth, d, P_WIDTH), BF16),
        compiler_params=pltpu.CompilerParams(vmem_limit_bytes=VMEM_LIMIT),
        name="w_in_relayout",
    )(w_in)


def _gla_gate_weights(w_a2, b_a):
    n_pairs = GLA_HEADS // 2
    wg = jnp.zeros((2, n_pairs, LANE, 2 * GLA_DK), F32)
    for dd in range(2):
        blk = w_a2[dd].reshape(GLA_RANK, n_pairs, 2 * GLA_DK).transpose(1, 0, 2)
        wg = wg.at[dd, :, dd * GLA_RANK:(dd + 1) * GLA_RANK, :].set(blk)
    bg = b_a.reshape(2, n_pairs, 1, 2 * GLA_DK)
    return wg.astype(BF16), bg.astype(F32)


def _rope_tables(seq, n_ctx):
    quarter = GLA_DK // 4
    inv_freq = ROPE_BASE ** (-np.arange(quarter, dtype=np.float64) / quarter)
    pos = np.arange(seq)
    lane = np.arange(LANE)
    is_col = (lane % GLA_DK) // (GLA_DK // 2) == 1
    first = (lane % (GLA_DK // 2)) < quarter
    p = np.where(is_col[None, :], (pos % GRID_W)[:, None], (pos // GRID_W)[:, None]).astype(np.float64)
    ang = p * inv_freq[lane % quarter][None, :]
    cos = np.cos(ang)
    sin = np.where(first[None, :], -np.sin(ang), np.sin(ang))
    cos = np.concatenate([cos, np.ones((n_ctx, LANE))], axis=0)
    sin = np.concatenate([sin, np.zeros((n_ctx, LANE))], axis=0)
    return jnp.asarray(cos, F32), jnp.asarray(sin, F32)


def kernel(x, c, ctx, c_ctx, w_mod, b_mod, attn_norm, w_in, gla_w_a2, gla_b_a, gla_norm, hg_lower_bounds, hg_norm, na_q_norm, na_k_norm, na_rpb, w_out, mlp_norm, w_mlp1, w_mlp2):
    batch, seq, d = x.shape
    n_ctx = ctx.shape[1]
    depth = w_mod.shape[0]
    assert seq % ROW_BLK == 0 and n_ctx == ROW_BLK and batch * n_ctx == TOK_BLK and d % LANE == 0
    assert seq % TOK_BLK == 0 and seq % GRID_W == 0
    n_lat_blk = seq // ROW_BLK
    lat_rows = batch * seq
    n_lat_tok = lat_rows // TOK_BLK

    tok = _Tokens(x.reshape(lat_rows, d), ctx.reshape(batch * n_ctx, d), n_lat_tok, 0)

    cc = jnp.zeros((8, d), F32).at[0:batch].set(c).at[batch].set(c_ctx)
    mod = _modulation(cc, w_mod, b_mod)
    modt = mod[:, 0:batch + 1].reshape(depth, batch + 1, 6, d)

    lb_p = jax.nn.softmax(hg_lower_bounds.astype(F32), axis=0)
    lower = jnp.cumsum(lb_p, axis=0) - lb_p[0]
    lower = lower.reshape(depth, 2, HG_HEADS // 2, 1, 2 * HEAD_DIM)
    cos_t, sin_t = _rope_tables(seq, n_ctx)

    bias_tbl = _na_bias_table(na_rpb.reshape((depth * NA_HEADS,) + na_rpb.shape[2:]), seq // GRID_W)
    w_in_b = _permute_w_in(w_in)
    w_out_b, w1_b, w2_b = (w.astype(BF16) for w in (w_out, w_mlp1, w_mlp2))

    h = _first_norm(tok, modt[0], attn_norm[0], seq)
    for l in range(depth):
        last = l == depth - 1
        p = _in_projection(h, w_in_b, l)
        wg, bg = _gla_gate_weights(gla_w_a2[l], gla_b_a[l])
        ya, yb = _gated_scans(p, cos_t, sin_t, wg, bg, lower[l], gla_norm[l], hg_norm[l], batch, n_lat_blk)
        yc = _neighbourhood_attention(p, na_q_norm[l], na_k_norm[l], bias_tbl, l, batch, n_lat_blk)
        x1, h2 = _out_projection(tok, ya, yb, yc, modt[l], mlp_norm[l], w_out_b, l, seq, with_ctx=not last)
        if last:
            (xu,) = _mlp(h2, x1, modt[l], w1_b, w2_b, l, seq, None)
        else:
            xu, h = _mlp(h2, x1, modt[l], w1_b, w2_b, l, seq, (attn_norm[l + 1], modt[l + 1]))
        tok = _Tokens(xu, xu, n_lat_tok, n_lat_tok)
    return xu.reshape(batch, seq, d)
```

```python
import functools

import numpy as np
import jax
import jax.numpy as jnp
from jax import lax
from jax.experimental import pallas as pl
from jax.experimental.pallas import tpu as pltpu

F32 = jnp.float32
BF16 = jnp.bfloat16

EPS = 1e-6
GRID_W = 64
HEAD_DIM = 128
GLA_HEADS = 4
GLA_DK = 64
GLA_RANK = 16
GLA_TAU = 16.0
HG_HEADS = 4
NA_HEADS = 8
WIN_ROWS = 8
WIN_COLS = 16
ROPE_BASE = 10000.0
NEG_BIG = -1e30
LOG2E = 1.4426950408889634

LANE = 128
ROW_BLK = 256
CHUNK = 128
N_LEVELS = 7
TOK_BLK = 512
VMEM_LIMIT = 48 * 1024 * 1024
VMEM_LIMIT_BIG = 56 * 1024 * 1024


def _projection_layout():
    gw, hw, nw = GLA_HEADS * HEAD_DIM, HG_HEADS * HEAD_DIM, NA_HEADS * HEAD_DIM
    sections = (("gla_q", GLA_HEADS * GLA_DK), ("gla_k", GLA_HEADS * GLA_DK), ("gla_v", gw), ("gla_r", gw),
                ("a_low", 2 * GLA_RANK), ("hg_q", hw), ("hg_f", 2 * hw), ("hg_i", hw), ("hg_g", hw),
                ("na_q", nw), ("na_k", nw), ("na_v", nw))
    src, off = {}, 0
    for name, width in sections:
        src[name] = (off, width)
        off += width
    dst, off = {}, 0
    for name, width in sections:
        if name != "a_low":
            dst[name] = off
            off += width
    dst["a_low"] = off
    return src, dst, off + LANE


P_SRC, P_COLS, P_WIDTH = _projection_layout()


def _dot(a, b):
    return jnp.dot(a, b, preferred_element_type=F32)


def _dot_nt(a, b):
    return lax.dot_general(a, b, (((1,), (1,)), ((), ())), preferred_element_type=F32)


def _dot_tn(a, b):
    return lax.dot_general(a, b, (((0,), (0,)), ((), ())), preferred_element_type=F32)


def _sigmoid(x):
    return 1.0 / (1.0 + jnp.exp(-x))


def _silu(x):
    return x * _sigmoid(x)


def _split3(x):
    hi = x.astype(BF16)
    r1 = x - hi.astype(F32)
    mid = r1.astype(BF16)
    lo = (r1 - mid.astype(F32)).astype(BF16)
    return jnp.concatenate([hi, mid, lo], axis=-1)


def _sum3(y, w):
    return y[:, 0:w] + y[:, w:2 * w] + y[:, 2 * w:3 * w]


def _mod_kernel(c_ref, w_ref, b_ref, o_ref):
    s = _silu(c_ref[...]).astype(BF16)
    o_ref[...] = _dot(s, w_ref[...].astype(BF16)) + b_ref[...]


def _modulation(cc, w_mod, b_mod):
    depth, d, n = w_mod.shape
    tn = 1024
    return pl.pallas_call(
        _mod_kernel,
        grid=(depth, n // tn),
        in_specs=[
            pl.BlockSpec((8, d), lambda l, j: (0, 0)),
            pl.BlockSpec((None, d, tn), lambda l, j: (l, 0, j)),
            pl.BlockSpec((None, 1, tn), lambda l, j: (l, 0, j)),
        ],
        out_specs=pl.BlockSpec((None, 8, tn), lambda l, j: (l, 0, j)),
        out_shape=jax.ShapeDtypeStruct((depth, 8, n), F32),
        compiler_params=pltpu.CompilerParams(vmem_limit_bytes=VMEM_LIMIT),
        name="modulation",
    )(cc, w_mod, b_mod.reshape(depth, 1, n))


def _norm_mod(x, g, shift, scale):
    y = x * lax.rsqrt(jnp.mean(x * x, axis=-1, keepdims=True) + EPS) * g
    return y * (1.0 + scale) + shift


class _Tokens:
    def __init__(self, lat, ctx, n_lat, ctx_blk):
        self.lat, self.ctx, self.n_lat, self.ctx_blk = lat, ctx, n_lat, ctx_blk

    def specs(self, d):
        n_lat, ctx_blk = self.n_lat, self.ctx_blk
        return [pl.BlockSpec((TOK_BLK, d), lambda i, *_: (jnp.minimum(i, n_lat - 1), 0)),
                pl.BlockSpec((TOK_BLK, d), lambda i, *_: (ctx_blk, 0))]


def _first_norm_kernel(xl_ref, xc_ref, mod_ref, g_ref, h_ref, *, n_lat):
    i = pl.program_id(0)

    def emit(x_ref):
        h_ref[...] = _norm_mod(x_ref[...], g_ref[...], mod_ref[0:1, :], mod_ref[1:2, :]).astype(h_ref.dtype)

    pl.when(i < n_lat)(lambda: emit(xl_ref))
    pl.when(i >= n_lat)(lambda: emit(xc_ref))


def _first_norm(tok, modt, g, rows_per_mod):
    d = tok.lat.shape[1]
    per = rows_per_mod // TOK_BLK
    n_blk = tok.n_lat + 1
    return pl.pallas_call(
        functools.partial(_first_norm_kernel, n_lat=tok.n_lat),
        grid=(n_blk,),
        in_specs=tok.specs(d) + [
            pl.BlockSpec((None, 6, d), lambda i: (i // per, 0, 0)),
            pl.BlockSpec((1, d), lambda i: (0, 0)),
        ],
        out_specs=pl.BlockSpec((TOK_BLK, d), lambda i: (i, 0)),
        out_shape=jax.ShapeDtypeStruct((n_blk * TOK_BLK, d), BF16),
        compiler_params=pltpu.CompilerParams(vmem_limit_bytes=VMEM_LIMIT),
        name="first_norm",
    )(tok.lat, tok.ctx, modt, g.reshape(1, d))


def _inproj_kernel(h_ref, w_ref, o_ref):
    o_ref[...] = _dot(h_ref[...], w_ref[...])


def _in_projection(h, w, layer):
    m, d = h.shape
    n = w.shape[2]
    tn = n // 3
    return pl.pallas_call(
        _inproj_kernel,
        grid=(n // tn, m // TOK_BLK),
        in_specs=[
            pl.BlockSpec((TOK_BLK, d), lambda j, i: (i, 0)),
            pl.BlockSpec((None, d, tn), lambda j, i: (layer, 0, j)),
        ],
        out_specs=pl.BlockSpec((TOK_BLK, tn), lambda j, i: (i, j)),
        out_shape=jax.ShapeDtypeStruct((m, n), F32),
        compiler_params=pltpu.CompilerParams(vmem_limit_bytes=VMEM_LIMIT),
        name="in_projection",
    )(h, w)


def _scan_constants(reverse):
    c = CHUNK
    i = np.arange(c)[:, None]
    j = np.arange(c)[None, :]
    tri = (j >= i) if reverse else (j <= i)
    mask = np.zeros((N_LEVELS + 1, c, c), np.float32)
    for l in range(N_LEVELS):
        level = ((i ^ j) >> l) == 1
        mask[l] = level & ((i < j) if reverse else (i > j))
    mask[N_LEVELS] = np.eye(c, dtype=np.float32)
    return jnp.asarray(tri, BF16), jnp.asarray(np.concatenate([mask, mask], axis=1), BF16)


def _level_exponents(g, cum, scr_ref, reverse):
    c = CHUNK
    scr_ref[...] = cum
    row = lax.broadcasted_iota(jnp.int32, (c, LANE), 0)
    up = pltpu.roll(g, c - 1, 0)
    dn = pltpu.roll(g, 1, 0)
    m2 = row & 1
    m4 = row & 3
    if reverse:
        e0 = jnp.where(m2 == 0, g, 0.0)
        e1 = jnp.where(m4 == 0, g + up, jnp.where(m4 == 1, g, jnp.where(m4 == 2, 0.0, dn)))
    else:
        e0 = jnp.where(m2 == 1, g, 0.0)
        e1 = jnp.where(m4 == 0, up, jnp.where(m4 == 1, 0.0, jnp.where(m4 == 2, g, g + dn)))
    out = [e0, e1]
    for l in range(2, N_LEVELS):
        s = 1 << l
        pieces = []
        for blk in range(c // (2 * s)):
            lo = blk * 2 * s
            r = lo + (s if reverse else s - 1)
            pieces.append(-jnp.abs(cum[lo:lo + 2 * s] - scr_ref[r:r + 1, :]))
        out.append(pieces[0] if len(pieces) == 1 else jnp.concatenate(pieces, axis=0))
    return out


def _pair_weights(qs, k, exps, mask_ref):
    n = len(qs) * CHUNK

    def lhs(w):
        parts = [q if w is None else q * w for q in qs]
        return parts[0] if len(parts) == 1 else jnp.concatenate(parts, axis=0)

    att = mask_ref[N_LEVELS, 0:n, :] * _dot_nt(lhs(None), k).astype(BF16)
    for l, e in enumerate(exps):
        w = jnp.exp2(e).astype(BF16)
        att = att + mask_ref[l, 0:n, :] * _dot_nt(lhs(w), k * w).astype(BF16)
    return att


def _readout(o, gate, g):
    y = o * lax.rsqrt(jnp.mean(o * o, axis=-1, keepdims=True) + EPS) * g
    return y * _silu(gate)


def _scan_kernel(*refs, reverse):
    (gq_ref, gk_ref, gv_ref, hq_ref, hf_ref, hi_ref, al_ref, cos_ref, sin_ref, wg_ref, bg_ref, lb_ref,
     tri_ref, mask_ref) = refs[:14]
    if reverse:
        o_ref, scr_ref, sg_ref, sh_ref = refs[14:]
    else:
        gr_ref, hg_ref, orev_ref, gn_ref, hn_ref, ya_ref, yb_ref, scr_ref, sg_ref, sh_ref = refs[14:]
    c = CHUNK

    @pl.when(pl.program_id(2) == 0)
    def _():
        sg_ref[...] = jnp.zeros_like(sg_ref)
        sh_ref[...] = jnp.zeros_like(sh_ref)

    lane = lax.broadcasted_iota(jnp.int32, (c, LANE), 1)
    first_half = (lane % (GLA_DK // 2)) < GLA_DK // 4
    head_a = lane < GLA_DK
    lane_masks = (head_a, jnp.logical_not(head_a))
    state_mask = jnp.concatenate([head_a, jnp.logical_not(head_a)], axis=0)
    end_row = 0 if reverse else c - 1
    n_chunks = ROW_BLK // c
    offsets = [(n_chunks - 1 - ci) * c if reverse else ci * c for ci in range(n_chunks)]

    def rope(x, rows):
        swapped = jnp.where(first_half, pltpu.roll(x, LANE - GLA_DK // 4, 1), pltpu.roll(x, GLA_DK // 4, 1))
        return x * cos_ref[rows, :] + swapped * sin_ref[rows, :]

    chains = []
    for off in offsets:
        rows = slice(off, off + c)
        q = rope(gq_ref[rows, :], rows) * (GLA_DK ** -0.5)
        k = rope(gk_ref[rows, :], rows)
        logit = _dot(al_ref[rows, :].astype(BF16), wg_ref[...]) + bg_ref[...]
        soft = jnp.log2(1.0 + jnp.exp2(jnp.abs(logit) * -LOG2E))
        g = (jnp.minimum(logit, 0.0) * LOG2E - soft) * (1.0 / GLA_TAU)
        chains.append((q, k, g))
        for hh in range(2):
            head = slice(hh * LANE, (hh + 1) * LANE)
            lb = lb_ref[:, head]
            logit = hf_ref[rows, head]
            e = jnp.exp2(jnp.abs(logit) * -LOG2E)
            r = 1.0 / (1.0 + e)
            sig = jnp.where(logit >= 0, r, e * r)
            sig_neg = jnp.where(logit >= 0, e * r, r)
            g = jnp.log2(lb + (1.0 - lb) * sig)
            k = (1.0 - lb) * sig_neg
            q = _silu(hq_ref[rows, head])
            chains.append((q, k, g))

    g3 = jnp.concatenate([_split3(g) for (_, _, g) in chains], axis=-1)
    cum_all = _dot(tri_ref[...], g3)
    cums = [_sum3(cum_all[:, 3 * LANE * n:3 * LANE * (n + 1)], LANE) for n in range(len(chains))]

    atts = []
    for n, ((q, k, g), cum) in enumerate(zip(chains, cums)):
        exps = _level_exponents(g, cum, scr_ref.at[n], reverse)
        qs = [jnp.where(m, q, 0.0).astype(BF16) for m in lane_masks] if n % 3 == 0 else [q.astype(BF16)]
        atts.append(_pair_weights(qs, k.astype(BF16), exps, mask_ref))

    for ci, off in enumerate(offsets):
        rows = slice(off, off + c)
        (q, k, _), cum, att = chains[3 * ci], cums[3 * ci], atts[3 * ci]
        v = gv_ref[rows, :].astype(BF16)
        cum_end = cum[end_row:end_row + 1, :]
        st = sg_ref[...]
        inter = _dot_nt((q * jnp.exp2(cum)).astype(BF16), st.astype(BF16))
        o_gla = jnp.concatenate([_dot(att[0:c], v[:, 0:LANE]), _dot(att[c:2 * c], v[:, LANE:2 * LANE])],
                                axis=-1) + inter
        kd = (k * jnp.exp2(cum_end - cum)).astype(BF16)
        sg_ref[...] = st * jnp.exp2(cum_end) + jnp.where(state_mask, _dot_tn(v, kd), 0.0)
        o_hg = []
        for hh in range(2):
            (q, k, _), cum, att = chains[3 * ci + 1 + hh], cums[3 * ci + 1 + hh], atts[3 * ci + 1 + hh]
            v = hi_ref[rows, hh * LANE:(hh + 1) * LANE].astype(BF16)
            cum_end = cum[end_row:end_row + 1, :]
            st = sh_ref[hh]
            o_hg.append(_dot(att, v) + _dot_nt((q * jnp.exp2(cum)).astype(BF16), st.astype(BF16)))
            kd = (k * jnp.exp2(cum_end - cum)).astype(BF16)
            sh_ref[hh] = st * jnp.exp2(cum_end) + _dot_tn(v, kd)
        o_all = jnp.concatenate([o_gla] + o_hg, axis=-1)
        if reverse:
            o_ref[rows, :] = o_all
        else:
            o_all = o_all + orev_ref[rows, :]
            for hh in range(2):
                sl = slice(hh * LANE, (hh + 1) * LANE)
                y = _readout(o_all[:, sl], gr_ref[rows, sl], gn_ref[...])
                ya_ref[rows, sl] = y.astype(ya_ref.dtype)
                y = _readout(o_all[:, 2 * LANE + hh * LANE:2 * LANE + (hh + 1) * LANE], hg_ref[rows, sl],
                             hn_ref[...])
                yb_ref[rows, sl] = y.astype(yb_ref.dtype)


def _gated_scans(p, cos_t, sin_t, wg, bg, lb, gla_norm, hg_norm, batch, n_lat_blk):
    assert CHUNK == HEAD_DIM == LANE and GLA_HEADS == HG_HEADS and GLA_HEADS % 2 == 0
    m = p.shape[0]
    n_pairs = GLA_HEADS // 2
    n_blk = n_lat_blk + 1
    col = P_COLS

    def run(reverse, extra_in, extra_specs, out_shape, out_specs):
        def pos(t):
            lat = (n_lat_blk - t) if reverse else (t - 1)
            return jnp.where(t == 0, n_lat_blk, lat)

        def row_blk(b, t):
            return jnp.where(t == 0, batch * n_lat_blk + b, b * n_lat_blk + pos(t))

        def cols(start, width):
            return pl.BlockSpec((ROW_BLK, width), lambda b, h, t: (row_blk(b, t), start // width + h))

        d = 1 if reverse else 0
        tri, mask = _scan_constants(reverse)
        hw = HG_HEADS * HEAD_DIM
        in_specs = [
            cols(col["gla_q"], 2 * GLA_DK), cols(col["gla_k"], 2 * GLA_DK), cols(col["gla_v"], 2 * HEAD_DIM),
            cols(col["hg_q"], 2 * HEAD_DIM), cols(col["hg_f"] + d * hw, 2 * HEAD_DIM),
            cols(col["hg_i"], 2 * HEAD_DIM),
            pl.BlockSpec((ROW_BLK, LANE), lambda b, h, t: (row_blk(b, t), col["a_low"] // LANE)),
            pl.BlockSpec((ROW_BLK, LANE), lambda b, h, t: (pos(t), 0)),
            pl.BlockSpec((ROW_BLK, LANE), lambda b, h, t: (pos(t), 0)),
            pl.BlockSpec((None, None, LANE, LANE), lambda b, h, t: (d, h, 0, 0)),
            pl.BlockSpec((None, None, 1, LANE), lambda b, h, t: (d, h, 0, 0)),
            pl.BlockSpec((None, None, 1, 2 * LANE), lambda b, h, t: (d, h, 0, 0)),
            pl.BlockSpec((CHUNK, CHUNK), lambda b, h, t: (0, 0)),
            pl.BlockSpec((N_LEVELS + 1, 2 * CHUNK, CHUNK), lambda b, h, t: (0, 0, 0)),
        ] + [s(row_blk, cols) for s in extra_specs]
        return pl.pallas_call(
            functools.partial(_scan_kernel, reverse=reverse),
            grid=(batch, n_pairs, n_blk),
            in_specs=in_specs,
            out_specs=[s(row_blk, cols) for s in out_specs],
            out_shape=out_shape,
            scratch_shapes=[
                pltpu.VMEM((3 * ROW_BLK // CHUNK, CHUNK, LANE), F32),
                pltpu.VMEM((2 * HEAD_DIM, LANE), F32),
                pltpu.VMEM((2, HEAD_DIM, HEAD_DIM), F32),
            ],
            compiler_params=pltpu.CompilerParams(vmem_limit_bytes=VMEM_LIMIT),
            name="gated_scan_rev" if reverse else "gated_scan_fwd",
        )(p, p, p, p, p, p, p, cos_t, sin_t, wg, bg, lb, tri, mask, *extra_in)

    def rows_spec(width):
        return lambda row_blk, cols: pl.BlockSpec((ROW_BLK, width), lambda b, h, t: (row_blk(b, t), h))

    def section_spec(name):
        return lambda row_blk, cols: cols(col[name], 2 * HEAD_DIM)

    def const_spec(row_blk, cols):
        return pl.BlockSpec((1, LANE), lambda b, h, t: (0, 0))

    (o_rev,) = run(True, (), (), [jax.ShapeDtypeStruct((m, 2 * n_pairs * 2 * HEAD_DIM), F32)],
                   [rows_spec(4 * LANE)])
    return run(False, (p, p, o_rev, gla_norm.reshape(1, -1), hg_norm.reshape(1, -1)),
               (section_spec("gla_r"), section_spec("hg_g"), rows_spec(4 * LANE), const_spec, const_spec),
               [jax.ShapeDtypeStruct((m, GLA_HEADS * HEAD_DIM), BF16),
                jax.ShapeDtypeStruct((m, HG_HEADS * HEAD_DIM), BF16)],
               [rows_spec(2 * LANE), rows_spec(2 * LANE)])


def _head_norm(x, g):
    return x * lax.rsqrt(jnp.mean(x * x, axis=-1, keepdims=True) + EPS) * g


NA_Q_ROWS = ROW_BLK // GRID_W
NA_SPAN_ROWS = 12


def _na_span_base(n, n_rows, xp=jnp):
    return xp.clip(n * NA_Q_ROWS - WIN_ROWS // 2, 0, n_rows - NA_SPAN_ROWS)


def _na_variants(n_rows):
    n_blocks = n_rows // NA_Q_ROWS
    assert n_rows % NA_Q_ROWS == 0 and n_blocks >= 4

    def rows(n):
        base = int(_na_span_base(n, n_rows, np))
        out = []
        for i in range(NA_Q_ROWS):
            r = n * NA_Q_ROWS + i
            r0 = min(max(r - WIN_ROWS // 2, 0), n_rows - WIN_ROWS)
            out.append([(base + kk - r + WIN_ROWS - 1, r0 <= base + kk < r0 + WIN_ROWS)
                        for kk in range(NA_SPAN_ROWS)])
        return out

    variants = [rows(0), rows(1), rows(n_blocks - 1)]
    assert all(rows(n) == variants[1] for n in range(1, n_blocks - 1))
    inside = sorted({dr for row in variants[1] for dr, ok in row if ok})
    lo, hi = inside[0], inside[-1] + 1
    assert all(ok == (lo <= dr < hi) for row in variants[1] for dr, ok in row)
    return variants, (lo, hi)


def _na_kernel(q_ref, kl_ref, vl_ref, kc_ref, vc_ref, qn_ref, kn_ref, tbl_ref, y_ref,
               kls_ref, vls_ref, kcs_ref, vcs_ref, bias_ref, *, n_rows, steps_per_batch, batch):
    t = pl.program_id(1)

    @pl.when((t > 0) & ((t - 1) % steps_per_batch == 0))
    def _():
        kls_ref[...] = _head_norm(kl_ref[...], kn_ref[...]).astype(BF16)
        vls_ref[...] = vl_ref[...].astype(BF16)

    @pl.when(t == 0)
    def _():
        kcs_ref[...] = _head_norm(kc_ref[...], kn_ref[...]).astype(BF16)
        vcs_ref[...] = vc_ref[...].astype(BF16)
        variants, _ = _na_variants(n_rows)
        for v, var in enumerate(variants):
            for i, row in enumerate(var):
                for g in range(NA_SPAN_ROWS // 2):
                    (dr0, ok0), (dr1, ok1) = row[2 * g], row[2 * g + 1]
                    dst = (v, slice(i * GRID_W, (i + 1) * GRID_W), slice(g * LANE, (g + 1) * LANE))
                    if not (ok0 or ok1):
                        bias_ref[dst] = jnp.full((GRID_W, LANE), NEG_BIG, F32)
                    else:
                        assert dr1 == dr0 + 1 and 0 <= dr0 < tbl_ref.shape[1] and (v == 1 or (ok0 and ok1))
                        bias_ref[dst] = tbl_ref[0 if (ok0 and ok1) else 1, dr0]

    q = _head_norm(q_ref[...], qn_ref[...] * (HEAD_DIM ** -0.5 * LOG2E)).astype(BF16)

    @pl.when(t == 0)
    def _():
        for b in range(batch):
            rows = slice(b * ROW_BLK, (b + 1) * ROW_BLK)
            s = _dot_nt(q[rows], kcs_ref[rows, :])
            e = jnp.exp2(s - jnp.max(s, axis=-1, keepdims=True))
            o = _dot(e.astype(BF16), vcs_ref[rows, :]) / jnp.sum(e, axis=-1, keepdims=True)
            y_ref[rows, :] = o.astype(y_ref.dtype)

    @pl.when(t > 0)
    def _():
        ctx_rows = pl.ds(pl.multiple_of(((t - 1) // steps_per_batch) * ROW_BLK, ROW_BLK), ROW_BLK)
        kc = kcs_ref[ctx_rows, :]
        vc = vcs_ref[ctx_rows, :]
        for sub in range(TOK_BLK // ROW_BLK):
            rows = slice(sub * ROW_BLK, (sub + 1) * ROW_BLK)
            n = ((t - 1) % steps_per_batch) * (TOK_BLK // ROW_BLK) + sub
            base = _na_span_base(n, n_rows)
            variant = jnp.where(n == 0, 0, jnp.where(n == n_rows // NA_Q_ROWS - 1, 2, 1))
            span = pl.ds(pl.multiple_of(base * GRID_W, GRID_W), NA_SPAN_ROWS * GRID_W)
            s_win = _dot_nt(q[rows], kls_ref[span, :]) + bias_ref[variant]
            s_ctx = _dot_nt(q[rows], kc)
            mx = jnp.maximum(jnp.max(s_win, axis=-1, keepdims=True), jnp.max(s_ctx, axis=-1, keepdims=True))
            e_win = jnp.exp2(s_win - mx)
            e_ctx = jnp.exp2(s_ctx - mx)
            den = jnp.sum(e_win, axis=-1, keepdims=True) + jnp.sum(e_ctx, axis=-1, keepdims=True)
            o = _dot(e_win.astype(BF16), vls_ref[span, :]) + _dot(e_ctx.astype(BF16), vc)
            y_ref[rows, :] = (o / den).astype(y_ref.dtype)


def _na_bias_table(rpb, n_rows):
    n_heads, n_dr, n_dc = rpb.shape
    c = np.arange(GRID_W)[:, None]
    kc = np.arange(GRID_W)[None, :]
    c0 = np.clip(c - WIN_COLS // 2, 0, GRID_W - WIN_COLS)
    col_valid = (kc >= c0) & (kc < c0 + WIN_COLS)
    pick_dc = (np.arange(n_dc)[:, None, None] == (kc - c + WIN_COLS - 1)[None]) & col_valid[None]
    col = jnp.einsum("hrb,bcq->hrcq", rpb.astype(F32), jnp.asarray(pick_dc, F32), precision=lax.Precision.HIGHEST)
    col = col + jnp.asarray(np.where(col_valid, 0.0, NEG_BIG), F32)
    col = jnp.concatenate([col, jnp.full((n_heads, 1, GRID_W, GRID_W), NEG_BIG, F32)], axis=1) * LOG2E
    _, (lo, hi) = _na_variants(n_rows)
    dr = np.arange(n_dr + 1)[None, :, None, None]
    col_in = jnp.where((dr >= lo) & (dr < hi), col, NEG_BIG)

    def tiles(x):
        return jnp.concatenate([x[:, :-1], x[:, 1:]], axis=-1)

    return jnp.stack([tiles(col), tiles(col_in)], axis=1)


def _neighbourhood_attention(p, q_norm, k_norm, bias_tbl, layer, batch, n_lat_blk):
    m = p.shape[0]
    lat_rows = n_lat_blk * ROW_BLK
    assert lat_rows % TOK_BLK == 0 and batch * ROW_BLK == TOK_BLK
    steps_per_batch = lat_rows // TOK_BLK
    qb, kb, vb = (P_COLS[name] // LANE for name in ("na_q", "na_k", "na_v"))
    ctx_blk = batch * steps_per_batch

    def q_blk(t):
        return jnp.where(t == 0, ctx_blk, t - 1)

    def kv_blk(t):
        return jnp.maximum(t - 1, 0) // steps_per_batch

    kern = functools.partial(_na_kernel, n_rows=lat_rows // GRID_W, steps_per_batch=steps_per_batch, batch=batch)
    return pl.pallas_call(
        kern,
        grid=(NA_HEADS, 1 + batch * steps_per_batch),
        in_specs=[
            pl.BlockSpec((TOK_BLK, LANE), lambda h, t: (q_blk(t), qb + h)),
            pl.BlockSpec((lat_rows, LANE), lambda h, t: (kv_blk(t), kb + h)),
            pl.BlockSpec((lat_rows, LANE), lambda h, t: (kv_blk(t), vb + h)),
            pl.BlockSpec((TOK_BLK, LANE), lambda h, t: (ctx_blk, kb + h)),
            pl.BlockSpec((TOK_BLK, LANE), lambda h, t: (ctx_blk, vb + h)),
            pl.BlockSpec((1, LANE), lambda h, t: (0, 0)),
            pl.BlockSpec((1, LANE), lambda h, t: (0, 0)),
            pl.BlockSpec((None,) + bias_tbl.shape[1:], lambda h, t: (layer * NA_HEADS + h, 0, 0, 0, 0)),
        ],
        out_specs=pl.BlockSpec((TOK_BLK, LANE), lambda h, t: (q_blk(t), h)),
        out_shape=jax.ShapeDtypeStruct((m, NA_HEADS * HEAD_DIM), BF16),
        scratch_shapes=[
            pltpu.VMEM((lat_rows, HEAD_DIM), BF16),
            pltpu.VMEM((lat_rows, HEAD_DIM), BF16),
            pltpu.VMEM((TOK_BLK, HEAD_DIM), BF16),
            pltpu.VMEM((TOK_BLK, HEAD_DIM), BF16),
            pltpu.VMEM((3, ROW_BLK, NA_SPAN_ROWS * GRID_W), F32),
        ],
        compiler_params=pltpu.CompilerParams(vmem_limit_bytes=VMEM_LIMIT),
        name="neighbourhood_attention",
    )(p, p, p, p, p, q_norm.reshape(1, -1), k_norm.reshape(1, -1), bias_tbl)


def _outproj_kernel(xl_ref, xc_ref, ya_ref, yb_ref, yc_ref, mod_ref, g_ref, wa_ref, wb_ref, wc_ref,
                    o_ref, h_ref, *, n_lat):
    i = pl.program_id(0)
    acc = _dot(ya_ref[...], wa_ref[...]) + _dot(yb_ref[...], wb_ref[...]) + _dot(yc_ref[...], wc_ref[...])
    upd = mod_ref[2:3, :] * acc

    def emit(x_ref):
        x1 = x_ref[...] + upd
        o_ref[...] = x1
        h_ref[...] = _norm_mod(x1, g_ref[...], mod_ref[3:4, :], mod_ref[4:5, :]).astype(h_ref.dtype)

    pl.when(i < n_lat)(lambda: emit(xl_ref))
    pl.when(i >= n_lat)(lambda: emit(xc_ref))


def _out_projection(tok, ya, yb, yc, modt, g_mlp, w_out, layer, rows_per_mod, with_ctx):
    d = w_out.shape[2]
    tm = TOK_BLK
    per = rows_per_mod // tm
    n_blk = tok.n_lat + (1 if with_ctx else 0)
    wa, wb, wc = ya.shape[1], yb.shape[1], yc.shape[1]
    return pl.pallas_call(
        functools.partial(_outproj_kernel, n_lat=tok.n_lat),
        grid=(n_blk,),
        in_specs=tok.specs(d) + [
            pl.BlockSpec((tm, wa), lambda i: (i, 0)),
            pl.BlockSpec((tm, wb), lambda i: (i, 0)),
            pl.BlockSpec((tm, wc), lambda i: (i, 0)),
            pl.BlockSpec((None, 6, d), lambda i: (i // per, 0, 0)),
            pl.BlockSpec((1, d), lambda i: (0, 0)),
            pl.BlockSpec((None, wa, d), lambda i: (layer, 0, 0)),
            pl.BlockSpec((None, wb, d), lambda i: (layer, wa // wb, 0)),
            pl.BlockSpec((None, wc, d), lambda i: (layer, (wa + wb) // wc, 0)),
        ],
        out_specs=[pl.BlockSpec((tm, d), lambda i: (i, 0)), pl.BlockSpec((tm, d), lambda i: (i, 0))],
        out_shape=[jax.ShapeDtypeStruct((n_blk * tm, d), F32), jax.ShapeDtypeStruct((n_blk * tm, d), BF16)],
        compiler_params=pltpu.CompilerParams(vmem_limit_bytes=VMEM_LIMIT),
        name="out_projection",
    )(tok.lat, tok.ctx, ya, yb, yc, modt, g_mlp.reshape(1, d), w_out, w_out, w_out)


def _mlp_kernel(*refs, emit_next):
    if emit_next:
        h_ref, x_ref, mod_ref, w1_ref, w2_ref, gn_ref, modn_ref, o_ref, hn_ref, acc_ref = refs
    else:
        h_ref, x_ref, mod_ref, w1_ref, w2_ref, o_ref, acc_ref = refs
    j = pl.program_id(1)

    @pl.when(j == 0)
    def _():
        acc_ref[...] = jnp.zeros_like(acc_ref)

    a = jnp.maximum(_dot(h_ref[...], w1_ref[...]), 0.0)
    acc_ref[...] += _dot((a * a).astype(BF16), w2_ref[...])

    @pl.when(j == pl.num_programs(1) - 1)
    def _():
        x2 = x_ref[...] + mod_ref[5:6, :] * acc_ref[...]
        o_ref[...] = x2
        if emit_next:
            hn_ref[...] = _norm_mod(x2, gn_ref[...], modn_ref[0:1, :], modn_ref[1:2, :]).astype(hn_ref.dtype)


def _mlp(h, x1, modt, w1, w2, layer, rows_per_mod, next_norm):
    m, d = x1.shape
    ff = w1.shape[2]
    tm, tf = TOK_BLK, 1024
    per = rows_per_mod // tm
    emit_next = next_norm is not None

    def rows(i, j):
        return (i, 0)

    def mod_rows(i, j):
        return (i // per, 0, 0)

    in_specs = [
        pl.BlockSpec((tm, d), rows),
        pl.BlockSpec((tm, d), rows),
        pl.BlockSpec((None, 6, d), mod_rows),
        pl.BlockSpec((None, d, tf), lambda i, j: (layer, 0, j)),
        pl.BlockSpec((None, tf, d), lambda i, j: (layer, j, 0)),
    ]
    args = [h, x1, modt, w1, w2]
    out_specs = [pl.BlockSpec((tm, d), rows)]
    out_shape = [jax.ShapeDtypeStruct((m, d), F32)]
    if emit_next:
        g_next, modt_next = next_norm
        in_specs += [pl.BlockSpec((1, d), lambda i, j: (0, 0)), pl.BlockSpec((None, 6, d), mod_rows)]
        args += [g_next.reshape(1, d), modt_next]
        out_specs.append(pl.BlockSpec((tm, d), rows))
        out_shape.append(jax.ShapeDtypeStruct((m, d), BF16))
    return pl.pallas_call(
        functools.partial(_mlp_kernel, emit_next=emit_next),
        grid=(m // tm, ff // tf),
        in_specs=in_specs,
        out_specs=out_specs,
        out_shape=out_shape,
        scratch_shapes=[pltpu.VMEM((tm, d), F32)],
        compiler_params=pltpu.CompilerParams(vmem_limit_bytes=VMEM_LIMIT_BIG),
        name="mlp",
    )(*args)


def _relayout_kernel(w_ref, o_ref):
    al, aw = P_SRC["a_low"]
    x = w_ref[...]
    pad = jnp.zeros((x.shape[0], LANE - aw), x.dtype)
    o_ref[...] = jnp.concatenate([x[:, :al], x[:, al + aw:], x[:, al:al + aw], pad], axis=1).astype(o_ref.dtype)


def _permute_w_in(w_in):
    depth, d, n = w_in.shape
    tr = 256
    return pl.pallas_call(
        _relayout_kernel,
        grid=(depth, d // tr),
        in_specs=[pl.BlockSpec((None, tr, n), lambda l, i: (l, i, 0))],
        out_specs=pl.BlockSpec((None, tr, P_WIDTH), lambda l, i: (l, i, 0)),
        out_shape=jax.ShapeDtypeStruct((depth, d, P_WIDTH), BF16),
        compiler_params=pltpu.CompilerParams(vmem_limit_bytes=VMEM_LIMIT),
        name="w_in_relayout",
    )(w_in)


def _gla_gate_weights(w_a2, b_a):
    n_pairs = GLA_HEADS // 2
    wg = jnp.zeros((2, n_pairs, LANE, 2 * GLA_DK), F32)
    for dd in range(2):
        blk = w_a2[dd].reshape(GLA_RANK, n_pairs, 2 * GLA_DK).transpose(1, 0, 2)
        wg = wg.at[dd, :, dd * GLA_RANK:(dd + 1) * GLA_RANK, :].set(blk)
    bg = b_a.reshape(2, n_pairs, 1, 2 * GLA_DK)
    return wg.astype(BF16), bg.astype(F32)


def _rope_tables(seq, n_ctx):
    quarter = GLA_DK // 4
    inv_freq = ROPE_BASE ** (-np.arange(quarter, dtype=np.float64) / quarter)
    pos = np.arange(seq)
    lane = np.arange(LANE)
    is_col = (lane % GLA_DK) // (GLA_DK // 2) == 1
    first = (lane % (GLA_DK // 2)) < quarter
    p = np.where(is_col[None, :], (pos % GRID_W)[:, None], (pos // GRID_W)[:, None]).astype(np.float64)
    ang = p * inv_freq[lane % quarter][None, :]
    cos = np.cos(ang)
    sin = np.where(first[None, :], -np.sin(ang), np.sin(ang))
    cos = np.concatenate([cos, np.ones((n_ctx, LANE))], axis=0)
    sin = np.concatenate([sin, np.zeros((n_ctx, LANE))], axis=0)
    return jnp.asarray(cos, F32), jnp.asarray(sin, F32)


def kernel(x, c, ctx, c_ctx, w_mod, b_mod, attn_norm, w_in, gla_w_a2, gla_b_a, gla_norm, hg_lower_bounds, hg_norm, na_q_norm, na_k_norm, na_rpb, w_out, mlp_norm, w_mlp1, w_mlp2):
    batch, seq, d = x.shape
    n_ctx = ctx.shape[1]
    depth = w_mod.shape[0]
    assert seq % ROW_BLK == 0 and n_ctx == ROW_BLK and batch * n_ctx == TOK_BLK and d % LANE == 0
    assert seq % TOK_BLK == 0 and seq % GRID_W == 0
    n_lat_blk = seq // ROW_BLK
    lat_rows = batch * seq
    n_lat_tok = lat_rows // TOK_BLK

    tok = _Tokens(x.reshape(lat_rows, d), ctx.reshape(batch * n_ctx, d), n_lat_tok, 0)

    cc = jnp.zeros((8, d), F32).at[0:batch].set(c).at[batch].set(c_ctx)
    mod = _modulation(cc, w_mod, b_mod)
    modt = mod[:, 0:batch + 1].reshape(depth, batch + 1, 6, d)

    lb_p = jax.nn.softmax(hg_lower_bounds.astype(F32), axis=0)
    lower = jnp.cumsum(lb_p, axis=0) - lb_p[0]
    lower = lower.reshape(depth, 2, HG_HEADS // 2, 1, 2 * HEAD_DIM)
    cos_t, sin_t = _rope_tables(seq, n_ctx)

    bias_tbl = _na_bias_table(na_rpb.reshape((depth * NA_HEADS,) + na_rpb.shape[2:]), seq // GRID_W)
    w_in_b = _permute_w_in(w_in)
    w_out_b, w1_b, w2_b = (w.astype(BF16) for w in (w_out, w_mlp1, w_mlp2))

    h = _first_norm(tok, modt[0], attn_norm[0], seq)
    for l in range(depth):
        last = l == depth - 1
        p = _in_projection(h, w_in_b, l)
        wg, bg = _gla_gate_weights(gla_w_a2[l], gla_b_a[l])
        ya, yb = _gated_scans(p, cos_t, sin_t, wg, bg, lower[l], gla_norm[l], hg_norm[l], batch, n_lat_blk)
        yc = _neighbourhood_attention(p, na_q_norm[l], na_k_norm[l], bias_tbl, l, batch, n_lat_blk)
        x1, h2 = _out_projection(tok, ya, yb, yc, modt[l], mlp_norm[l], w_out_b, l, seq, with_ctx=not last)
        if last:
            (xu,) = _mlp(h2, x1, modt[l], w1_b, w2_b, l, seq, None)
        else:
            xu, h = _mlp(h2, x1, modt[l], w1_b, w2_b, l, seq, (attn_norm[l + 1], modt[l + 1]))
        tok = _Tokens(xu, xu, n_lat_tok, n_lat_tok)
    return xu.reshape(batch, seq, d)
```

```python
import functools

import numpy as np
import jax
import jax.numpy as jnp
from jax import lax
from jax.experimental import pallas as pl
from jax.experimental.pallas import tpu as pltpu

F32 = jnp.float32
BF16 = jnp.bfloat16

EPS = 1e-6
GRID_W = 64
HEAD_DIM = 128
GLA_HEADS = 4
GLA_DK = 64
GLA_RANK = 16
GLA_TAU = 16.0
HG_HEADS = 4
NA_HEADS = 8
WIN_ROWS = 8
WIN_COLS = 16
ROPE_BASE = 10000.0
NEG_BIG = -1e30
LOG2E = 1.4426950408889634

LANE = 128
ROW_BLK = 256
CHUNK = 128
N_LEVELS = 7
TOK_BLK = 512
VMEM_LIMIT = 48 * 1024 * 1024
VMEM_LIMIT_BIG = 56 * 1024 * 1024


def _projection_layout():
    gw, hw, nw = GLA_HEADS * HEAD_DIM, HG_HEADS * HEAD_DIM, NA_HEADS * HEAD_DIM
    sections = (("gla_q", GLA_HEADS * GLA_DK), ("gla_k", GLA_HEADS * GLA_DK), ("gla_v", gw), ("gla_r", gw),
                ("a_low", 2 * GLA_RANK), ("hg_q", hw), ("hg_f", 2 * hw), ("hg_i", hw), ("hg_g", hw),
                ("na_q", nw), ("na_k", nw), ("na_v", nw))
    src, off = {}, 0
    for name, width in sections:
        src[name] = (off, width)
        off += width
    dst, off = {}, 0
    for name, width in sections:
        if name != "a_low":
            dst[name] = off
            off += width
    dst["a_low"] = off
    return src, dst, off + LANE


P_SRC, P_COLS, P_WIDTH = _projection_layout()


def _dot(a, b):
    return jnp.dot(a, b, preferred_element_type=F32)


def _dot_nt(a, b):
    return lax.dot_general(a, b, (((1,), (1,)), ((), ())), preferred_element_type=F32)


def _dot_tn(a, b):
    return lax.dot_general(a, b, (((0,), (0,)), ((), ())), preferred_element_type=F32)


def _sigmoid(x):
    return 1.0 / (1.0 + jnp.exp(-x))


def _silu(x):
    return x * _sigmoid(x)


def _split3(x):
    hi = x.astype(BF16)
    r1 = x - hi.astype(F32)
    mid = r1.astype(BF16)
    lo = (r1 - mid.astype(F32)).astype(BF16)
    return jnp.concatenate([hi, mid, lo], axis=-1)


def _sum3(y, w):
    return y[:, 0:w] + y[:, w:2 * w] + y[:, 2 * w:3 * w]


def _mod_kernel(c_ref, w_ref, b_ref, o_ref):
    s = _silu(c_ref[...]).astype(BF16)
    o_ref[...] = _dot(s, w_ref[...].astype(BF16)) + b_ref[...]


def _modulation(cc, w_mod, b_mod):
    depth, d, n = w_mod.shape
    tn = 1024
    return pl.pallas_call(
        _mod_kernel,
        grid=(depth, n // tn),
        in_specs=[
            pl.BlockSpec((8, d), lambda l, j: (0, 0)),
            pl.BlockSpec((None, d, tn), lambda l, j: (l, 0, j)),
            pl.BlockSpec((None, 1, tn), lambda l, j: (l, 0, j)),
        ],
        out_specs=pl.BlockSpec((None, 8, tn), lambda l, j: (l, 0, j)),
        out_shape=jax.ShapeDtypeStruct((depth, 8, n), F32),
        compiler_params=pltpu.CompilerParams(vmem_limit_bytes=VMEM_LIMIT),
        name="modulation",
    )(cc, w_mod, b_mod.reshape(depth, 1, n))


def _norm_mod(x, g, shift, scale):
    y = x * lax.rsqrt(jnp.mean(x * x, axis=-1, keepdims=True) + EPS) * g
    return y * (1.0 + scale) + shift


class _Tokens:
    def __init__(self, lat, ctx, n_lat, ctx_blk):
        self.lat, self.ctx, self.n_lat, self.ctx_blk = lat, ctx, n_lat, ctx_blk

    def specs(self, d):
        n_lat, ctx_blk = self.n_lat, self.ctx_blk
        return [pl.BlockSpec((TOK_BLK, d), lambda i, *_: (jnp.minimum(i, n_lat - 1), 0)),
                pl.BlockSpec((TOK_BLK, d), lambda i, *_: (ctx_blk, 0))]


def _first_norm_kernel(xl_ref, xc_ref, mod_ref, g_ref, h_ref, *, n_lat):
    i = pl.program_id(0)

    def emit(x_ref):
        h_ref[...] = _norm_mod(x_ref[...], g_ref[...], mod_ref[0:1, :], mod_ref[1:2, :]).astype(h_ref.dtype)

    pl.when(i < n_lat)(lambda: emit(xl_ref))
    pl.when(i >= n_lat)(lambda: emit(xc_ref))


def _first_norm(tok, modt, g, rows_per_mod):
    d = tok.lat.shape[1]
    per = rows_per_mod // TOK_BLK
    n_blk = tok.n_lat + 1
    return pl.pallas_call(
        functools.partial(_first_norm_kernel, n_lat=tok.n_lat),
        grid=(n_blk,),
        in_specs=tok.specs(d) + [
            pl.BlockSpec((None, 6, d), lambda i: (i // per, 0, 0)),
            pl.BlockSpec((1, d), lambda i: (0, 0)),
        ],
        out_specs=pl.BlockSpec((TOK_BLK, d), lambda i: (i, 0)),
        out_shape=jax.ShapeDtypeStruct((n_blk * TOK_BLK, d), BF16),
        compiler_params=pltpu.CompilerParams(vmem_limit_bytes=VMEM_LIMIT),
        name="first_norm",
    )(tok.lat, tok.ctx, modt, g.reshape(1, d))


def _inproj_kernel(h_ref, w_ref, w1_ref, w2_ref, o_ref, w1o_ref, w2o_ref, *, n_cast):
    o_ref[...] = _dot(h_ref[...], w_ref[...])

    @pl.when(pl.program_id(0) * pl.num_programs(1) + pl.program_id(1) < n_cast)
    def _():
        w1o_ref[...] = w1_ref[...].astype(w1o_ref.dtype)
        w2o_ref[...] = w2_ref[...].astype(w2o_ref.dtype)


def _in_projection(h, w, w1, w2, layer):
    m, d = h.shape
    n = w.shape[2]
    ff = w1.shape[2]
    tn = n // 3
    n_i = m // TOK_BLK
    tc = 256
    n_cast = ff // tc
    assert n_cast <= 3 * n_i

    def cast_blk(j, i):
        return jnp.minimum(j * n_i + i, n_cast - 1)

    return pl.pallas_call(
        functools.partial(_inproj_kernel, n_cast=n_cast),
        grid=(n // tn, n_i),
        in_specs=[
            pl.BlockSpec((TOK_BLK, d), lambda j, i: (i, 0)),
            pl.BlockSpec((None, d, tn), lambda j, i: (layer, 0, j)),
            pl.BlockSpec((None, d, tc), lambda j, i: (layer, 0, cast_blk(j, i))),
            pl.BlockSpec((None, tc, d), lambda j, i: (layer, cast_blk(j, i), 0)),
        ],
        out_specs=[
            pl.BlockSpec((TOK_BLK, tn), lambda j, i: (i, j)),
            pl.BlockSpec((d, tc), lambda j, i: (0, cast_blk(j, i))),
            pl.BlockSpec((tc, d), lambda j, i: (cast_blk(j, i), 0)),
        ],
        out_shape=[
            jax.ShapeDtypeStruct((m, n), F32),
            jax.ShapeDtypeStruct((d, ff), BF16),
            jax.ShapeDtypeStruct((ff, d), BF16),
        ],
        compiler_params=pltpu.CompilerParams(vmem_limit_bytes=VMEM_LIMIT_BIG),
        name="in_projection",
    )(h, w, w1, w2)


def _scan_constants(reverse):
    c = CHUNK
    i = np.arange(c)[:, None]
    j = np.arange(c)[None, :]
    tri = (j >= i) if reverse else (j <= i)
    mask = np.zeros((N_LEVELS + 1, c, c), np.float32)
    for l in range(N_LEVELS):
        level = ((i ^ j) >> l) == 1
        mask[l] = level & ((i < j) if reverse else (i > j))
    mask[N_LEVELS] = np.eye(c, dtype=np.float32)
    return jnp.asarray(tri, BF16), jnp.asarray(np.concatenate([mask, mask], axis=1), BF16)


def _level_exponents(g, cum, scr_ref, reverse):
    c = CHUNK
    scr_ref[...] = cum
    row = lax.broadcasted_iota(jnp.int32, (c, LANE), 0)
    up = pltpu.roll(g, c - 1, 0)
    dn = pltpu.roll(g, 1, 0)
    m2 = row & 1
    m4 = row & 3
    if reverse:
        e0 = jnp.where(m2 == 0, g, 0.0)
        e1 = jnp.where(m4 == 0, g + up, jnp.where(m4 == 1, g, jnp.where(m4 == 2, 0.0, dn)))
    else:
        e0 = jnp.where(m2 == 1, g, 0.0)
        e1 = jnp.where(m4 == 0, up, jnp.where(m4 == 1, 0.0, jnp.where(m4 == 2, g, g + dn)))
    out = [e0, e1]
    for l in range(2, N_LEVELS):
        s = 1 << l
        pieces = []
        for blk in range(c // (2 * s)):
            lo = blk * 2 * s
            r = lo + (s if reverse else s - 1)
            pieces.append(-jnp.abs(cum[lo:lo + 2 * s] - scr_ref[r:r + 1, :]))
        out.append(pieces[0] if len(pieces) == 1 else jnp.concatenate(pieces, axis=0))
    return out


def _pair_weights(qs, k, exps, mask_ref):
    n = len(qs) * CHUNK

    def lhs(w):
        parts = [q if w is None else q * w for q in qs]
        return parts[0] if len(parts) == 1 else jnp.concatenate(parts, axis=0)

    att = mask_ref[N_LEVELS, 0:n, :] * _dot_nt(lhs(None), k).astype(BF16)
    for l, e in enumerate(exps):
        w = jnp.exp2(e).astype(BF16)
        att = att + mask_ref[l, 0:n, :] * _dot_nt(lhs(w), k * w).astype(BF16)
    return att


def _readout(o, gate, g):
    y = o * lax.rsqrt(jnp.mean(o * o, axis=-1, keepdims=True) + EPS) * g
    return y * _silu(gate)


def _scan_kernel(*refs, reverse):
    (gq_ref, gk_ref, gv_ref, hq_ref, hf_ref, hi_ref, al_ref, cos_ref, sin_ref, wg_ref, bg_ref, lb_ref,
     tri_ref, mask_ref) = refs[:14]
    if reverse:
        o_ref, scr_ref, sg_ref, sh_ref = refs[14:]
    else:
        gr_ref, hg_ref, orev_ref, gn_ref, hn_ref, ya_ref, yb_ref, scr_ref, sg_ref, sh_ref = refs[14:]
    c = CHUNK

    @pl.when(pl.program_id(2) == 0)
    def _():
        sg_ref[...] = jnp.zeros_like(sg_ref)
        sh_ref[...] = jnp.zeros_like(sh_ref)

    lane = lax.broadcasted_iota(jnp.int32, (c, LANE), 1)
    first_half = (lane % (GLA_DK // 2)) < GLA_DK // 4
    head_a = lane < GLA_DK
    lane_masks = (head_a, jnp.logical_not(head_a))
    state_mask = jnp.concatenate([head_a, jnp.logical_not(head_a)], axis=0)
    end_row = 0 if reverse else c - 1
    n_chunks = ROW_BLK // c
    offsets = [(n_chunks - 1 - ci) * c if reverse else ci * c for ci in range(n_chunks)]

    def rope(x, rows):
        swapped = jnp.where(first_half, pltpu.roll(x, LANE - GLA_DK // 4, 1), pltpu.roll(x, GLA_DK // 4, 1))
        return x * cos_ref[rows, :] + swapped * sin_ref[rows, :]

    chains = []
    for off in offsets:
        rows = slice(off, off + c)
        q = rope(gq_ref[rows, :], rows) * (GLA_DK ** -0.5)
        k = rope(gk_ref[rows, :], rows)
        logit = _dot(al_ref[rows, :].astype(BF16), wg_ref[...]) + bg_ref[...]
        soft = jnp.log2(1.0 + jnp.exp2(jnp.abs(logit) * -LOG2E))
        g = (jnp.minimum(logit, 0.0) * LOG2E - soft) * (1.0 / GLA_TAU)
        chains.append((q, k, g))
        for hh in range(2):
            head = slice(hh * LANE, (hh + 1) * LANE)
            lb = lb_ref[:, head]
            logit = hf_ref[rows, head]
            e = jnp.exp2(jnp.abs(logit) * -LOG2E)
            r = 1.0 / (1.0 + e)
            sig = jnp.where(logit >= 0, r, e * r)
            sig_neg = jnp.where(logit >= 0, e * r, r)
            g = jnp.log2(lb + (1.0 - lb) * sig)
            k = (1.0 - lb) * sig_neg
            q = _silu(hq_ref[rows, head])
            chains.append((q, k, g))

    g3 = jnp.concatenate([_split3(g) for (_, _, g) in chains], axis=-1)
    cum_all = _dot(tri_ref[...], g3)
    cums = [_sum3(cum_all[:, 3 * LANE * n:3 * LANE * (n + 1)], LANE) for n in range(len(chains))]

    atts = []
    for n, ((q, k, g), cum) in enumerate(zip(chains, cums)):
        exps = _level_exponents(g, cum, scr_ref.at[n], reverse)
        qs = [jnp.where(m, q, 0.0).astype(BF16) for m in lane_masks] if n % 3 == 0 else [q.astype(BF16)]
        atts.append(_pair_weights(qs, k.astype(BF16), exps, mask_ref))

    for ci, off in enumerate(offsets):
        rows = slice(off, off + c)
        (q, k, _), cum, att = chains[3 * ci], cums[3 * ci], atts[3 * ci]
        v = gv_ref[rows, :].astype(BF16)
        cum_end = cum[end_row:end_row + 1, :]
        st = sg_ref[...]
        inter = _dot_nt((q * jnp.exp2(cum)).astype(BF16), st.astype(BF16))
        o_gla = jnp.concatenate([_dot(att[0:c], v[:, 0:LANE]), _dot(att[c:2 * c], v[:, LANE:2 * LANE])],
                                axis=-1) + inter
        kd = (k * jnp.exp2(cum_end - cum)).astype(BF16)
        sg_ref[...] = st * jnp.exp2(cum_end) + jnp.where(state_mask, _dot_tn(v, kd), 0.0)
        o_hg = []
        for hh in range(2):
            (q, k, _), cum, att = chains[3 * ci + 1 + hh], cums[3 * ci + 1 + hh], atts[3 * ci + 1 + hh]
            v = hi_ref[rows, hh * LANE:(hh + 1) * LANE].astype(BF16)
            cum_end = cum[end_row:end_row + 1, :]
            st = sh_ref[hh]
            o_hg.append(_dot(att, v) + _dot_nt((q * jnp.exp2(cum)).astype(BF16), st.astype(BF16)))
            kd = (k * jnp.exp2(cum_end - cum)).astype(BF16)
            sh_ref[hh] = st * jnp.exp2(cum_end) + _dot_tn(v, kd)
        o_all = jnp.concatenate([o_gla] + o_hg, axis=-1)
        if reverse:
            o_ref[rows, :] = o_all
        else:
            o_all = o_all + orev_ref[rows, :]
            for hh in range(2):
                sl = slice(hh * LANE, (hh + 1) * LANE)
                y = _readout(o_all[:, sl], gr_ref[rows, sl], gn_ref[...])
                ya_ref[rows, sl] = y.astype(ya_ref.dtype)
                y = _readout(o_all[:, 2 * LANE + hh * LANE:2 * LANE + (hh + 1) * LANE], hg_ref[rows, sl],
                             hn_ref[...])
                yb_ref[rows, sl] = y.astype(yb_ref.dtype)


def _gated_scans(p, cos_t, sin_t, wg, bg, lb, gla_norm, hg_norm, batch, n_lat_blk):
    assert CHUNK == HEAD_DIM == LANE and GLA_HEADS == HG_HEADS and GLA_HEADS % 2 == 0
    m = p.shape[0]
    n_pairs = GLA_HEADS // 2
    n_blk = n_lat_blk + 1
    col = P_COLS

    def run(reverse, extra_in, extra_specs, out_shape, out_specs):
        def pos(t):
            lat = (n_lat_blk - t) if reverse else (t - 1)
            return jnp.where(t == 0, n_lat_blk, lat)

        def row_blk(b, t):
            return jnp.where(t == 0, batch * n_lat_blk + b, b * n_lat_blk + pos(t))

        def cols(start, width):
            return pl.BlockSpec((ROW_BLK, width), lambda b, h, t: (row_blk(b, t), start // width + h))

        d = 1 if reverse else 0
        tri, mask = _scan_constants(reverse)
        hw = HG_HEADS * HEAD_DIM
        in_specs = [
            cols(col["gla_q"], 2 * GLA_DK), cols(col["gla_k"], 2 * GLA_DK), cols(col["gla_v"], 2 * HEAD_DIM),
            cols(col["hg_q"], 2 * HEAD_DIM), cols(col["hg_f"] + d * hw, 2 * HEAD_DIM),
            cols(col["hg_i"], 2 * HEAD_DIM),
            pl.BlockSpec((ROW_BLK, LANE), lambda b, h, t: (row_blk(b, t), col["a_low"] // LANE)),
            pl.BlockSpec((ROW_BLK, LANE), lambda b, h, t: (pos(t), 0)),
            pl.BlockSpec((ROW_BLK, LANE), lambda b, h, t: (pos(t), 0)),
            pl.BlockSpec((None, None, LANE, LANE), lambda b, h, t: (d, h, 0, 0)),
            pl.BlockSpec((None, None, 1, LANE), lambda b, h, t: (d, h, 0, 0)),
            pl.BlockSpec((None, None, 1, 2 * LANE), lambda b, h, t: (d, h, 0, 0)),
            pl.BlockSpec((CHUNK, CHUNK), lambda b, h, t: (0, 0)),
            pl.BlockSpec((N_LEVELS + 1, 2 * CHUNK, CHUNK), lambda b, h, t: (0, 0, 0)),
        ] + [s(row_blk, cols) for s in extra_specs]
        return pl.pallas_call(
            functools.partial(_scan_kernel, reverse=reverse),
            grid=(batch, n_pairs, n_blk),
            in_specs=in_specs,
            out_specs=[s(row_blk, cols) for s in out_specs],
            out_shape=out_shape,
            scratch_shapes=[
                pltpu.VMEM((3 * ROW_BLK // CHUNK, CHUNK, LANE), F32),
                pltpu.VMEM((2 * HEAD_DIM, LANE), F32),
                pltpu.VMEM((2, HEAD_DIM, HEAD_DIM), F32),
            ],
            compiler_params=pltpu.CompilerParams(vmem_limit_bytes=VMEM_LIMIT),
            name="gated_scan_rev" if reverse else "gated_scan_fwd",
        )(p, p, p, p, p, p, p, cos_t, sin_t, wg, bg, lb, tri, mask, *extra_in)

    def rows_spec(width):
        return lambda row_blk, cols: pl.BlockSpec((ROW_BLK, width), lambda b, h, t: (row_blk(b, t), h))

    def section_spec(name):
        return lambda row_blk, cols: cols(col[name], 2 * HEAD_DIM)

    def const_spec(row_blk, cols):
        return pl.BlockSpec((1, LANE), lambda b, h, t: (0, 0))

    (o_rev,) = run(True, (), (), [jax.ShapeDtypeStruct((m, 2 * n_pairs * 2 * HEAD_DIM), F32)],
                   [rows_spec(4 * LANE)])
    return run(False, (p, p, o_rev, gla_norm.reshape(1, -1), hg_norm.reshape(1, -1)),
               (section_spec("gla_r"), section_spec("hg_g"), rows_spec(4 * LANE), const_spec, const_spec),
               [jax.ShapeDtypeStruct((m, GLA_HEADS * HEAD_DIM), BF16),
                jax.ShapeDtypeStruct((m, HG_HEADS * HEAD_DIM), BF16)],
               [rows_spec(2 * LANE), rows_spec(2 * LANE)])


def _head_norm(x, g):
    return x * lax.rsqrt(jnp.mean(x * x, axis=-1, keepdims=True) + EPS) * g


NA_Q_ROWS = ROW_BLK // GRID_W
NA_SPAN_ROWS = 12


def _na_span_base(n, n_rows, xp=jnp):
    return xp.clip(n * NA_Q_ROWS - WIN_ROWS // 2, 0, n_rows - NA_SPAN_ROWS)


def _na_variants(n_rows):
    n_blocks = n_rows // NA_Q_ROWS
    assert n_rows % NA_Q_ROWS == 0 and n_blocks >= 4

    def rows(n):
        base = int(_na_span_base(n, n_rows, np))
        out = []
        for i in range(NA_Q_ROWS):
            r = n * NA_Q_ROWS + i
            r0 = min(max(r - WIN_ROWS // 2, 0), n_rows - WIN_ROWS)
            out.append([(base + kk - r + WIN_ROWS - 1, r0 <= base + kk < r0 + WIN_ROWS)
                        for kk in range(NA_SPAN_ROWS)])
        return out

    variants = [rows(0), rows(1), rows(n_blocks - 1)]
    assert all(rows(n) == variants[1] for n in range(1, n_blocks - 1))
    return variants


def _na_kernel(q_ref, kl_ref, vl_ref, kc_ref, vc_ref, qn_ref, kn_ref, tbl_ref, y_ref,
               kls_ref, vls_ref, kcs_ref, vcs_ref, bias_ref, *, n_rows, steps_per_batch, batch):
    t = pl.program_id(1)

    @pl.when((t > 0) & ((t - 1) % steps_per_batch == 0))
    def _():
        kls_ref[...] = _head_norm(kl_ref[...], kn_ref[...]).astype(BF16)
        vls_ref[...] = vl_ref[...].astype(BF16)

    @pl.when(t == 0)
    def _():
        kcs_ref[...] = _head_norm(kc_ref[...], kn_ref[...]).astype(BF16)
        vcs_ref[...] = vc_ref[...].astype(BF16)
        n_dr = tbl_ref.shape[0] - 1
        for v, var in enumerate(_na_variants(n_rows)):
            for i, row in enumerate(var):
                for g in range(NA_SPAN_ROWS // 2):
                    pair = [tbl_ref[dr if ok else n_dr] for dr, ok in row[2 * g:2 * g + 2]]
                    assert all(0 <= dr < n_dr for dr, ok in row[2 * g:2 * g + 2] if ok)
                    bias_ref[v, i * GRID_W:(i + 1) * GRID_W, g * LANE:(g + 1) * LANE] = jnp.concatenate(pair, axis=1)

    q = _head_norm(q_ref[...], qn_ref[...] * (HEAD_DIM ** -0.5 * LOG2E)).astype(BF16)

    @pl.when(t == 0)
    def _():
        for b in range(batch):
            rows = slice(b * ROW_BLK, (b + 1) * ROW_BLK)
            s = _dot_nt(q[rows], kcs_ref[rows, :])
            e = jnp.exp2(s - jnp.max(s, axis=-1, keepdims=True))
            o = _dot(e.astype(BF16), vcs_ref[rows, :]) / jnp.sum(e, axis=-1, keepdims=True)
            y_ref[rows, :] = o.astype(y_ref.dtype)

    @pl.when(t > 0)
    def _():
        ctx_rows = pl.ds(pl.multiple_of(((t - 1) // steps_per_batch) * ROW_BLK, ROW_BLK), ROW_BLK)
        kc = kcs_ref[ctx_rows, :]
        vc = vcs_ref[ctx_rows, :]
        for sub in range(TOK_BLK // ROW_BLK):
            rows = slice(sub * ROW_BLK, (sub + 1) * ROW_BLK)
            n = ((t - 1) % steps_per_batch) * (TOK_BLK // ROW_BLK) + sub
            base = _na_span_base(n, n_rows)
            variant = jnp.where(n == 0, 0, jnp.where(n == n_rows // NA_Q_ROWS - 1, 2, 1))
            span = pl.ds(pl.multiple_of(base * GRID_W, GRID_W), NA_SPAN_ROWS * GRID_W)
            s_win = _dot_nt(q[rows], kls_ref[span, :]) + bias_ref[variant]
            s_ctx = _dot_nt(q[rows], kc)
            mx = jnp.maximum(jnp.max(s_win, axis=-1, keepdims=True), jnp.max(s_ctx, axis=-1, keepdims=True))
            e_win = jnp.exp2(s_win - mx)
            e_ctx = jnp.exp2(s_ctx - mx)
            den = jnp.sum(e_win, axis=-1, keepdims=True) + jnp.sum(e_ctx, axis=-1, keepdims=True)
            o = _dot(e_win.astype(BF16), vls_ref[span, :]) + _dot(e_ctx.astype(BF16), vc)
            y_ref[rows, :] = (o / den).astype(y_ref.dtype)


def _na_bias_table(rpb):
    n_heads, n_dr, n_dc = rpb.shape
    c = np.arange(GRID_W)[:, None]
    kc = np.arange(GRID_W)[None, :]
    c0 = np.clip(c - WIN_COLS // 2, 0, GRID_W - WIN_COLS)
    col_valid = (kc >= c0) & (kc < c0 + WIN_COLS)
    pick_dc = (np.arange(n_dc)[:, None, None] == (kc - c + WIN_COLS - 1)[None]) & col_valid[None]
    col = jnp.einsum("hrb,bcq->hrcq", rpb.astype(F32), jnp.asarray(pick_dc, F32), precision=lax.Precision.HIGHEST)
    col = col + jnp.asarray(np.where(col_valid, 0.0, NEG_BIG), F32)
    return jnp.concatenate([col, jnp.full((n_heads, 1, GRID_W, GRID_W), NEG_BIG, F32)], axis=1) * LOG2E


def _neighbourhood_attention(p, q_norm, k_norm, bias_tbl, layer, batch, n_lat_blk):
    m = p.shape[0]
    lat_rows = n_lat_blk * ROW_BLK
    assert lat_rows % TOK_BLK == 0 and batch * ROW_BLK == TOK_BLK
    steps_per_batch = lat_rows // TOK_BLK
    qb, kb, vb = (P_COLS[name] // LANE for name in ("na_q", "na_k", "na_v"))
    ctx_blk = batch * steps_per_batch

    def q_blk(t):
        return jnp.where(t == 0, ctx_blk, t - 1)

    def kv_blk(t):
        return jnp.maximum(t - 1, 0) // steps_per_batch

    kern = functools.partial(_na_kernel, n_rows=lat_rows // GRID_W, steps_per_batch=steps_per_batch, batch=batch)
    return pl.pallas_call(
        kern,
        grid=(NA_HEADS, 1 + batch * steps_per_batch),
        in_specs=[
            pl.BlockSpec((TOK_BLK, LANE), lambda h, t: (q_blk(t), qb + h)),
            pl.BlockSpec((lat_rows, LANE), lambda h, t: (kv_blk(t), kb + h)),
            pl.BlockSpec((lat_rows, LANE), lambda h, t: (kv_blk(t), vb + h)),
            pl.BlockSpec((TOK_BLK, LANE), lambda h, t: (ctx_blk, kb + h)),
            pl.BlockSpec((TOK_BLK, LANE), lambda h, t: (ctx_blk, vb + h)),
            pl.BlockSpec((1, LANE), lambda h, t: (0, 0)),
            pl.BlockSpec((1, LANE), lambda h, t: (0, 0)),
            pl.BlockSpec((None,) + bias_tbl.shape[1:], lambda h, t: (layer * NA_HEADS + h, 0, 0, 0)),
        ],
        out_specs=pl.BlockSpec((TOK_BLK, LANE), lambda h, t: (q_blk(t), h)),
        out_shape=jax.ShapeDtypeStruct((m, NA_HEADS * HEAD_DIM), BF16),
        scratch_shapes=[
            pltpu.VMEM((lat_rows, HEAD_DIM), BF16),
            pltpu.VMEM((lat_rows, HEAD_DIM), BF16),
            pltpu.VMEM((TOK_BLK, HEAD_DIM), BF16),
            pltpu.VMEM((TOK_BLK, HEAD_DIM), BF16),
            pltpu.VMEM((3, ROW_BLK, NA_SPAN_ROWS * GRID_W), F32),
        ],
        compiler_params=pltpu.CompilerParams(vmem_limit_bytes=VMEM_LIMIT),
        name="neighbourhood_attention",
    )(p, p, p, p, p, q_norm.reshape(1, -1), k_norm.reshape(1, -1), bias_tbl)


def _outproj_kernel(xl_ref, xc_ref, ya_ref, yb_ref, yc_ref, mod_ref, g_ref, wa_ref, wb_ref, wc_ref,
                    o_ref, h_ref, *, n_lat):
    i = pl.program_id(0)
    acc = _dot(ya_ref[...], wa_ref[...]) + _dot(yb_ref[...], wb_ref[...]) + _dot(yc_ref[...], wc_ref[...])
    upd = mod_ref[2:3, :] * acc

    def emit(x_ref):
        x1 = x_ref[...] + upd
        o_ref[...] = x1
        h_ref[...] = _norm_mod(x1, g_ref[...], mod_ref[3:4, :], mod_ref[4:5, :]).astype(h_ref.dtype)

    pl.when(i < n_lat)(lambda: emit(xl_ref))
    pl.when(i >= n_lat)(lambda: emit(xc_ref))


def _out_projection(tok, ya, yb, yc, modt, g_mlp, w_out, layer, rows_per_mod, with_ctx):
    d = w_out.shape[2]
    tm = TOK_BLK
    per = rows_per_mod // tm
    n_blk = tok.n_lat + (1 if with_ctx else 0)
    wa, wb, wc = ya.shape[1], yb.shape[1], yc.shape[1]
    return pl.pallas_call(
        functools.partial(_outproj_kernel, n_lat=tok.n_lat),
        grid=(n_blk,),
        in_specs=tok.specs(d) + [
            pl.BlockSpec((tm, wa), lambda i: (i, 0)),
            pl.BlockSpec((tm, wb), lambda i: (i, 0)),
            pl.BlockSpec((tm, wc), lambda i: (i, 0)),
            pl.BlockSpec((None, 6, d), lambda i: (i // per, 0, 0)),
            pl.BlockSpec((1, d), lambda i: (0, 0)),
            pl.BlockSpec((None, wa, d), lambda i: (layer, 0, 0)),
            pl.BlockSpec((None, wb, d), lambda i: (layer, wa // wb, 0)),
            pl.BlockSpec((None, wc, d), lambda i: (layer, (wa + wb) // wc, 0)),
        ],
        out_specs=[pl.BlockSpec((tm, d), lambda i: (i, 0)), pl.BlockSpec((tm, d), lambda i: (i, 0))],
        out_shape=[jax.ShapeDtypeStruct((n_blk * tm, d), F32), jax.ShapeDtypeStruct((n_blk * tm, d), BF16)],
        compiler_params=pltpu.CompilerParams(vmem_limit_bytes=VMEM_LIMIT),
        name="out_projection",
    )(tok.lat, tok.ctx, ya, yb, yc, modt, g_mlp.reshape(1, d), w_out, w_out, w_out)


def _mlp_kernel(*refs, emit_next):
    if emit_next:
        h_ref, x_ref, mod_ref, w1_ref, w2_ref, gn_ref, modn_ref, o_ref, hn_ref, acc_ref = refs
    else:
        h_ref, x_ref, mod_ref, w1_ref, w2_ref, o_ref, acc_ref = refs
    j = pl.program_id(1)

    @pl.when(j == 0)
    def _():
        acc_ref[...] = jnp.zeros_like(acc_ref)

    a = jnp.maximum(_dot(h_ref[...], w1_ref[...]), 0.0)
    acc_ref[...] += _dot((a * a).astype(BF16), w2_ref[...])

    @pl.when(j == pl.num_programs(1) - 1)
    def _():
        x2 = x_ref[...] + mod_ref[5:6, :] * acc_ref[...]
        o_ref[...] = x2
        if emit_next:
            hn_ref[...] = _norm_mod(x2, gn_ref[...], modn_ref[0:1, :], modn_ref[1:2, :]).astype(hn_ref.dtype)


def _mlp(h, x1, modt, w1, w2, rows_per_mod, next_norm):
    m, d = x1.shape
    ff = w1.shape[1]
    tm, tf = TOK_BLK, 1024
    per = rows_per_mod // tm
    emit_next = next_norm is not None

    def rows(i, j):
        return (i, 0)

    def mod_rows(i, j):
        return (i // per, 0, 0)

    in_specs = [
        pl.BlockSpec((tm, d), rows),
        pl.BlockSpec((tm, d), rows),
        pl.BlockSpec((None, 6, d), mod_rows),
        pl.BlockSpec((d, tf), lambda i, j: (0, j)),
        pl.BlockSpec((tf, d), lambda i, j: (j, 0)),
    ]
    args = [h, x1, modt, w1, w2]
    out_specs = [pl.BlockSpec((tm, d), rows)]
    out_shape = [jax.ShapeDtypeStruct((m, d), F32)]
    if emit_next:
        g_next, modt_next = next_norm
        in_specs += [pl.BlockSpec((1, d), lambda i, j: (0, 0)), pl.BlockSpec((None, 6, d), mod_rows)]
        args += [g_next.reshape(1, d), modt_next]
        out_specs.append(pl.BlockSpec((tm, d), rows))
        out_shape.append(jax.ShapeDtypeStruct((m, d), BF16))
    return pl.pallas_call(
        functools.partial(_mlp_kernel, emit_next=emit_next),
        grid=(m // tm, ff // tf),
        in_specs=in_specs,
        out_specs=out_specs,
        out_shape=out_shape,
        scratch_shapes=[pltpu.VMEM((tm, d), F32)],
        compiler_params=pltpu.CompilerParams(vmem_limit_bytes=VMEM_LIMIT_BIG),
        name="mlp",
    )(*args)


def _relayout_kernel(wt_ref, o_ref):
    al, aw = P_SRC["a_low"]
    x = wt_ref[...]
    pad = jnp.zeros((LANE - aw, x.shape[1]), x.dtype)
    cols = jnp.concatenate([x[:al], x[al + aw:], x[al:al + aw], pad], axis=0)
    o_ref[...] = cols.T.astype(o_ref.dtype)


def _permute_w_in(w_in):
    depth, d, n = w_in.shape
    tc = 256
    return pl.pallas_call(
        _relayout_kernel,
        grid=(depth, d // tc),
        in_specs=[pl.BlockSpec((None, n, tc), lambda l, i: (l, 0, i))],
        out_specs=pl.BlockSpec((None, tc, P_WIDTH), lambda l, i: (l, i, 0)),
        out_shape=jax.ShapeDtypeStruct((depth, d, P_WIDTH), BF16),
        compiler_params=pltpu.CompilerParams(vmem_limit_bytes=VMEM_LIMIT_BIG),
        name="w_in_relayout",
    )(jnp.swapaxes(w_in, 1, 2))


def _gla_gate_weights(w_a2, b_a):
    n_pairs = GLA_HEADS // 2
    wg = jnp.zeros((2, n_pairs, LANE, 2 * GLA_DK), F32)
    for dd in range(2):
        blk = w_a2[dd].reshape(GLA_RANK, n_pairs, 2 * GLA_DK).transpose(1, 0, 2)
        wg = wg.at[dd, :, dd * GLA_RANK:(dd + 1) * GLA_RANK, :].set(blk)
    bg = b_a.reshape(2, n_pairs, 1, 2 * GLA_DK)
    return wg.astype(BF16), bg.astype(F32)


def _rope_tables(seq, n_ctx):
    quarter = GLA_DK // 4
    inv_freq = ROPE_BASE ** (-np.arange(quarter, dtype=np.float64) / quarter)
    pos = np.arange(seq)
    lane = np.arange(LANE)
    is_col = (lane % GLA_DK) // (GLA_DK // 2) == 1
    first = (lane % (GLA_DK // 2)) < quarter
    p = np.where(is_col[None, :], (pos % GRID_W)[:, None], (pos // GRID_W)[:, None]).astype(np.float64)
    ang = p * inv_freq[lane % quarter][None, :]
    cos = np.cos(ang)
    sin = np.where(first[None, :], -np.sin(ang), np.sin(ang))
    cos = np.concatenate([cos, np.ones((n_ctx, LANE))], axis=0)
    sin = np.concatenate([sin, np.zeros((n_ctx, LANE))], axis=0)
    return jnp.asarray(cos, F32), jnp.asarray(sin, F32)


def kernel(x, c, ctx, c_ctx, w_mod, b_mod, attn_norm, w_in, gla_w_a2, gla_b_a, gla_norm, hg_lower_bounds, hg_norm, na_q_norm, na_k_norm, na_rpb, w_out, mlp_norm, w_mlp1, w_mlp2):
    batch, seq, d = x.shape
    n_ctx = ctx.shape[1]
    depth = w_mod.shape[0]
    assert seq % ROW_BLK == 0 and n_ctx == ROW_BLK and batch * n_ctx == TOK_BLK and d % LANE == 0
    assert seq % TOK_BLK == 0 and seq % GRID_W == 0
    n_lat_blk = seq // ROW_BLK
    lat_rows = batch * seq
    n_lat_tok = lat_rows // TOK_BLK

    tok = _Tokens(x.reshape(lat_rows, d), ctx.reshape(batch * n_ctx, d), n_lat_tok, 0)

    cc = jnp.zeros((8, d), F32).at[0:batch].set(c).at[batch].set(c_ctx)
    mod = _modulation(cc, w_mod, b_mod)
    modt = mod[:, 0:batch + 1].reshape(depth, batch + 1, 6, d)

    lb_p = jax.nn.softmax(hg_lower_bounds.astype(F32), axis=0)
    lower = jnp.cumsum(lb_p, axis=0) - lb_p[0]
    lower = lower.reshape(depth, 2, HG_HEADS // 2, 1, 2 * HEAD_DIM)
    cos_t, sin_t = _rope_tables(seq, n_ctx)

    bias_tbl = _na_bias_table(na_rpb.reshape((depth * NA_HEADS,) + na_rpb.shape[2:]))
    w_in_b = _permute_w_in(w_in)
    w_out_b = w_out.astype(BF16)

    h = _first_norm(tok, modt[0], attn_norm[0], seq)
    for l in range(depth):
        last = l == depth - 1
        p, w1_b, w2_b = _in_projection(h, w_in_b, w_mlp1, w_mlp2, l)
        wg, bg = _gla_gate_weights(gla_w_a2[l], gla_b_a[l])
        ya, yb = _gated_scans(p, cos_t, sin_t, wg, bg, lower[l], gla_norm[l], hg_norm[l], batch, n_lat_blk)
        yc = _neighbourhood_attention(p, na_q_norm[l], na_k_norm[l], bias_tbl, l, batch, n_lat_blk)
        x1, h2 = _out_projection(tok, ya, yb, yc, modt[l], mlp_norm[l], w_out_b, l, seq, with_ctx=not last)
        if last:
            (xu,) = _mlp(h2, x1, modt[l], w1_b, w2_b, seq, None)
        else:
            xu, h = _mlp(h2, x1, modt[l], w1_b, w2_b, seq, (attn_norm[l + 1], modt[l + 1]))
        tok = _Tokens(xu, xu, n_lat_tok, n_lat_tok)
    return xu.reshape(batch, seq, d)
```

```python
import functools

import numpy as np
import jax
import jax.numpy as jnp
from jax import lax
from jax.experimental import pallas as pl
from jax.experimental.pallas import tpu as pltpu

F32 = jnp.float32
BF16 = jnp.bfloat16

EPS = 1e-6
GRID_W = 64
HEAD_DIM = 128
GLA_HEADS = 4
GLA_DK = 64
GLA_RANK = 16
GLA_TAU = 16.0
HG_HEADS = 4
NA_HEADS = 8
WIN_ROWS = 8
WIN_COLS = 16
ROPE_BASE = 10000.0
NEG_BIG = -1e30
LOG2E = 1.4426950408889634

LANE = 128
ROW_BLK = 256
CHUNK = 128
N_LEVELS = 7
TOK_BLK = 512
VMEM_LIMIT = 48 * 1024 * 1024
VMEM_LIMIT_BIG = 56 * 1024 * 1024


def _projection_layout():
    gw, hw, nw = GLA_HEADS * HEAD_DIM, HG_HEADS * HEAD_DIM, NA_HEADS * HEAD_DIM
    sections = (("gla_q", GLA_HEADS * GLA_DK), ("gla_k", GLA_HEADS * GLA_DK), ("gla_v", gw), ("gla_r", gw),
                ("a_low", 2 * GLA_RANK), ("hg_q", hw), ("hg_f", 2 * hw), ("hg_i", hw), ("hg_g", hw),
                ("na_q", nw), ("na_k", nw), ("na_v", nw))
    src, off = {}, 0
    for name, width in sections:
        src[name] = (off, width)
        off += width
    dst, off = {}, 0
    for name, width in sections:
        if name != "a_low":
            dst[name] = off
            off += width
    dst["a_low"] = off
    return src, dst, off + LANE


P_SRC, P_COLS, P_WIDTH = _projection_layout()


def _dot(a, b):
    return jnp.dot(a, b, preferred_element_type=F32)


def _dot_nt(a, b):
    return lax.dot_general(a, b, (((1,), (1,)), ((), ())), preferred_element_type=F32)


def _dot_tn(a, b):
    return lax.dot_general(a, b, (((0,), (0,)), ((), ())), preferred_element_type=F32)


def _sigmoid(x):
    return 1.0 / (1.0 + jnp.exp(-x))


def _silu(x):
    return x * _sigmoid(x)


def _split3(x):
    hi = x.astype(BF16)
    r1 = x - hi.astype(F32)
    mid = r1.astype(BF16)
    lo = (r1 - mid.astype(F32)).astype(BF16)
    return jnp.concatenate([hi, mid, lo], axis=-1)


def _sum3(y, w):
    return y[:, 0:w] + y[:, w:2 * w] + y[:, 2 * w:3 * w]


def _mod_kernel(c_ref, w_ref, b_ref, o_ref):
    s = _silu(c_ref[...]).astype(BF16)
    o_ref[...] = _dot(s, w_ref[...].astype(BF16)) + b_ref[...]


def _modulation(cc, w_mod, b_mod):
    depth, d, n = w_mod.shape
    tn = 1024
    return pl.pallas_call(
        _mod_kernel,
        grid=(depth, n // tn),
        in_specs=[
            pl.BlockSpec((8, d), lambda l, j: (0, 0)),
            pl.BlockSpec((None, d, tn), lambda l, j: (l, 0, j)),
            pl.BlockSpec((None, 1, tn), lambda l, j: (l, 0, j)),
        ],
        out_specs=pl.BlockSpec((None, 8, tn), lambda l, j: (l, 0, j)),
        out_shape=jax.ShapeDtypeStruct((depth, 8, n), F32),
        compiler_params=pltpu.CompilerParams(vmem_limit_bytes=VMEM_LIMIT),
        name="modulation",
    )(cc, w_mod, b_mod.reshape(depth, 1, n))


def _norm_mod(x, g, shift, scale):
    y = x * lax.rsqrt(jnp.mean(x * x, axis=-1, keepdims=True) + EPS) * g
    return y * (1.0 + scale) + shift


class _Tokens:
    def __init__(self, lat, ctx, n_lat, ctx_blk):
        self.lat, self.ctx, self.n_lat, self.ctx_blk = lat, ctx, n_lat, ctx_blk

    def specs(self, d):
        n_lat, ctx_blk = self.n_lat, self.ctx_blk
        return [pl.BlockSpec((TOK_BLK, d), lambda i, *_: (jnp.minimum(i, n_lat - 1), 0)),
                pl.BlockSpec((TOK_BLK, d), lambda i, *_: (ctx_blk, 0))]


def _first_norm_kernel(xl_ref, xc_ref, mod_ref, g_ref, h_ref, *, n_lat):
    i = pl.program_id(0)

    def emit(x_ref):
        h_ref[...] = _norm_mod(x_ref[...], g_ref[...], mod_ref[0:1, :], mod_ref[1:2, :]).astype(h_ref.dtype)

    pl.when(i < n_lat)(lambda: emit(xl_ref))
    pl.when(i >= n_lat)(lambda: emit(xc_ref))


def _first_norm(tok, modt, g, rows_per_mod):
    d = tok.lat.shape[1]
    per = rows_per_mod // TOK_BLK
    n_blk = tok.n_lat + 1
    return pl.pallas_call(
        functools.partial(_first_norm_kernel, n_lat=tok.n_lat),
        grid=(n_blk,),
        in_specs=tok.specs(d) + [
            pl.BlockSpec((None, 6, d), lambda i: (i // per, 0, 0)),
            pl.BlockSpec((1, d), lambda i: (0, 0)),
        ],
        out_specs=pl.BlockSpec((TOK_BLK, d), lambda i: (i, 0)),
        out_shape=jax.ShapeDtypeStruct((n_blk * TOK_BLK, d), BF16),
        compiler_params=pltpu.CompilerParams(vmem_limit_bytes=VMEM_LIMIT),
        name="first_norm",
    )(tok.lat, tok.ctx, modt, g.reshape(1, d))


def _inproj_kernel(h_ref, w_ref, w1_ref, w2_ref, o_ref, w1o_ref, w2o_ref, *, n_cast):
    o_ref[...] = _dot(h_ref[...], w_ref[...])

    @pl.when(pl.program_id(0) * pl.num_programs(1) + pl.program_id(1) < n_cast)
    def _():
        w1o_ref[...] = w1_ref[...].astype(w1o_ref.dtype)
        w2o_ref[...] = w2_ref[...].astype(w2o_ref.dtype)


def _in_projection(h, w, w1, w2, layer):
    m, d = h.shape
    n = w.shape[2]
    ff = w1.shape[2]
    tn = n // 3
    n_i = m // TOK_BLK
    tc = 256
    n_cast = ff // tc
    assert n_cast <= 3 * n_i

    def cast_blk(j, i):
        return jnp.minimum(j * n_i + i, n_cast - 1)

    return pl.pallas_call(
        functools.partial(_inproj_kernel, n_cast=n_cast),
        grid=(n // tn, n_i),
        in_specs=[
            pl.BlockSpec((TOK_BLK, d), lambda j, i: (i, 0)),
            pl.BlockSpec((None, d, tn), lambda j, i: (layer, 0, j)),
            pl.BlockSpec((None, d, tc), lambda j, i: (layer, 0, cast_blk(j, i))),
            pl.BlockSpec((None, tc, d), lambda j, i: (layer, cast_blk(j, i), 0)),
        ],
        out_specs=[
            pl.BlockSpec((TOK_BLK, tn), lambda j, i: (i, j)),
            pl.BlockSpec((d, tc), lambda j, i: (0, cast_blk(j, i))),
            pl.BlockSpec((tc, d), lambda j, i: (cast_blk(j, i), 0)),
        ],
        out_shape=[
            jax.ShapeDtypeStruct((m, n), F32),
            jax.ShapeDtypeStruct((d, ff), BF16),
            jax.ShapeDtypeStruct((ff, d), BF16),
        ],
        compiler_params=pltpu.CompilerParams(vmem_limit_bytes=VMEM_LIMIT_BIG),
        name="in_projection",
    )(h, w, w1, w2)


def _scan_constants(reverse):
    c = CHUNK
    i = np.arange(c)[:, None]
    j = np.arange(c)[None, :]
    tri = (j >= i) if reverse else (j <= i)
    mask = np.zeros((N_LEVELS + 1, c, c), np.float32)
    for l in range(N_LEVELS):
        level = ((i ^ j) >> l) == 1
        mask[l] = level & ((i < j) if reverse else (i > j))
    mask[N_LEVELS] = np.eye(c, dtype=np.float32)
    return jnp.asarray(tri, BF16), jnp.asarray(np.concatenate([mask, mask], axis=1), BF16)


def _level_exponents(g, cum, scr_ref, reverse):
    c = CHUNK
    scr_ref[...] = cum
    row = lax.broadcasted_iota(jnp.int32, (c, LANE), 0)
    up = pltpu.roll(g, c - 1, 0)
    dn = pltpu.roll(g, 1, 0)
    m2 = row & 1
    m4 = row & 3
    if reverse:
        e0 = jnp.where(m2 == 0, g, 0.0)
        e1 = jnp.where(m4 == 0, g + up, jnp.where(m4 == 1, g, jnp.where(m4 == 2, 0.0, dn)))
    else:
        e0 = jnp.where(m2 == 1, g, 0.0)
        e1 = jnp.where(m4 == 0, up, jnp.where(m4 == 1, 0.0, jnp.where(m4 == 2, g, g + dn)))
    out = [e0, e1]
    for l in range(2, N_LEVELS):
        s = 1 << l
        pieces = []
        for blk in range(c // (2 * s)):
            lo = blk * 2 * s
            r = lo + (s if reverse else s - 1)
            pieces.append(-jnp.abs(cum[lo:lo + 2 * s] - scr_ref[r:r + 1, :]))
        out.append(pieces[0] if len(pieces) == 1 else jnp.concatenate(pieces, axis=0))
    return out


def _pair_weights(qs, k, exps, mask_ref):
    n = len(qs) * CHUNK

    def lhs(w):
        parts = [q if w is None else q * w for q in qs]
        return parts[0] if len(parts) == 1 else jnp.concatenate(parts, axis=0)

    att = mask_ref[N_LEVELS, 0:n, :] * _dot_nt(lhs(None), k).astype(BF16)
    for l, e in enumerate(exps):
        w = jnp.exp2(e).astype(BF16)
        att = att + mask_ref[l, 0:n, :] * _dot_nt(lhs(w), k * w).astype(BF16)
    return att


def _readout(o, gate, g):
    y = o * lax.rsqrt(jnp.mean(o * o, axis=-1, keepdims=True) + EPS) * g
    return y * _silu(gate)


def _scan_kernel(*refs, reverse):
    (gq_ref, gk_ref, gv_ref, hq_ref, hf_ref, hi_ref, al_ref, cos_ref, sin_ref, wg_ref, bg_ref, lb_ref,
     tri_ref, mask_ref) = refs[:14]
    if reverse:
        o_ref, scr_ref, sg_ref, sh_ref = refs[14:]
    else:
        gr_ref, hg_ref, orev_ref, gn_ref, hn_ref, ya_ref, yb_ref, scr_ref, sg_ref, sh_ref = refs[14:]
    c = CHUNK

    @pl.when(pl.program_id(2) == 0)
    def _():
        sg_ref[...] = jnp.zeros_like(sg_ref)
        sh_ref[...] = jnp.zeros_like(sh_ref)

    lane = lax.broadcasted_iota(jnp.int32, (c, LANE), 1)
    first_half = (lane % (GLA_DK // 2)) < GLA_DK // 4
    head_a = lane < GLA_DK
    lane_masks = (head_a, jnp.logical_not(head_a))
    state_mask = jnp.concatenate([head_a, jnp.logical_not(head_a)], axis=0)
    end_row = 0 if reverse else c - 1
    n_chunks = ROW_BLK // c
    offsets = [(n_chunks - 1 - ci) * c if reverse else ci * c for ci in range(n_chunks)]

    def rope(x, rows):
        swapped = jnp.where(first_half, pltpu.roll(x, LANE - GLA_DK // 4, 1), pltpu.roll(x, GLA_DK // 4, 1))
        return x * cos_ref[rows, :] + swapped * sin_ref[rows, :]

    chains = []
    for pr in range(SCAN_PAIRS_PER_STEP):
        gla = slice(pr * LANE, (pr + 1) * LANE)
        for off in offsets:
            rows = slice(off, off + c)
            q = rope(gq_ref[rows, gla], rows) * (GLA_DK ** -0.5)
            k = rope(gk_ref[rows, gla], rows)
            logit = _dot(al_ref[rows, :].astype(BF16), wg_ref[pr]) + bg_ref[pr]
            soft = jnp.log2(1.0 + jnp.exp2(jnp.abs(logit) * -LOG2E))
            g = (jnp.minimum(logit, 0.0) * LOG2E - soft) * (1.0 / GLA_TAU)
            chains.append((q, k, g))
            for hh in range(2):
                head = slice((2 * pr + hh) * LANE, (2 * pr + hh + 1) * LANE)
                lb = lb_ref[pr][:, hh * LANE:(hh + 1) * LANE]
                logit = hf_ref[rows, head]
                e = jnp.exp2(jnp.abs(logit) * -LOG2E)
                r = 1.0 / (1.0 + e)
                sig = jnp.where(logit >= 0, r, e * r)
                sig_neg = jnp.where(logit >= 0, e * r, r)
                g = jnp.log2(lb + (1.0 - lb) * sig)
                k = (1.0 - lb) * sig_neg
                q = _silu(hq_ref[rows, head])
                chains.append((q, k, g))

    g3 = jnp.concatenate([_split3(g) for (_, _, g) in chains], axis=-1)
    cum_all = _dot(tri_ref[...], g3)
    cums = [_sum3(cum_all[:, 3 * LANE * n:3 * LANE * (n + 1)], LANE) for n in range(len(chains))]

    atts = []
    for n, ((q, k, g), cum) in enumerate(zip(chains, cums)):
        exps = _level_exponents(g, cum, scr_ref.at[n], reverse)
        qs = [jnp.where(m, q, 0.0).astype(BF16) for m in lane_masks] if n % 3 == 0 else [q.astype(BF16)]
        atts.append(_pair_weights(qs, k.astype(BF16), exps, mask_ref))

    for pr in range(SCAN_PAIRS_PER_STEP):
        for ci, off in enumerate(offsets):
            rows = slice(off, off + c)
            n0 = (pr * n_chunks + ci) * 3
            (q, k, _), cum, att = chains[n0], cums[n0], atts[n0]
            v = gv_ref[rows, pr * 2 * LANE:(pr + 1) * 2 * LANE].astype(BF16)
            cum_end = cum[end_row:end_row + 1, :]
            st = sg_ref[pr]
            inter = _dot_nt((q * jnp.exp2(cum)).astype(BF16), st.astype(BF16))
            o_gla = jnp.concatenate([_dot(att[0:c], v[:, 0:LANE]), _dot(att[c:2 * c], v[:, LANE:2 * LANE])],
                                    axis=-1) + inter
            kd = (k * jnp.exp2(cum_end - cum)).astype(BF16)
            sg_ref[pr] = st * jnp.exp2(cum_end) + jnp.where(state_mask, _dot_tn(v, kd), 0.0)
            o_hg = []
            for hh in range(2):
                head = slice((2 * pr + hh) * LANE, (2 * pr + hh + 1) * LANE)
                (q, k, _), cum, att = chains[n0 + 1 + hh], cums[n0 + 1 + hh], atts[n0 + 1 + hh]
                v = hi_ref[rows, head].astype(BF16)
                cum_end = cum[end_row:end_row + 1, :]
                st = sh_ref[2 * pr + hh]
                o_hg.append(_dot(att, v) + _dot_nt((q * jnp.exp2(cum)).astype(BF16), st.astype(BF16)))
                kd = (k * jnp.exp2(cum_end - cum)).astype(BF16)
                sh_ref[2 * pr + hh] = st * jnp.exp2(cum_end) + _dot_tn(v, kd)
            o_all = jnp.concatenate([o_gla] + o_hg, axis=-1)
            out_cols = slice(pr * 4 * LANE, (pr + 1) * 4 * LANE)
            if reverse:
                o_ref[rows, out_cols] = o_all
            else:
                o_all = o_all + orev_ref[rows, out_cols]
                for hh in range(2):
                    head = slice((2 * pr + hh) * LANE, (2 * pr + hh + 1) * LANE)
                    y = _readout(o_all[:, hh * LANE:(hh + 1) * LANE], gr_ref[rows, head], gn_ref[...])
                    ya_ref[rows, head] = y.astype(ya_ref.dtype)
                    y = _readout(o_all[:, (2 + hh) * LANE:(3 + hh) * LANE], hg_ref[rows, head], hn_ref[...])
                    yb_ref[rows, head] = y.astype(yb_ref.dtype)


def _gated_scans(p, cos_t, sin_t, wg, bg, lb, gla_norm, hg_norm, batch, n_lat_blk):
    assert CHUNK == HEAD_DIM == LANE and GLA_HEADS == HG_HEADS and GLA_HEADS % 2 == 0
    m = p.shape[0]
    pps = SCAN_PAIRS_PER_STEP
    n_groups = GLA_HEADS // 2 // pps
    assert GLA_HEADS % (2 * pps) == 0
    n_blk = n_lat_blk + 1
    col = P_COLS

    def run(reverse, extra_in, extra_specs, out_shape, out_specs):
        def pos(t):
            lat = (n_lat_blk - t) if reverse else (t - 1)
            return jnp.where(t == 0, n_lat_blk, lat)

        def row_blk(b, t):
            return jnp.where(t == 0, batch * n_lat_blk + b, b * n_lat_blk + pos(t))

        def cols(start, pair_width):
            width = pps * pair_width
            assert start % width == 0
            return pl.BlockSpec((ROW_BLK, width), lambda b, h, t: (row_blk(b, t), start // width + h))

        d = 1 if reverse else 0
        tri, mask = _scan_constants(reverse)
        hw = HG_HEADS * HEAD_DIM
        in_specs = [
            cols(col["gla_q"], 2 * GLA_DK), cols(col["gla_k"], 2 * GLA_DK), cols(col["gla_v"], 2 * HEAD_DIM),
            cols(col["hg_q"], 2 * HEAD_DIM), cols(col["hg_f"] + d * hw, 2 * HEAD_DIM),
            cols(col["hg_i"], 2 * HEAD_DIM),
            pl.BlockSpec((ROW_BLK, LANE), lambda b, h, t: (row_blk(b, t), col["a_low"] // LANE)),
            pl.BlockSpec((ROW_BLK, LANE), lambda b, h, t: (pos(t), 0)),
            pl.BlockSpec((ROW_BLK, LANE), lambda b, h, t: (pos(t), 0)),
            pl.BlockSpec((None, pps, LANE, LANE), lambda b, h, t: (d, h, 0, 0)),
            pl.BlockSpec((None, pps, 1, LANE), lambda b, h, t: (d, h, 0, 0)),
            pl.BlockSpec((None, pps, 1, 2 * LANE), lambda b, h, t: (d, h, 0, 0)),
            pl.BlockSpec((CHUNK, CHUNK), lambda b, h, t: (0, 0)),
            pl.BlockSpec((N_LEVELS + 1, 2 * CHUNK, CHUNK), lambda b, h, t: (0, 0, 0)),
        ] + [s(row_blk, cols) for s in extra_specs]
        return pl.pallas_call(
            functools.partial(_scan_kernel, reverse=reverse),
            grid=(batch, n_groups, n_blk),
            in_specs=in_specs,
            out_specs=[s(row_blk, cols) for s in out_specs],
            out_shape=out_shape,
            scratch_shapes=[
                pltpu.VMEM((pps * 3 * ROW_BLK // CHUNK, CHUNK, LANE), F32),
                pltpu.VMEM((pps, 2 * HEAD_DIM, LANE), F32),
                pltpu.VMEM((pps * 2, HEAD_DIM, HEAD_DIM), F32),
            ],
            compiler_params=pltpu.CompilerParams(vmem_limit_bytes=VMEM_LIMIT),
            name="gated_scan_rev" if reverse else "gated_scan_fwd",
        )(p, p, p, p, p, p, p, cos_t, sin_t, wg, bg, lb, tri, mask, *extra_in)

    def rows_spec(pair_width):
        return lambda row_blk, cols: pl.BlockSpec((ROW_BLK, pps * pair_width),
                                                  lambda b, h, t: (row_blk(b, t), h))

    def section_spec(name):
        return lambda row_blk, cols: cols(col[name], 2 * HEAD_DIM)

    def const_spec(row_blk, cols):
        return pl.BlockSpec((1, LANE), lambda b, h, t: (0, 0))

    (o_rev,) = run(True, (), (), [jax.ShapeDtypeStruct((m, (GLA_HEADS + HG_HEADS) * HEAD_DIM), F32)],
                   [rows_spec(4 * LANE)])
    return run(False, (p, p, o_rev, gla_norm.reshape(1, -1), hg_norm.reshape(1, -1)),
               (section_spec("gla_r"), section_spec("hg_g"), rows_spec(4 * LANE), const_spec, const_spec),
               [jax.ShapeDtypeStruct((m, GLA_HEADS * HEAD_DIM), BF16),
                jax.ShapeDtypeStruct((m, HG_HEADS * HEAD_DIM), BF16)],
               [rows_spec(2 * LANE), rows_spec(2 * LANE)])


def _head_norm(x, g):
    return x * lax.rsqrt(jnp.mean(x * x, axis=-1, keepdims=True) + EPS) * g


NA_Q_ROWS = ROW_BLK // GRID_W
NA_SPAN_ROWS = 12
NA_HEADS_PER_STEP = 2
SCAN_PAIRS_PER_STEP = 2


def _na_span_base(n, n_rows, xp=jnp):
    return xp.clip(n * NA_Q_ROWS - WIN_ROWS // 2, 0, n_rows - NA_SPAN_ROWS)


def _na_variants(n_rows):
    n_blocks = n_rows // NA_Q_ROWS
    assert n_rows % NA_Q_ROWS == 0 and n_blocks >= 4

    def rows(n):
        base = int(_na_span_base(n, n_rows, np))
        out = []
        for i in range(NA_Q_ROWS):
            r = n * NA_Q_ROWS + i
            r0 = min(max(r - WIN_ROWS // 2, 0), n_rows - WIN_ROWS)
            out.append([(base + kk - r + WIN_ROWS - 1, r0 <= base + kk < r0 + WIN_ROWS)
                        for kk in range(NA_SPAN_ROWS)])
        return out

    variants = [rows(0), rows(1), rows(n_blocks - 1)]
    assert all(rows(n) == variants[1] for n in range(1, n_blocks - 1))
    return variants


def _na_kernel(q_ref, kl_ref, vl_ref, kc_ref, vc_ref, qn_ref, kn_ref, tbl_ref, y_ref,
               kls_ref, vls_ref, kcs_ref, vcs_ref, bias_ref, *, n_rows, steps_per_batch, batch):
    t = pl.program_id(1)
    heads = [slice(hd * HEAD_DIM, (hd + 1) * HEAD_DIM) for hd in range(NA_HEADS_PER_STEP)]

    @pl.when((t > 0) & ((t - 1) % steps_per_batch == 0))
    def _():
        for cols in heads:
            kls_ref[:, cols] = _head_norm(kl_ref[:, cols], kn_ref[...]).astype(BF16)
        vls_ref[...] = vl_ref[...].astype(BF16)

    @pl.when(t == 0)
    def _():
        for cols in heads:
            kcs_ref[:, cols] = _head_norm(kc_ref[:, cols], kn_ref[...]).astype(BF16)
        vcs_ref[...] = vc_ref[...].astype(BF16)
        n_dr = tbl_ref.shape[1] - 1
        for hd in range(NA_HEADS_PER_STEP):
            for v, var in enumerate(_na_variants(n_rows)):
                for i, row in enumerate(var):
                    for g in range(NA_SPAN_ROWS // 2):
                        pair = [tbl_ref[hd, dr if ok else n_dr] for dr, ok in row[2 * g:2 * g + 2]]
                        assert all(0 <= dr < n_dr for dr, ok in row[2 * g:2 * g + 2] if ok)
                        bias_ref[hd, v, i * GRID_W:(i + 1) * GRID_W, g * LANE:(g + 1) * LANE] = (
                            jnp.concatenate(pair, axis=1))

    qs = [_head_norm(q_ref[:, cols], qn_ref[...] * (HEAD_DIM ** -0.5 * LOG2E)).astype(BF16) for cols in heads]

    @pl.when(t == 0)
    def _():
        for q, cols in zip(qs, heads):
            for b in range(batch):
                rows = slice(b * ROW_BLK, (b + 1) * ROW_BLK)
                s = _dot_nt(q[rows], kcs_ref[rows, cols])
                e = jnp.exp2(s - jnp.max(s, axis=-1, keepdims=True))
                o = _dot(e.astype(BF16), vcs_ref[rows, cols]) / jnp.sum(e, axis=-1, keepdims=True)
                y_ref[rows, cols] = o.astype(y_ref.dtype)

    @pl.when(t > 0)
    def _():
        ctx_rows = pl.ds(pl.multiple_of(((t - 1) // steps_per_batch) * ROW_BLK, ROW_BLK), ROW_BLK)
        for sub in range(TOK_BLK // ROW_BLK):
            rows = slice(sub * ROW_BLK, (sub + 1) * ROW_BLK)
            n = ((t - 1) % steps_per_batch) * (TOK_BLK // ROW_BLK) + sub
            base = _na_span_base(n, n_rows)
            variant = jnp.where(n == 0, 0, jnp.where(n == n_rows // NA_Q_ROWS - 1, 2, 1))
            span = pl.ds(pl.multiple_of(base * GRID_W, GRID_W), NA_SPAN_ROWS * GRID_W)
            for hd, (q, cols) in enumerate(zip(qs, heads)):
                s_win = _dot_nt(q[rows], kls_ref[span, cols]) + bias_ref[hd, variant]
                s_ctx = _dot_nt(q[rows], kcs_ref[ctx_rows, cols])
                mx = jnp.maximum(jnp.max(s_win, axis=-1, keepdims=True), jnp.max(s_ctx, axis=-1, keepdims=True))
                e_win = jnp.exp2(s_win - mx)
                e_ctx = jnp.exp2(s_ctx - mx)
                den = jnp.sum(e_win, axis=-1, keepdims=True) + jnp.sum(e_ctx, axis=-1, keepdims=True)
                o = (_dot(e_win.astype(BF16), vls_ref[span, cols])
                     + _dot(e_ctx.astype(BF16), vcs_ref[ctx_rows, cols]))
                y_ref[rows, cols] = (o / den).astype(y_ref.dtype)


def _na_bias_table(rpb):
    n_heads, n_dr, n_dc = rpb.shape
    c = np.arange(GRID_W)[:, None]
    kc = np.arange(GRID_W)[None, :]
    c0 = np.clip(c - WIN_COLS // 2, 0, GRID_W - WIN_COLS)
    col_valid = (kc >= c0) & (kc < c0 + WIN_COLS)
    pick_dc = (np.arange(n_dc)[:, None, None] == (kc - c + WIN_COLS - 1)[None]) & col_valid[None]
    col = jnp.einsum("hrb,bcq->hrcq", rpb.astype(F32), jnp.asarray(pick_dc, F32), precision=lax.Precision.HIGHEST)
    col = col + jnp.asarray(np.where(col_valid, 0.0, NEG_BIG), F32)
    return jnp.concatenate([col, jnp.full((n_heads, 1, GRID_W, GRID_W), NEG_BIG, F32)], axis=1) * LOG2E


def _neighbourhood_attention(p, q_norm, k_norm, bias_tbl, layer, batch, n_lat_blk):
    m = p.shape[0]
    lat_rows = n_lat_blk * ROW_BLK
    assert lat_rows % TOK_BLK == 0 and batch * ROW_BLK == TOK_BLK
    steps_per_batch = lat_rows // TOK_BLK
    width = NA_HEADS_PER_STEP * HEAD_DIM
    qb, kb, vb = (P_COLS[name] // width for name in ("na_q", "na_k", "na_v"))
    assert NA_HEADS % NA_HEADS_PER_STEP == 0 and all(P_COLS[name] % width == 0 for name in ("na_q", "na_k", "na_v"))
    ctx_blk = batch * steps_per_batch
    groups = NA_HEADS // NA_HEADS_PER_STEP

    def q_blk(t):
        return jnp.where(t == 0, ctx_blk, t - 1)

    def kv_blk(t):
        return jnp.maximum(t - 1, 0) // steps_per_batch

    kern = functools.partial(_na_kernel, n_rows=lat_rows // GRID_W, steps_per_batch=steps_per_batch, batch=batch)
    return pl.pallas_call(
        kern,
        grid=(groups, 1 + batch * steps_per_batch),
        in_specs=[
            pl.BlockSpec((TOK_BLK, width), lambda h, t: (q_blk(t), qb + h)),
            pl.BlockSpec((lat_rows, width), lambda h, t: (kv_blk(t), kb + h)),
            pl.BlockSpec((lat_rows, width), lambda h, t: (kv_blk(t), vb + h)),
            pl.BlockSpec((TOK_BLK, width), lambda h, t: (ctx_blk, kb + h)),
            pl.BlockSpec((TOK_BLK, width), lambda h, t: (ctx_blk, vb + h)),
            pl.BlockSpec((1, LANE), lambda h, t: (0, 0)),
            pl.BlockSpec((1, LANE), lambda h, t: (0, 0)),
            pl.BlockSpec((NA_HEADS_PER_STEP,) + bias_tbl.shape[1:], lambda h, t: (layer * groups + h, 0, 0, 0)),
        ],
        out_specs=pl.BlockSpec((TOK_BLK, width), lambda h, t: (q_blk(t), h)),
        out_shape=jax.ShapeDtypeStruct((m, NA_HEADS * HEAD_DIM), BF16),
        scratch_shapes=[
            pltpu.VMEM((lat_rows, width), BF16),
            pltpu.VMEM((lat_rows, width), BF16),
            pltpu.VMEM((TOK_BLK, width), BF16),
            pltpu.VMEM((TOK_BLK, width), BF16),
            pltpu.VMEM((NA_HEADS_PER_STEP, 3, ROW_BLK, NA_SPAN_ROWS * GRID_W), F32),
        ],
        compiler_params=pltpu.CompilerParams(vmem_limit_bytes=VMEM_LIMIT),
        name="neighbourhood_attention",
    )(p, p, p, p, p, q_norm.reshape(1, -1), k_norm.reshape(1, -1), bias_tbl)


def _outproj_kernel(xl_ref, xc_ref, ya_ref, yb_ref, yc_ref, mod_ref, g_ref, wa_ref, wb_ref, wc_ref,
                    o_ref, h_ref, *, n_lat):
    i = pl.program_id(0)
    acc = _dot(ya_ref[...], wa_ref[...]) + _dot(yb_ref[...], wb_ref[...]) + _dot(yc_ref[...], wc_ref[...])
    upd = mod_ref[2:3, :] * acc

    def emit(x_ref):
        x1 = x_ref[...] + upd
        o_ref[...] = x1
        h_ref[...] = _norm_mod(x1, g_ref[...], mod_ref[3:4, :], mod_ref[4:5, :]).astype(h_ref.dtype)

    pl.when(i < n_lat)(lambda: emit(xl_ref))
    pl.when(i >= n_lat)(lambda: emit(xc_ref))


def _out_projection(tok, ya, yb, yc, modt, g_mlp, w_out, layer, rows_per_mod, with_ctx):
    d = w_out.shape[2]
    tm = TOK_BLK
    per = rows_per_mod // tm
    n_blk = tok.n_lat + (1 if with_ctx else 0)
    wa, wb, wc = ya.shape[1], yb.shape[1], yc.shape[1]
    return pl.pallas_call(
        functools.partial(_outproj_kernel, n_lat=tok.n_lat),
        grid=(n_blk,),
        in_specs=tok.specs(d) + [
            pl.BlockSpec((tm, wa), lambda i: (i, 0)),
            pl.BlockSpec((tm, wb), lambda i: (i, 0)),
            pl.BlockSpec((tm, wc), lambda i: (i, 0)),
            pl.BlockSpec((None, 6, d), lambda i: (i // per, 0, 0)),
            pl.BlockSpec((1, d), lambda i: (0, 0)),
            pl.BlockSpec((None, wa, d), lambda i: (layer, 0, 0)),
            pl.BlockSpec((None, wb, d), lambda i: (layer, wa // wb, 0)),
            pl.BlockSpec((None, wc, d), lambda i: (layer, (wa + wb) // wc, 0)),
        ],
        out_specs=[pl.BlockSpec((tm, d), lambda i: (i, 0)), pl.BlockSpec((tm, d), lambda i: (i, 0))],
        out_shape=[jax.ShapeDtypeStruct((n_blk * tm, d), F32), jax.ShapeDtypeStruct((n_blk * tm, d), BF16)],
        compiler_params=pltpu.CompilerParams(vmem_limit_bytes=VMEM_LIMIT),
        name="out_projection",
    )(tok.lat, tok.ctx, ya, yb, yc, modt, g_mlp.reshape(1, d), w_out, w_out, w_out)


def _mlp_kernel(*refs, emit_next):
    if emit_next:
        h_ref, x_ref, mod_ref, w1_ref, w2_ref, gn_ref, modn_ref, o_ref, hn_ref, acc_ref = refs
    else:
        h_ref, x_ref, mod_ref, w1_ref, w2_ref, o_ref, acc_ref = refs
    j = pl.program_id(1)

    @pl.when(j == 0)
    def _():
        acc_ref[...] = jnp.zeros_like(acc_ref)

    a = jnp.maximum(_dot(h_ref[...], w1_ref[...]), 0.0)
    acc_ref[...] += _dot((a * a).astype(BF16), w2_ref[...])

    @pl.when(j == pl.num_programs(1) - 1)
    def _():
        x2 = x_ref[...] + mod_ref[5:6, :] * acc_ref[...]
        o_ref[...] = x2
        if emit_next:
            hn_ref[...] = _norm_mod(x2, gn_ref[...], modn_ref[0:1, :], modn_ref[1:2, :]).astype(hn_ref.dtype)


def _mlp(h, x1, modt, w1, w2, rows_per_mod, next_norm):
    m, d = x1.shape
    ff = w1.shape[1]
    tm, tf = TOK_BLK, 1024
    per = rows_per_mod // tm
    emit_next = next_norm is not None

    def rows(i, j):
        return (i, 0)

    def mod_rows(i, j):
        return (i // per, 0, 0)

    in_specs = [
        pl.BlockSpec((tm, d), rows),
        pl.BlockSpec((tm, d), rows),
        pl.BlockSpec((None, 6, d), mod_rows),
        pl.BlockSpec((d, tf), lambda i, j: (0, j)),
        pl.BlockSpec((tf, d), lambda i, j: (j, 0)),
    ]
    args = [h, x1, modt, w1, w2]
    out_specs = [pl.BlockSpec((tm, d), rows)]
    out_shape = [jax.ShapeDtypeStruct((m, d), F32)]
    if emit_next:
        g_next, modt_next = next_norm
        in_specs += [pl.BlockSpec((1, d), lambda i, j: (0, 0)), pl.BlockSpec((None, 6, d), mod_rows)]
        args += [g_next.reshape(1, d), modt_next]
        out_specs.append(pl.BlockSpec((tm, d), rows))
        out_shape.append(jax.ShapeDtypeStruct((m, d), BF16))
    return pl.pallas_call(
        functools.partial(_mlp_kernel, emit_next=emit_next),
        grid=(m // tm, ff // tf),
        in_specs=in_specs,
        out_specs=out_specs,
        out_shape=out_shape,
        scratch_shapes=[pltpu.VMEM((tm, d), F32)],
        compiler_params=pltpu.CompilerParams(vmem_limit_bytes=VMEM_LIMIT_BIG),
        name="mlp",
    )(*args)


def _relayout_kernel(wt_ref, o_ref):
    al, aw = P_SRC["a_low"]
    x = wt_ref[...]
    pad = jnp.zeros((LANE - aw, x.shape[1]), x.dtype)
    cols = jnp.concatenate([x[:al], x[al + aw:], x[al:al + aw], pad], axis=0)
    o_ref[...] = cols.T.astype(o_ref.dtype)


def _permute_w_in(w_in):
    depth, d, n = w_in.shape
    tc = 256
    return pl.pallas_call(
        _relayout_kernel,
        grid=(depth, d // tc),
        in_specs=[pl.BlockSpec((None, n, tc), lambda l, i: (l, 0, i))],
        out_specs=pl.BlockSpec((None, tc, P_WIDTH), lambda l, i: (l, i, 0)),
        out_shape=jax.ShapeDtypeStruct((depth, d, P_WIDTH), BF16),
        compiler_params=pltpu.CompilerParams(vmem_limit_bytes=VMEM_LIMIT_BIG),
        name="w_in_relayout",
    )(jnp.swapaxes(w_in, 1, 2))


def _gla_gate_weights(w_a2, b_a):
    n_pairs = GLA_HEADS // 2
    wg = jnp.zeros((2, n_pairs, LANE, 2 * GLA_DK), F32)
    for dd in range(2):
        blk = w_a2[dd].reshape(GLA_RANK, n_pairs, 2 * GLA_DK).transpose(1, 0, 2)
        wg = wg.at[dd, :, dd * GLA_RANK:(dd + 1) * GLA_RANK, :].set(blk)
    bg = b_a.reshape(2, n_pairs, 1, 2 * GLA_DK)
    return wg.astype(BF16), bg.astype(F32)


def _rope_tables(seq, n_ctx):
    quarter = GLA_DK // 4
    inv_freq = ROPE_BASE ** (-np.arange(quarter, dtype=np.float64) / quarter)
    pos = np.arange(seq)
    lane = np.arange(LANE)
    is_col = (lane % GLA_DK) // (GLA_DK // 2) == 1
    first = (lane % (GLA_DK // 2)) < quarter
    p = np.where(is_col[None, :], (pos % GRID_W)[:, None], (pos // GRID_W)[:, None]).astype(np.float64)
    ang = p * inv_freq[lane % quarter][None, :]
    cos = np.cos(ang)
    sin = np.where(first[None, :], -np.sin(ang), np.sin(ang))
    cos = np.concatenate([cos, np.ones((n_ctx, LANE))], axis=0)
    sin = np.concatenate([sin, np.zeros((n_ctx, LANE))], axis=0)
    return jnp.asarray(cos, F32), jnp.asarray(sin, F32)


def kernel(x, c, ctx, c_ctx, w_mod, b_mod, attn_norm, w_in, gla_w_a2, gla_b_a, gla_norm, hg_lower_bounds, hg_norm, na_q_norm, na_k_norm, na_rpb, w_out, mlp_norm, w_mlp1, w_mlp2):
    batch, seq, d = x.shape
    n_ctx = ctx.shape[1]
    depth = w_mod.shape[0]
    assert seq % ROW_BLK == 0 and n_ctx == ROW_BLK and batch * n_ctx == TOK_BLK and d % LANE == 0
    assert seq % TOK_BLK == 0 and seq % GRID_W == 0
    n_lat_blk = seq // ROW_BLK
    lat_rows = batch * seq
    n_lat_tok = lat_rows // TOK_BLK

    tok = _Tokens(x.reshape(lat_rows, d), ctx.reshape(batch * n_ctx, d), n_lat_tok, 0)

    cc = jnp.zeros((8, d), F32).at[0:batch].set(c).at[batch].set(c_ctx)
    mod = _modulation(cc, w_mod, b_mod)
    modt = mod[:, 0:batch + 1].reshape(depth, batch + 1, 6, d)

    lb_p = jax.nn.softmax(hg_lower_bounds.astype(F32), axis=0)
    lower = jnp.cumsum(lb_p, axis=0) - lb_p[0]
    lower = lower.reshape(depth, 2, HG_HEADS // 2, 1, 2 * HEAD_DIM)
    cos_t, sin_t = _rope_tables(seq, n_ctx)

    bias_tbl = _na_bias_table(na_rpb.reshape((depth * NA_HEADS,) + na_rpb.shape[2:]))
    w_in_b = _permute_w_in(w_in)
    w_out_b = w_out.astype(BF16)

    h = _first_norm(tok, modt[0], attn_norm[0], seq)
    for l in range(depth):
        last = l == depth - 1
        p, w1_b, w2_b = _in_projection(h, w_in_b, w_mlp1, w_mlp2, l)
        wg, bg = _gla_gate_weights(gla_w_a2[l], gla_b_a[l])
        ya, yb = _gated_scans(p, cos_t, sin_t, wg, bg, lower[l], gla_norm[l], hg_norm[l], batch, n_lat_blk)
        yc = _neighbourhood_attention(p, na_q_norm[l], na_k_norm[l], bias_tbl, l, batch, n_lat_blk)
        x1, h2 = _out_projection(tok, ya, yb, yc, modt[l], mlp_norm[l], w_out_b, l, seq, with_ctx=not last)
        if last:
            (xu,) = _mlp(h2, x1, modt[l], w1_b, w2_b, seq, None)
        else:
            xu, h = _mlp(h2, x1, modt[l], w1_b, w2_b, seq, (attn_norm[l + 1], modt[l + 1]))
        tok = _Tokens(xu, xu, n_lat_tok, n_lat_tok)
    return xu.reshape(batch, seq, d)
```

```python
import functools

import numpy as np
import jax
import jax.numpy as jnp
from jax import lax
from jax.experimental import pallas as pl
from jax.experimental.pallas import tpu as pltpu

F32 = jnp.float32
BF16 = jnp.bfloat16

EPS = 1e-6
GRID_W = 64
HEAD_DIM = 128
GLA_HEADS = 4
GLA_DK = 64
GLA_RANK = 16
GLA_TAU = 16.0
HG_HEADS = 4
NA_HEADS = 8
WIN_ROWS = 8
WIN_COLS = 16
ROPE_BASE = 10000.0
NEG_BIG = -1e30
LOG2E = 1.4426950408889634

LANE = 128
ROW_BLK = 256
CHUNK = 128
N_LEVELS = 7
TOK_BLK = 512
VMEM_LIMIT = 48 * 1024 * 1024
VMEM_LIMIT_BIG = 56 * 1024 * 1024


def _projection_layout():
    gw, hw, nw = GLA_HEADS * HEAD_DIM, HG_HEADS * HEAD_DIM, NA_HEADS * HEAD_DIM
    sections = (("gla_q", GLA_HEADS * GLA_DK), ("gla_k", GLA_HEADS * GLA_DK), ("gla_v", gw), ("gla_r", gw),
                ("a_low", 2 * GLA_RANK), ("hg_q", hw), ("hg_f", 2 * hw), ("hg_i", hw), ("hg_g", hw),
                ("na_q", nw), ("na_k", nw), ("na_v", nw))
    src, off = {}, 0
    for name, width in sections:
        src[name] = (off, width)
        off += width
    dst, off = {}, 0
    for name, width in sections:
        if name != "a_low":
            dst[name] = off
            off += width
    dst["a_low"] = off
    return src, dst, off + LANE


P_SRC, P_COLS, P_WIDTH = _projection_layout()


def _dot(a, b):
    return jnp.dot(a, b, preferred_element_type=F32)


def _dot_nt(a, b):
    return lax.dot_general(a, b, (((1,), (1,)), ((), ())), preferred_element_type=F32)


def _dot_tn(a, b):
    return lax.dot_general(a, b, (((0,), (0,)), ((), ())), preferred_element_type=F32)


def _sigmoid(x):
    return 1.0 / (1.0 + jnp.exp(-x))


def _silu(x):
    return x * _sigmoid(x)


def _split3(x):
    hi = x.astype(BF16)
    r1 = x - hi.astype(F32)
    mid = r1.astype(BF16)
    lo = (r1 - mid.astype(F32)).astype(BF16)
    return jnp.concatenate([hi, mid, lo], axis=-1)


def _sum3(y, w):
    return y[:, 0:w] + y[:, w:2 * w] + y[:, 2 * w:3 * w]


def _mod_kernel(c_ref, w_ref, b_ref, o_ref):
    s = _silu(c_ref[...]).astype(BF16)
    o_ref[...] = _dot(s, w_ref[...].astype(BF16)) + b_ref[...]


def _modulation(cc, w_mod, b_mod):
    depth, d, n = w_mod.shape
    tn = 1024
    return pl.pallas_call(
        _mod_kernel,
        grid=(depth, n // tn),
        in_specs=[
            pl.BlockSpec((8, d), lambda l, j: (0, 0)),
            pl.BlockSpec((None, d, tn), lambda l, j: (l, 0, j)),
            pl.BlockSpec((None, 1, tn), lambda l, j: (l, 0, j)),
        ],
        out_specs=pl.BlockSpec((None, 8, tn), lambda l, j: (l, 0, j)),
        out_shape=jax.ShapeDtypeStruct((depth, 8, n), F32),
        compiler_params=pltpu.CompilerParams(vmem_limit_bytes=VMEM_LIMIT),
        name="modulation",
    )(cc, w_mod, b_mod.reshape(depth, 1, n))


def _norm_mod(x, g, shift, scale):
    y = x * lax.rsqrt(jnp.mean(x * x, axis=-1, keepdims=True) + EPS) * g
    return y * (1.0 + scale) + shift


class _Tokens:
    def __init__(self, lat, ctx, n_lat, ctx_blk):
        self.lat, self.ctx, self.n_lat, self.ctx_blk = lat, ctx, n_lat, ctx_blk

    def specs(self, d):
        n_lat, ctx_blk = self.n_lat, self.ctx_blk
        return [pl.BlockSpec((TOK_BLK, d), lambda i, *_: (jnp.minimum(i, n_lat - 1), 0)),
                pl.BlockSpec((TOK_BLK, d), lambda i, *_: (ctx_blk, 0))]


def _first_norm_kernel(xl_ref, xc_ref, mod_ref, g_ref, h_ref, *, n_lat):
    i = pl.program_id(0)

    def emit(x_ref):
        h_ref[...] = _norm_mod(x_ref[...], g_ref[...], mod_ref[0:1, :], mod_ref[1:2, :]).astype(h_ref.dtype)

    pl.when(i < n_lat)(lambda: emit(xl_ref))
    pl.when(i >= n_lat)(lambda: emit(xc_ref))


def _first_norm(tok, modt, g, rows_per_mod):
    d = tok.lat.shape[1]
    per = rows_per_mod // TOK_BLK
    n_blk = tok.n_lat + 1
    return pl.pallas_call(
        functools.partial(_first_norm_kernel, n_lat=tok.n_lat),
        grid=(n_blk,),
        in_specs=tok.specs(d) + [
            pl.BlockSpec((None, 6, d), lambda i: (i // per, 0, 0)),
            pl.BlockSpec((1, d), lambda i: (0, 0)),
        ],
        out_specs=pl.BlockSpec((TOK_BLK, d), lambda i: (i, 0)),
        out_shape=jax.ShapeDtypeStruct((n_blk * TOK_BLK, d), BF16),
        compiler_params=pltpu.CompilerParams(vmem_limit_bytes=VMEM_LIMIT),
        name="first_norm",
    )(tok.lat, tok.ctx, modt, g.reshape(1, d))


def _inproj_kernel(h_ref, w_ref, w1_ref, w2_ref, o_ref, w1o_ref, w2o_ref, *, n_cast):
    o_ref[...] = _dot(h_ref[...], w_ref[...])

    @pl.when(pl.program_id(0) * pl.num_programs(1) + pl.program_id(1) < n_cast)
    def _():
        w1o_ref[...] = w1_ref[...].astype(w1o_ref.dtype)
        w2o_ref[...] = w2_ref[...].astype(w2o_ref.dtype)


def _in_projection(h, w, w1, w2, layer):
    m, d = h.shape
    n = w.shape[2]
    ff = w1.shape[2]
    tn = n // 3
    n_i = m // TOK_BLK
    tc = 256
    n_cast = ff // tc
    assert n_cast <= 3 * n_i

    def cast_blk(j, i):
        return jnp.minimum(j * n_i + i, n_cast - 1)

    return pl.pallas_call(
        functools.partial(_inproj_kernel, n_cast=n_cast),
        grid=(n // tn, n_i),
        in_specs=[
            pl.BlockSpec((TOK_BLK, d), lambda j, i: (i, 0)),
            pl.BlockSpec((None, d, tn), lambda j, i: (layer, 0, j)),
            pl.BlockSpec((None, d, tc), lambda j, i: (layer, 0, cast_blk(j, i))),
            pl.BlockSpec((None, tc, d), lambda j, i: (layer, cast_blk(j, i), 0)),
        ],
        out_specs=[
            pl.BlockSpec((TOK_BLK, tn), lambda j, i: (i, j)),
            pl.BlockSpec((d, tc), lambda j, i: (0, cast_blk(j, i))),
            pl.BlockSpec((tc, d), lambda j, i: (cast_blk(j, i), 0)),
        ],
        out_shape=[
            jax.ShapeDtypeStruct((m, n), F32),
            jax.ShapeDtypeStruct((d, ff), BF16),
            jax.ShapeDtypeStruct((ff, d), BF16),
        ],
        compiler_params=pltpu.CompilerParams(vmem_limit_bytes=VMEM_LIMIT_BIG),
        name="in_projection",
    )(h, w, w1, w2)


def _scan_constants(reverse):
    c = CHUNK
    i = np.arange(c)[:, None]
    j = np.arange(c)[None, :]
    tri = (j >= i) if reverse else (j <= i)
    mask = np.zeros((N_LEVELS + 1, c, c), np.float32)
    for l in range(N_LEVELS):
        level = ((i ^ j) >> l) == 1
        mask[l] = level & ((i < j) if reverse else (i > j))
    mask[N_LEVELS] = np.eye(c, dtype=np.float32)
    return jnp.asarray(tri, BF16), jnp.asarray(np.concatenate([mask, mask], axis=1), BF16)


def _level_exponents(g, cum, scr_ref, reverse):
    c = CHUNK
    scr_ref[...] = cum
    row = lax.broadcasted_iota(jnp.int32, (c, LANE), 0)
    up = pltpu.roll(g, c - 1, 0)
    dn = pltpu.roll(g, 1, 0)
    m2 = row & 1
    m4 = row & 3
    if reverse:
        e0 = jnp.where(m2 == 0, g, 0.0)
        e1 = jnp.where(m4 == 0, g + up, jnp.where(m4 == 1, g, jnp.where(m4 == 2, 0.0, dn)))
    else:
        e0 = jnp.where(m2 == 1, g, 0.0)
        e1 = jnp.where(m4 == 0, up, jnp.where(m4 == 1, 0.0, jnp.where(m4 == 2, g, g + dn)))
    out = [e0, e1]
    for l in range(2, N_LEVELS):
        s = 1 << l
        pieces = []
        for blk in range(c // (2 * s)):
            lo = blk * 2 * s
            r = lo + (s if reverse else s - 1)
            pieces.append(-jnp.abs(cum[lo:lo + 2 * s] - scr_ref[r:r + 1, :]))
        out.append(pieces[0] if len(pieces) == 1 else jnp.concatenate(pieces, axis=0))
    return out


def _pair_weights(qs, k, exps, mask_ref):
    n = len(qs) * CHUNK

    def lhs(w):
        parts = [q if w is None else q * w for q in qs]
        return parts[0] if len(parts) == 1 else jnp.concatenate(parts, axis=0)

    att = mask_ref[N_LEVELS, 0:n, :] * _dot_nt(lhs(None), k).astype(BF16)
    for l, e in enumerate(exps):
        w = jnp.exp2(e).astype(BF16)
        att = att + mask_ref[l, 0:n, :] * _dot_nt(lhs(w), k * w).astype(BF16)
    return att


def _readout(o, gate, g):
    y = o * lax.rsqrt(jnp.mean(o * o, axis=-1, keepdims=True) + EPS) * g
    return y * _silu(gate)


def _mixer_kernel(*refs, reverse, n_lat_blk, n_rows):
    (gq_ref, gk_ref, gv_ref, hq_ref, hf_ref, hi_ref, al_ref, cos_ref, sin_ref, wg_ref, bg_ref, lb_ref,
     tri_ref, mask_ref, naq_ref, nakl_ref, navl_ref, nakc_ref, navc_ref, qn_ref, tbl_ref) = refs[:21]
    if reverse:
        o_ref, yc_ref, scr_ref, sg_ref, sh_ref, bias_ref = refs[21:]
    else:
        (gr_ref, hg_ref, orev_ref, gn_ref, hn_ref, ya_ref, yb_ref, yc_ref,
         scr_ref, sg_ref, sh_ref, bias_ref) = refs[21:]
    c = CHUNK
    t = pl.program_id(2)

    @pl.when(t == 0)
    def _():
        sg_ref[...] = jnp.zeros_like(sg_ref)
        sh_ref[...] = jnp.zeros_like(sh_ref)

    @pl.when((t == 0) & (pl.program_id(0) == 0))
    def _():
        _na_build_bias(tbl_ref, bias_ref, n_rows)

    n_na = jnp.where(t == 0, 0, (n_lat_blk - t) if reverse else (t - 1))

    def attend(*heads):
        _na_block(heads, n_na, t == 0, naq_ref, nakl_ref, navl_ref, nakc_ref, navc_ref, qn_ref, bias_ref,
                  yc_ref, n_rows)

    lane = lax.broadcasted_iota(jnp.int32, (c, LANE), 1)
    first_half = (lane % (GLA_DK // 2)) < GLA_DK // 4
    head_a = lane < GLA_DK
    lane_masks = (head_a, jnp.logical_not(head_a))
    state_mask = jnp.concatenate([head_a, jnp.logical_not(head_a)], axis=0)
    end_row = 0 if reverse else c - 1
    n_chunks = ROW_BLK // c
    offsets = [(n_chunks - 1 - ci) * c if reverse else ci * c for ci in range(n_chunks)]

    def rope(x, rows):
        swapped = jnp.where(first_half, pltpu.roll(x, LANE - GLA_DK // 4, 1), pltpu.roll(x, GLA_DK // 4, 1))
        return x * cos_ref[rows, :] + swapped * sin_ref[rows, :]

    chains = []
    for pr in range(SCAN_PAIRS_PER_STEP):
        gla = slice(pr * LANE, (pr + 1) * LANE)
        for off in offsets:
            rows = slice(off, off + c)
            q = rope(gq_ref[rows, gla], rows) * (GLA_DK ** -0.5)
            k = rope(gk_ref[rows, gla], rows)
            logit = _dot(al_ref[rows, :].astype(BF16), wg_ref[pr]) + bg_ref[pr]
            soft = jnp.log2(1.0 + jnp.exp2(jnp.abs(logit) * -LOG2E))
            g = (jnp.minimum(logit, 0.0) * LOG2E - soft) * (1.0 / GLA_TAU)
            chains.append((q, k, g))
            for hh in range(2):
                head = slice((2 * pr + hh) * LANE, (2 * pr + hh + 1) * LANE)
                lb = lb_ref[pr][:, hh * LANE:(hh + 1) * LANE]
                logit = hf_ref[rows, head]
                e = jnp.exp2(jnp.abs(logit) * -LOG2E)
                r = 1.0 / (1.0 + e)
                sig = jnp.where(logit >= 0, r, e * r)
                sig_neg = jnp.where(logit >= 0, e * r, r)
                g = jnp.log2(lb + (1.0 - lb) * sig)
                k = (1.0 - lb) * sig_neg
                q = _silu(hq_ref[rows, head])
                chains.append((q, k, g))

    g3 = jnp.concatenate([_split3(g) for (_, _, g) in chains], axis=-1)
    cum_all = _dot(tri_ref[...], g3)
    cums = [_sum3(cum_all[:, 3 * LANE * n:3 * LANE * (n + 1)], LANE) for n in range(len(chains))]

    atts = []
    n_na_heads = bias_ref.shape[0]
    for n, ((q, k, g), cum) in enumerate(zip(chains, cums)):
        exps = _level_exponents(g, cum, scr_ref.at[n], reverse)
        qs = [jnp.where(m, q, 0.0).astype(BF16) for m in lane_masks] if n % 3 == 0 else [q.astype(BF16)]
        atts.append(_pair_weights(qs, k.astype(BF16), exps, mask_ref))
        if n % 3 == 2 and n // 3 < n_na_heads:
            attend(n // 3)
    assert len(chains) // 3 >= n_na_heads

    for pr in range(SCAN_PAIRS_PER_STEP):
        for ci, off in enumerate(offsets):
            rows = slice(off, off + c)
            n0 = (pr * n_chunks + ci) * 3
            (q, k, _), cum, att = chains[n0], cums[n0], atts[n0]
            v = gv_ref[rows, pr * 2 * LANE:(pr + 1) * 2 * LANE].astype(BF16)
            cum_end = cum[end_row:end_row + 1, :]
            st = sg_ref[pr]
            inter = _dot_nt((q * jnp.exp2(cum)).astype(BF16), st.astype(BF16))
            o_gla = jnp.concatenate([_dot(att[0:c], v[:, 0:LANE]), _dot(att[c:2 * c], v[:, LANE:2 * LANE])],
                                    axis=-1) + inter
            kd = (k * jnp.exp2(cum_end - cum)).astype(BF16)
            sg_ref[pr] = st * jnp.exp2(cum_end) + jnp.where(state_mask, _dot_tn(v, kd), 0.0)
            o_hg = []
            for hh in range(2):
                head = slice((2 * pr + hh) * LANE, (2 * pr + hh + 1) * LANE)
                (q, k, _), cum, att = chains[n0 + 1 + hh], cums[n0 + 1 + hh], atts[n0 + 1 + hh]
                v = hi_ref[rows, head].astype(BF16)
                cum_end = cum[end_row:end_row + 1, :]
                st = sh_ref[2 * pr + hh]
                o_hg.append(_dot(att, v) + _dot_nt((q * jnp.exp2(cum)).astype(BF16), st.astype(BF16)))
                kd = (k * jnp.exp2(cum_end - cum)).astype(BF16)
                sh_ref[2 * pr + hh] = st * jnp.exp2(cum_end) + _dot_tn(v, kd)
            o_all = jnp.concatenate([o_gla] + o_hg, axis=-1)
            out_cols = slice(pr * 4 * LANE, (pr + 1) * 4 * LANE)
            if reverse:
                o_ref[rows, out_cols] = o_all
            else:
                o_all = o_all + orev_ref[rows, out_cols]
                for hh in range(2):
                    head = slice((2 * pr + hh) * LANE, (2 * pr + hh + 1) * LANE)
                    y = _readout(o_all[:, hh * LANE:(hh + 1) * LANE], gr_ref[rows, head], gn_ref[...])
                    ya_ref[rows, head] = y.astype(ya_ref.dtype)
                    y = _readout(o_all[:, (2 + hh) * LANE:(3 + hh) * LANE], hg_ref[rows, head], hn_ref[...])
                    yb_ref[rows, head] = y.astype(yb_ref.dtype)


def _mixers(p, kn, vb, cos_t, sin_t, wg, bg, lb, gla_norm, hg_norm, q_norm, bias_tbl, layer, batch, n_lat_blk):
    assert CHUNK == HEAD_DIM == LANE and GLA_HEADS == HG_HEADS and NA_HEADS % 2 == 0
    assert GLA_HEADS == 2 * SCAN_PAIRS_PER_STEP
    m = p.shape[0]
    pps = SCAN_PAIRS_PER_STEP
    n_blk = n_lat_blk + 1
    lat_rows = n_lat_blk * ROW_BLK
    na_width = NA_HEADS // 2 * HEAD_DIM
    col = P_COLS
    assert col["na_q"] % na_width == 0

    def run(reverse, extra_in, extra_specs, out_shape, out_specs):
        def pos(t):
            lat = (n_lat_blk - t) if reverse else (t - 1)
            return jnp.where(t == 0, n_lat_blk, lat)

        def row_blk(b, t):
            return jnp.where(t == 0, batch * n_lat_blk + b, b * n_lat_blk + pos(t))

        def cols(start, pair_width):
            width = pps * pair_width
            assert start % width == 0
            return pl.BlockSpec((ROW_BLK, width), lambda b, h, t: (row_blk(b, t), start // width))

        def const(shape):
            return pl.BlockSpec(shape, lambda b, h, t: (0,) * len(shape))

        d = 1 if reverse else 0
        hg = 0 if reverse else 1
        ctx_blk = batch * n_lat_blk
        tri, mask = _scan_constants(reverse)
        hw = HG_HEADS * HEAD_DIM
        in_specs = [
            cols(col["gla_q"], 2 * GLA_DK), cols(col["gla_k"], 2 * GLA_DK), cols(col["gla_v"], 2 * HEAD_DIM),
            cols(col["hg_q"], 2 * HEAD_DIM), cols(col["hg_f"] + d * hw, 2 * HEAD_DIM),
            cols(col["hg_i"], 2 * HEAD_DIM),
            pl.BlockSpec((ROW_BLK, LANE), lambda b, h, t: (row_blk(b, t), col["a_low"] // LANE)),
            pl.BlockSpec((ROW_BLK, LANE), lambda b, h, t: (pos(t), 0)),
            pl.BlockSpec((ROW_BLK, LANE), lambda b, h, t: (pos(t), 0)),
            pl.BlockSpec((None, pps, LANE, LANE), lambda b, h, t: (d, 0, 0, 0)),
            pl.BlockSpec((None, pps, 1, LANE), lambda b, h, t: (d, 0, 0, 0)),
            pl.BlockSpec((None, pps, 1, 2 * LANE), lambda b, h, t: (d, 0, 0, 0)),
            const((CHUNK, CHUNK)),
            const((N_LEVELS + 1, 2 * CHUNK, CHUNK)),
            pl.BlockSpec((ROW_BLK, na_width), lambda b, h, t: (row_blk(b, t), col["na_q"] // na_width + hg)),
            pl.BlockSpec((lat_rows, na_width), lambda b, h, t: (b, hg)),
            pl.BlockSpec((lat_rows, na_width), lambda b, h, t: (b, hg)),
            pl.BlockSpec((ROW_BLK, na_width), lambda b, h, t: (ctx_blk + b, hg)),
            pl.BlockSpec((ROW_BLK, na_width), lambda b, h, t: (ctx_blk + b, hg)),
            const((1, LANE)),
            pl.BlockSpec((NA_HEADS // 2,) + bias_tbl.shape[1:], lambda b, h, t: (2 * layer + hg, 0, 0, 0)),
        ] + [s(row_blk, cols) for s in extra_specs]
        yc_spec = pl.BlockSpec((ROW_BLK, na_width), lambda b, h, t: (row_blk(b, t), 0))
        return pl.pallas_call(
            functools.partial(_mixer_kernel, reverse=reverse, n_lat_blk=n_lat_blk, n_rows=lat_rows // GRID_W),
            grid=(batch, 1, n_blk),
            in_specs=in_specs,
            out_specs=[s(row_blk, cols) for s in out_specs] + [yc_spec],
            out_shape=out_shape + [jax.ShapeDtypeStruct((m, na_width), BF16)],
            scratch_shapes=[
                pltpu.VMEM((pps * 3 * ROW_BLK // CHUNK, CHUNK, LANE), F32),
                pltpu.VMEM((pps, 2 * HEAD_DIM, LANE), F32),
                pltpu.VMEM((pps * 2, HEAD_DIM, HEAD_DIM), F32),
                pltpu.VMEM((NA_HEADS // 2, 4, ROW_BLK, NA_SPAN_ROWS * GRID_W), F32),
            ],
            compiler_params=pltpu.CompilerParams(vmem_limit_bytes=VMEM_LIMIT_BIG),
            name="mixers_rev" if reverse else "mixers_fwd",
        )(p, p, p, p, p, p, p, cos_t, sin_t, wg, bg, lb, tri, mask,
          p, kn, vb, kn, vb, q_norm.reshape(1, -1), bias_tbl, *extra_in)

    def rows_spec(pair_width):
        return lambda row_blk, cols: pl.BlockSpec((ROW_BLK, pps * pair_width),
                                                  lambda b, h, t: (row_blk(b, t), 0))

    def section_spec(name):
        return lambda row_blk, cols: cols(col[name], 2 * HEAD_DIM)

    def const_spec(row_blk, cols):
        return pl.BlockSpec((1, LANE), lambda b, h, t: (0, 0))

    o_rev, yc_rev = run(True, (), (), [jax.ShapeDtypeStruct((m, (GLA_HEADS + HG_HEADS) * HEAD_DIM), F32)],
                        [rows_spec(4 * LANE)])
    ya, yb, yc_fwd = run(False, (p, p, o_rev, gla_norm.reshape(1, -1), hg_norm.reshape(1, -1)),
                         (section_spec("gla_r"), section_spec("hg_g"), rows_spec(4 * LANE), const_spec, const_spec),
                         [jax.ShapeDtypeStruct((m, GLA_HEADS * HEAD_DIM), BF16),
                          jax.ShapeDtypeStruct((m, HG_HEADS * HEAD_DIM), BF16)],
                         [rows_spec(2 * LANE), rows_spec(2 * LANE)])
    return ya, yb, yc_rev, yc_fwd


def _head_norm(x, g):
    return x * lax.rsqrt(jnp.mean(x * x, axis=-1, keepdims=True) + EPS) * g


NA_Q_ROWS = ROW_BLK // GRID_W
NA_SPAN_ROWS = 12
NA_HEADS_PER_STEP = 2
SCAN_PAIRS_PER_STEP = 2


def _na_span_base(n, n_rows, xp=jnp):
    return xp.clip(n * NA_Q_ROWS - WIN_ROWS // 2, 0, n_rows - NA_SPAN_ROWS)


def _na_variants(n_rows):
    n_blocks = n_rows // NA_Q_ROWS
    assert n_rows % NA_Q_ROWS == 0 and n_blocks >= 4

    def rows(n):
        base = int(_na_span_base(n, n_rows, np))
        out = []
        for i in range(NA_Q_ROWS):
            r = n * NA_Q_ROWS + i
            r0 = min(max(r - WIN_ROWS // 2, 0), n_rows - WIN_ROWS)
            out.append([(base + kk - r + WIN_ROWS - 1, r0 <= base + kk < r0 + WIN_ROWS)
                        for kk in range(NA_SPAN_ROWS)])
        return out

    variants = [rows(0), rows(1), rows(n_blocks - 1)]
    assert all(rows(n) == variants[1] for n in range(1, n_blocks - 1))
    return variants


def _na_build_bias(tbl_ref, bias_ref, n_rows):
    n_dr = tbl_ref.shape[1] - 1
    masked = [[(n_dr, False)] * NA_SPAN_ROWS] * NA_Q_ROWS
    for hd in range(tbl_ref.shape[0]):
        for v, var in enumerate(_na_variants(n_rows) + [masked]):
            for i, row in enumerate(var):
                for g in range(NA_SPAN_ROWS // 2):
                    pair = [tbl_ref[hd, dr if ok else n_dr] for dr, ok in row[2 * g:2 * g + 2]]
                    assert all(0 <= dr < n_dr for dr, ok in row[2 * g:2 * g + 2] if ok)
                    bias_ref[hd, v, i * GRID_W:(i + 1) * GRID_W, g * LANE:(g + 1) * LANE] = (
                        jnp.concatenate(pair, axis=1))


def _na_block(heads, n, is_ctx, q_ref, kl_ref, vl_ref, kc_ref, vc_ref, qn_ref, bias_ref, y_ref, n_rows):
    base = _na_span_base(n, n_rows)
    variant = jnp.where(is_ctx, 3, jnp.where(n == 0, 0, jnp.where(n == n_rows // NA_Q_ROWS - 1, 2, 1)))
    span = pl.ds(pl.multiple_of(base * GRID_W, GRID_W), NA_SPAN_ROWS * GRID_W)
    gain = qn_ref[...] * (HEAD_DIM ** -0.5 * LOG2E)
    for hd in heads:
        cols = slice(hd * HEAD_DIM, (hd + 1) * HEAD_DIM)
        q = _head_norm(q_ref[:, cols], gain).astype(BF16)
        s_win = _dot_nt(q, kl_ref[span, cols]) + bias_ref[hd, variant]
        s_ctx = _dot_nt(q, kc_ref[:, cols])
        mx = jnp.maximum(jnp.max(s_win, axis=-1, keepdims=True), jnp.max(s_ctx, axis=-1, keepdims=True))
        e_win = jnp.exp2(s_win - mx)
        e_ctx = jnp.exp2(s_ctx - mx)
        den = jnp.sum(e_win, axis=-1, keepdims=True) + jnp.sum(e_ctx, axis=-1, keepdims=True)
        o = _dot(e_win.astype(BF16), vl_ref[span, cols]) + _dot(e_ctx.astype(BF16), vc_ref[:, cols])
        y_ref[:, cols] = (o / den).astype(y_ref.dtype)


def _na_prep_kernel(k_ref, v_ref, kn_ref, ko_ref, vo_ref):
    for hd in range(k_ref.shape[1] // HEAD_DIM):
        cols = slice(hd * HEAD_DIM, (hd + 1) * HEAD_DIM)
        ko_ref[:, cols] = _head_norm(k_ref[:, cols], kn_ref[...]).astype(ko_ref.dtype)
    vo_ref[...] = v_ref[...].astype(vo_ref.dtype)


def _na_prep(p, k_norm):
    m = p.shape[0]
    width = NA_HEADS * HEAD_DIM
    kb, vb = (P_COLS[name] // width for name in ("na_k", "na_v"))
    assert all(P_COLS[name] % width == 0 for name in ("na_k", "na_v"))
    return pl.pallas_call(
        _na_prep_kernel,
        grid=(m // TOK_BLK,),
        in_specs=[
            pl.BlockSpec((TOK_BLK, width), lambda i: (i, kb)),
            pl.BlockSpec((TOK_BLK, width), lambda i: (i, vb)),
            pl.BlockSpec((1, LANE), lambda i: (0, 0)),
        ],
        out_specs=[pl.BlockSpec((TOK_BLK, width), lambda i: (i, 0))] * 2,
        out_shape=[jax.ShapeDtypeStruct((m, width), BF16)] * 2,
        compiler_params=pltpu.CompilerParams(vmem_limit_bytes=VMEM_LIMIT),
        name="na_prep",
    )(p, p, k_norm.reshape(1, -1))


def _na_bias_table(rpb):
    n_heads, n_dr, n_dc = rpb.shape
    c = np.arange(GRID_W)[:, None]
    kc = np.arange(GRID_W)[None, :]
    c0 = np.clip(c - WIN_COLS // 2, 0, GRID_W - WIN_COLS)
    col_valid = (kc >= c0) & (kc < c0 + WIN_COLS)
    pick_dc = (np.arange(n_dc)[:, None, None] == (kc - c + WIN_COLS - 1)[None]) & col_valid[None]
    col = jnp.einsum("hrb,bcq->hrcq", rpb.astype(F32), jnp.asarray(pick_dc, F32), precision=lax.Precision.HIGHEST)
    col = col + jnp.asarray(np.where(col_valid, 0.0, NEG_BIG), F32)
    return jnp.concatenate([col, jnp.full((n_heads, 1, GRID_W, GRID_W), NEG_BIG, F32)], axis=1) * LOG2E


def _outproj_kernel(*refs, n_lat, n_y):
    xl_ref, xc_ref = refs[:2]
    y_refs = refs[2:2 + n_y]
    mod_ref, g_ref = refs[2 + n_y:4 + n_y]
    w_refs = refs[4 + n_y:4 + 2 * n_y]
    o_ref, h_ref = refs[4 + 2 * n_y:]
    i = pl.program_id(0)
    acc = _dot(y_refs[0][...], w_refs[0][...])
    for y_ref, w_ref in zip(y_refs[1:], w_refs[1:]):
        acc = acc + _dot(y_ref[...], w_ref[...])
    upd = mod_ref[2:3, :] * acc

    def emit(x_ref):
        x1 = x_ref[...] + upd
        o_ref[...] = x1
        h_ref[...] = _norm_mod(x1, g_ref[...], mod_ref[3:4, :], mod_ref[4:5, :]).astype(h_ref.dtype)

    pl.when(i < n_lat)(lambda: emit(xl_ref))
    pl.when(i >= n_lat)(lambda: emit(xc_ref))


def _out_projection(tok, ys, modt, g_mlp, w_out, layer, rows_per_mod, with_ctx):
    d = w_out.shape[2]
    tm = TOK_BLK
    per = rows_per_mod // tm
    n_blk = tok.n_lat + (1 if with_ctx else 0)
    y_specs, w_specs, row0 = [], [], 0
    for y in ys:
        width = y.shape[1]
        assert row0 % width == 0
        y_specs.append(pl.BlockSpec((tm, width), lambda i: (i, 0)))
        w_specs.append(pl.BlockSpec((None, width, d), lambda i, blk=row0 // width: (layer, blk, 0)))
        row0 += width
    assert row0 == w_out.shape[1]
    return pl.pallas_call(
        functools.partial(_outproj_kernel, n_lat=tok.n_lat, n_y=len(ys)),
        grid=(n_blk,),
        in_specs=tok.specs(d) + y_specs + [
            pl.BlockSpec((None, 6, d), lambda i: (i // per, 0, 0)),
            pl.BlockSpec((1, d), lambda i: (0, 0)),
        ] + w_specs,
        out_specs=[pl.BlockSpec((tm, d), lambda i: (i, 0)), pl.BlockSpec((tm, d), lambda i: (i, 0))],
        out_shape=[jax.ShapeDtypeStruct((n_blk * tm, d), F32), jax.ShapeDtypeStruct((n_blk * tm, d), BF16)],
        compiler_params=pltpu.CompilerParams(vmem_limit_bytes=VMEM_LIMIT),
        name="out_projection",
    )(tok.lat, tok.ctx, *ys, modt, g_mlp.reshape(1, d), *([w_out] * len(ys)))


def _mlp_kernel(*refs, emit_next):
    if emit_next:
        h_ref, x_ref, mod_ref, w1_ref, w2_ref, gn_ref, modn_ref, o_ref, hn_ref, acc_ref = refs
    else:
        h_ref, x_ref, mod_ref, w1_ref, w2_ref, o_ref, acc_ref = refs
    j = pl.program_id(1)

    @pl.when(j == 0)
    def _():
        acc_ref[...] = jnp.zeros_like(acc_ref)

    a = jnp.maximum(_dot(h_ref[...], w1_ref[...]), 0.0)
    acc_ref[...] += _dot((a * a).astype(BF16), w2_ref[...])

    @pl.when(j == pl.num_programs(1) - 1)
    def _():
        x2 = x_ref[...] + mod_ref[5:6, :] * acc_ref[...]
        o_ref[...] = x2
        if emit_next:
            hn_ref[...] = _norm_mod(x2, gn_ref[...], modn_ref[0:1, :], modn_ref[1:2, :]).astype(hn_ref.dtype)


def _mlp(h, x1, modt, w1, w2, rows_per_mod, next_norm):
    m, d = x1.shape
    ff = w1.shape[1]
    tm, tf = TOK_BLK, 1024
    per = rows_per_mod // tm
    emit_next = next_norm is not None

    def rows(i, j):
        return (i, 0)

    def mod_rows(i, j):
        return (i // per, 0, 0)

    in_specs = [
        pl.BlockSpec((tm, d), rows),
        pl.BlockSpec((tm, d), rows),
        pl.BlockSpec((None, 6, d), mod_rows),
        pl.BlockSpec((d, tf), lambda i, j: (0, j)),
        pl.BlockSpec((tf, d), lambda i, j: (j, 0)),
    ]
    args = [h, x1, modt, w1, w2]
    out_specs = [pl.BlockSpec((tm, d), rows)]
    out_shape = [jax.ShapeDtypeStruct((m, d), F32)]
    if emit_next:
        g_next, modt_next = next_norm
        in_specs += [pl.BlockSpec((1, d), lambda i, j: (0, 0)), pl.BlockSpec((None, 6, d), mod_rows)]
        args += [g_next.reshape(1, d), modt_next]
        out_specs.append(pl.BlockSpec((tm, d), rows))
        out_shape.append(jax.ShapeDtypeStruct((m, d), BF16))
    return pl.pallas_call(
        functools.partial(_mlp_kernel, emit_next=emit_next),
        grid=(m // tm, ff // tf),
        in_specs=in_specs,
        out_specs=out_specs,
        out_shape=out_shape,
        scratch_shapes=[pltpu.VMEM((tm, d), F32)],
        compiler_params=pltpu.CompilerParams(vmem_limit_bytes=VMEM_LIMIT_BIG),
        name="mlp",
    )(*args)


def _relayout_kernel(wt_ref, o_ref):
    al, aw = P_SRC["a_low"]
    x = wt_ref[...]
    pad = jnp.zeros((LANE - aw, x.shape[1]), x.dtype)
    cols = jnp.concatenate([x[:al], x[al + aw:], x[al:al + aw], pad], axis=0)
    o_ref[...] = cols.T.astype(o_ref.dtype)


def _permute_w_in(w_in):
    depth, d, n = w_in.shape
    tc = 256
    return pl.pallas_call(
        _relayout_kernel,
        grid=(depth, d // tc),
        in_specs=[pl.BlockSpec((None, n, tc), lambda l, i: (l, 0, i))],
        out_specs=pl.BlockSpec((None, tc, P_WIDTH), lambda l, i: (l, i, 0)),
        out_shape=jax.ShapeDtypeStruct((depth, d, P_WIDTH), BF16),
        compiler_params=pltpu.CompilerParams(vmem_limit_bytes=VMEM_LIMIT_BIG),
        name="w_in_relayout",
    )(jnp.swapaxes(w_in, 1, 2))


def _gla_gate_weights(w_a2, b_a):
    n_pairs = GLA_HEADS // 2
    wg = jnp.zeros((2, n_pairs, LANE, 2 * GLA_DK), F32)
    for dd in range(2):
        blk = w_a2[dd].reshape(GLA_RANK, n_pairs, 2 * GLA_DK).transpose(1, 0, 2)
        wg = wg.at[dd, :, dd * GLA_RANK:(dd + 1) * GLA_RANK, :].set(blk)
    bg = b_a.reshape(2, n_pairs, 1, 2 * GLA_DK)
    return wg.astype(BF16), bg.astype(F32)


def _rope_tables(seq, n_ctx):
    quarter = GLA_DK // 4
    inv_freq = ROPE_BASE ** (-np.arange(quarter, dtype=np.float64) / quarter)
    pos = np.arange(seq)
    lane = np.arange(LANE)
    is_col = (lane % GLA_DK) // (GLA_DK // 2) == 1
    first = (lane % (GLA_DK // 2)) < quarter
    p = np.where(is_col[None, :], (pos % GRID_W)[:, None], (pos // GRID_W)[:, None]).astype(np.float64)
    ang = p * inv_freq[lane % quarter][None, :]
    cos = np.cos(ang)
    sin = np.where(first[None, :], -np.sin(ang), np.sin(ang))
    cos = np.concatenate([cos, np.ones((n_ctx, LANE))], axis=0)
    sin = np.concatenate([sin, np.zeros((n_ctx, LANE))], axis=0)
    return jnp.asarray(cos, F32), jnp.asarray(sin, F32)


def kernel(x, c, ctx, c_ctx, w_mod, b_mod, attn_norm, w_in, gla_w_a2, gla_b_a, gla_norm, hg_lower_bounds, hg_norm, na_q_norm, na_k_norm, na_rpb, w_out, mlp_norm, w_mlp1, w_mlp2):
    batch, seq, d = x.shape
    n_ctx = ctx.shape[1]
    depth = w_mod.shape[0]
    assert seq % ROW_BLK == 0 and n_ctx == ROW_BLK and batch * n_ctx == TOK_BLK and d % LANE == 0
    assert seq % TOK_BLK == 0 and seq % GRID_W == 0
    n_lat_blk = seq // ROW_BLK
    lat_rows = batch * seq
    n_lat_tok = lat_rows // TOK_BLK

    tok = _Tokens(x.reshape(lat_rows, d), ctx.reshape(batch * n_ctx, d), n_lat_tok, 0)

    cc = jnp.zeros((8, d), F32).at[0:batch].set(c).at[batch].set(c_ctx)
    mod = _modulation(cc, w_mod, b_mod)
    modt = mod[:, 0:batch + 1].reshape(depth, batch + 1, 6, d)

    lb_p = jax.nn.softmax(hg_lower_bounds.astype(F32), axis=0)
    lower = jnp.cumsum(lb_p, axis=0) - lb_p[0]
    lower = lower.reshape(depth, 2, HG_HEADS // 2, 1, 2 * HEAD_DIM)
    cos_t, sin_t = _rope_tables(seq, n_ctx)

    bias_tbl = _na_bias_table(na_rpb.reshape((depth * NA_HEADS,) + na_rpb.shape[2:]))
    w_in_b = _permute_w_in(w_in)
    w_out_b = w_out.astype(BF16)

    h = _first_norm(tok, modt[0], attn_norm[0], seq)
    for l in range(depth):
        last = l == depth - 1
        p, w1_b, w2_b = _in_projection(h, w_in_b, w_mlp1, w_mlp2, l)
        wg, bg = _gla_gate_weights(gla_w_a2[l], gla_b_a[l])
        kn, vb = _na_prep(p, na_k_norm[l])
        ys = _mixers(p, kn, vb, cos_t, sin_t, wg, bg, lower[l], gla_norm[l], hg_norm[l], na_q_norm[l],
                     bias_tbl, l, batch, n_lat_blk)
        x1, h2 = _out_projection(tok, ys, modt[l], mlp_norm[l], w_out_b, l, seq, with_ctx=not last)
        if last:
            (xu,) = _mlp(h2, x1, modt[l], w1_b, w2_b, seq, None)
        else:
            xu, h = _mlp(h2, x1, modt[l], w1_b, w2_b, seq, (attn_norm[l + 1], modt[l + 1]))
        tok = _Tokens(xu, xu, n_lat_tok, n_lat_tok)
    return xu.reshape(batch, seq, d)
```

```python
import functools

import numpy as np
import jax
import jax.numpy as jnp
from jax import lax
from jax.experimental import pallas as pl
from jax.experimental.pallas import tpu as pltpu

F32 = jnp.float32
BF16 = jnp.bfloat16

EPS = 1e-6
GRID_W = 64
HEAD_DIM = 128
GLA_HEADS = 4
GLA_DK = 64
GLA_RANK = 16
GLA_TAU = 16.0
HG_HEADS = 4
NA_HEADS = 8
WIN_ROWS = 8
WIN_COLS = 16
ROPE_BASE = 10000.0
NEG_BIG = -1e30
LOG2E = 1.4426950408889634

LANE = 128
ROW_BLK = 256
CHUNK = 128
N_LEVELS = 7
TOK_BLK = 512
MLP_TF = 1024
VMEM_LIMIT = 48 * 1024 * 1024
VMEM_LIMIT_BIG = 56 * 1024 * 1024

NA_Q_ROWS = ROW_BLK // GRID_W
NA_SPAN_ROWS = 12
NA_HEADS_PER_STEP = 2
SCAN_PAIRS_PER_STEP = 2


def _projection_layout():
    gw, hw, nw = GLA_HEADS * HEAD_DIM, HG_HEADS * HEAD_DIM, NA_HEADS * HEAD_DIM
    sections = (("gla_q", GLA_HEADS * GLA_DK), ("gla_k", GLA_HEADS * GLA_DK), ("gla_v", gw), ("gla_r", gw),
                ("a_low", 2 * GLA_RANK), ("hg_q", hw), ("hg_f", 2 * hw), ("hg_i", hw), ("hg_g", hw),
                ("na_q", nw), ("na_k", nw), ("na_v", nw))
    src, off = {}, 0
    for name, width in sections:
        src[name] = (off, width)
        off += width
    dst, off = {}, 0
    for name, width in sections:
        if name != "a_low":
            dst[name] = off
            off += width
    dst["a_low"] = off
    return src, dst, off + LANE


P_SRC, P_COLS, P_WIDTH = _projection_layout()


def _dot(a, b):
    return jnp.dot(a, b, preferred_element_type=F32)


def _dot_nt(a, b):
    return lax.dot_general(a, b, (((1,), (1,)), ((), ())), preferred_element_type=F32)


def _dot_tn(a, b):
    return lax.dot_general(a, b, (((0,), (0,)), ((), ())), preferred_element_type=F32)


def _sigmoid(x):
    return 1.0 / (1.0 + jnp.exp(-x))


def _silu(x):
    return x * _sigmoid(x)


def _split3(x):
    hi = x.astype(BF16)
    r1 = x - hi.astype(F32)
    mid = r1.astype(BF16)
    lo = (r1 - mid.astype(F32)).astype(BF16)
    return jnp.concatenate([hi, mid, lo], axis=-1)


def _sum3(y, w):
    return y[:, 0:w] + y[:, w:2 * w] + y[:, 2 * w:3 * w]


def _mod_kernel(c_ref, w_ref, b_ref, o_ref):
    s = _silu(c_ref[...]).astype(BF16)
    o_ref[...] = _dot(s, w_ref[...].astype(BF16)) + b_ref[...]


def _modulation(cc, w_mod, b_mod):
    depth, d, n = w_mod.shape
    tn = 1024
    return pl.pallas_call(
        _mod_kernel,
        grid=(depth, n // tn),
        in_specs=[
            pl.BlockSpec((8, d), lambda l, j: (0, 0)),
            pl.BlockSpec((None, d, tn), lambda l, j: (l, 0, j)),
            pl.BlockSpec((None, 1, tn), lambda l, j: (l, 0, j)),
        ],
        out_specs=pl.BlockSpec((None, 8, tn), lambda l, j: (l, 0, j)),
        out_shape=jax.ShapeDtypeStruct((depth, 8, n), F32),
        compiler_params=pltpu.CompilerParams(vmem_limit_bytes=VMEM_LIMIT),
        name="modulation",
    )(cc, w_mod, b_mod.reshape(depth, 1, n))


def _norm_mod(x, g, shift, scale):
    y = x * lax.rsqrt(jnp.mean(x * x, axis=-1, keepdims=True) + EPS) * g
    return y * (1.0 + scale) + shift


class _Tokens:
    def __init__(self, lat, ctx, n_lat, ctx_blk):
        self.lat, self.ctx, self.n_lat, self.ctx_blk = lat, ctx, n_lat, ctx_blk

    def specs(self, d):
        n_lat, ctx_blk = self.n_lat, self.ctx_blk
        return [pl.BlockSpec((TOK_BLK, d), lambda i, *_: (jnp.minimum(i, n_lat - 1), 0)),
                pl.BlockSpec((TOK_BLK, d), lambda i, *_: (ctx_blk, 0))]


def _first_norm_kernel(xl_ref, xc_ref, mod_ref, g_ref, h_ref, *, n_lat):
    i = pl.program_id(0)

    def emit(x_ref):
        h_ref[...] = _norm_mod(x_ref[...], g_ref[...], mod_ref[0:1, :], mod_ref[1:2, :]).astype(h_ref.dtype)

    pl.when(i < n_lat)(lambda: emit(xl_ref))
    pl.when(i >= n_lat)(lambda: emit(xc_ref))


def _first_norm(tok, modt, g, rows_per_mod):
    d = tok.lat.shape[1]
    per = rows_per_mod // TOK_BLK
    n_blk = tok.n_lat + 1
    return pl.pallas_call(
        functools.partial(_first_norm_kernel, n_lat=tok.n_lat),
        grid=(n_blk,),
        in_specs=tok.specs(d) + [
            pl.BlockSpec((None, 6, d), lambda i: (i // per, 0, 0)),
            pl.BlockSpec((1, d), lambda i: (0, 0)),
        ],
        out_specs=pl.BlockSpec((TOK_BLK, d), lambda i: (i, 0)),
        out_shape=jax.ShapeDtypeStruct((n_blk * TOK_BLK, d), BF16),
        compiler_params=pltpu.CompilerParams(vmem_limit_bytes=VMEM_LIMIT),
        name="first_norm",
    )(tok.lat, tok.ctx, modt, g.reshape(1, d))


def _inproj_kernel(h_ref, w_ref, w1_ref, w2_ref, wo_ref, o_ref, w1o_ref, w2o_ref, woo_ref, *, n_cast):
    o_ref[...] = _dot(h_ref[...], w_ref[...])

    @pl.when(pl.program_id(0) * pl.num_programs(1) + pl.program_id(1) < n_cast)
    def _():
        w1o_ref[...] = w1_ref[...].astype(w1o_ref.dtype)
        w2o_ref[...] = w2_ref[...].astype(w2o_ref.dtype)
        woo_ref[...] = wo_ref[...].astype(woo_ref.dtype)


def _in_projection(h, w, w1, w2, w_out, layer):
    m, d = h.shape
    n = w.shape[2]
    ff = w1.shape[2]
    d_out = w_out.shape[2]
    tn = n // 3
    n_i = m // TOK_BLK
    tc = 256
    n_cast = ff // tc
    to = w_out.shape[1] // n_cast
    per_slab = MLP_TF // tc
    assert n_cast <= 3 * n_i and w_out.shape[1] % n_cast == 0 and to % 16 == 0 and MLP_TF % tc == 0

    def cast_blk(j, i):
        return jnp.minimum(j * n_i + i, n_cast - 1)

    return pl.pallas_call(
        functools.partial(_inproj_kernel, n_cast=n_cast),
        grid=(n // tn, n_i),
        in_specs=[
            pl.BlockSpec((TOK_BLK, d), lambda j, i: (i, 0)),
            pl.BlockSpec((None, d, tn), lambda j, i: (layer, 0, j)),
            pl.BlockSpec((None, d, tc), lambda j, i: (layer, 0, cast_blk(j, i))),
            pl.BlockSpec((None, tc, d), lambda j, i: (layer, cast_blk(j, i), 0)),
            pl.BlockSpec((None, to, d_out), lambda j, i: (layer, cast_blk(j, i), 0)),
        ],
        out_specs=[
            pl.BlockSpec((TOK_BLK, tn), lambda j, i: (i, j)),
            pl.BlockSpec((None, d, tc), lambda j, i: (cast_blk(j, i) // per_slab, 0, cast_blk(j, i) % per_slab)),
            pl.BlockSpec((tc, d), lambda j, i: (cast_blk(j, i), 0)),
            pl.BlockSpec((to, d_out), lambda j, i: (cast_blk(j, i), 0)),
        ],
        out_shape=[
            jax.ShapeDtypeStruct((m, n), F32),
            jax.ShapeDtypeStruct((ff // MLP_TF, d, MLP_TF), BF16),
            jax.ShapeDtypeStruct((ff, d), BF16),
            jax.ShapeDtypeStruct(w_out.shape[1:], BF16),
        ],
        compiler_params=pltpu.CompilerParams(vmem_limit_bytes=VMEM_LIMIT_BIG),
        name="in_projection",
    )(h, w, w1, w2, w_out)


def _scan_constants(reverse):
    c = CHUNK
    i = np.arange(c)[:, None]
    j = np.arange(c)[None, :]
    tri = (j >= i) if reverse else (j <= i)
    mask = np.zeros((N_LEVELS + 1, c, c), np.float32)
    for l in range(N_LEVELS):
        level = ((i ^ j) >> l) == 1
        mask[l] = level & ((i < j) if reverse else (i > j))
    mask[N_LEVELS] = np.eye(c, dtype=np.float32)
    return jnp.asarray(tri, BF16), jnp.asarray(np.concatenate([mask, mask], axis=1), BF16)


def _level_exponents(g, cum, scr_ref, reverse):
    c = CHUNK
    scr_ref[...] = cum
    row = lax.broadcasted_iota(jnp.int32, (c, LANE), 0)
    up = pltpu.roll(g, c - 1, 0)
    dn = pltpu.roll(g, 1, 0)
    m2 = row & 1
    m4 = row & 3
    if reverse:
        e0 = jnp.where(m2 == 0, g, 0.0)
        e1 = jnp.where(m4 == 0, g + up, jnp.where(m4 == 1, g, jnp.where(m4 == 2, 0.0, dn)))
    else:
        e0 = jnp.where(m2 == 1, g, 0.0)
        e1 = jnp.where(m4 == 0, up, jnp.where(m4 == 1, 0.0, jnp.where(m4 == 2, g, g + dn)))
    out = [e0, e1]
    for l in range(2, N_LEVELS):
        s = 1 << l
        pieces = []
        for blk in range(c // (2 * s)):
            lo = blk * 2 * s
            ref = scr_ref[lo + (s if reverse else s - 1):lo + (s if reverse else s - 1) + 1, :]
            if s < 8:
                pieces.append(-jnp.abs(cum[lo:lo + 2 * s] - ref))
            else:
                first, second = cum[lo:lo + s], cum[lo + s:lo + 2 * s]
                pieces += [first - ref, ref - second] if reverse else [ref - first, second - ref]
        out.append(pieces[0] if len(pieces) == 1 else jnp.concatenate(pieces, axis=0))
    return out


def _pair_weights(qs, k, exps, mask_ref):
    n = len(qs) * CHUNK

    def lhs(w):
        parts = [q if w is None else q * w for q in qs]
        return parts[0] if len(parts) == 1 else jnp.concatenate(parts, axis=0)

    att = mask_ref[N_LEVELS, 0:n, :] * _dot_nt(lhs(None), k).astype(BF16)
    for l, e in enumerate(exps):
        w = jnp.exp2(e).astype(BF16)
        att = att + mask_ref[l, 0:n, :] * _dot_nt(lhs(w), k * w).astype(BF16)
    return att


def _readout(o, gate, g):
    y = o * lax.rsqrt(jnp.mean(o * o, axis=-1, keepdims=True) + EPS) * g
    return y * _silu(gate)


def _scan_kernel(*refs, reverse):
    (gq_ref, gk_ref, gv_ref, hq_ref, hf_ref, hi_ref, al_ref, cos_ref, sin_ref, wg_ref, bg_ref, lb_ref,
     tri_ref, mask_ref) = refs[:14]
    if reverse:
        o_ref, scr_ref, sg_ref, sh_ref = refs[14:]
    else:
        gr_ref, hg_ref, orev_ref, gn_ref, hn_ref, ya_ref, yb_ref, scr_ref, sg_ref, sh_ref = refs[14:]
    c = CHUNK

    @pl.when(pl.program_id(2) == 0)
    def _():
        sg_ref[...] = jnp.zeros_like(sg_ref)
        sh_ref[...] = jnp.zeros_like(sh_ref)

    lane = lax.broadcasted_iota(jnp.int32, (c, LANE), 1)
    first_half = (lane % (GLA_DK // 2)) < GLA_DK // 4
    head_a = lane < GLA_DK
    lane_masks = (head_a, jnp.logical_not(head_a))
    state_mask = jnp.concatenate([head_a, jnp.logical_not(head_a)], axis=0)
    end_row = 0 if reverse else c - 1
    n_chunks = ROW_BLK // c
    offsets = [(n_chunks - 1 - ci) * c if reverse else ci * c for ci in range(n_chunks)]

    def rope(x, rows):
        swapped = jnp.where(first_half, pltpu.roll(x, LANE - GLA_DK // 4, 1), pltpu.roll(x, GLA_DK // 4, 1))
        return x * cos_ref[rows, :] + swapped * sin_ref[rows, :]

    chains = []
    for pr in range(SCAN_PAIRS_PER_STEP):
        gla = slice(pr * LANE, (pr + 1) * LANE)
        for off in offsets:
            rows = slice(off, off + c)
            q = rope(gq_ref[rows, gla], rows) * (GLA_DK ** -0.5)
            k = rope(gk_ref[rows, gla], rows)
            logit = _dot(al_ref[rows, :].astype(BF16), wg_ref[pr]) + bg_ref[pr]
            soft = jnp.log2(1.0 + jnp.exp2(jnp.abs(logit) * -LOG2E))
            g = (jnp.minimum(logit, 0.0) * LOG2E - soft) * (1.0 / GLA_TAU)
            chains.append((q, k, g))
            for hh in range(2):
                head = slice((2 * pr + hh) * LANE, (2 * pr + hh + 1) * LANE)
                lb = lb_ref[pr][:, hh * LANE:(hh + 1) * LANE]
                logit = hf_ref[rows, head]
                e = jnp.exp2(jnp.abs(logit) * -LOG2E)
                r = 1.0 / (1.0 + e)
                sig = jnp.where(logit >= 0, r, e * r)
                sig_neg = jnp.where(logit >= 0, e * r, r)
                g = jnp.log2(lb + (1.0 - lb) * sig)
                k = (1.0 - lb) * sig_neg
                q = _silu(hq_ref[rows, head])
                chains.append((q, k, g))

    g3 = jnp.concatenate([_split3(g) for (_, _, g) in chains], axis=-1)
    cum_all = _dot(tri_ref[...], g3)
    cums = [_sum3(cum_all[:, 3 * LANE * n:3 * LANE * (n + 1)], LANE) for n in range(len(chains))]

    atts = []
    for n, ((q, k, g), cum) in enumerate(zip(chains, cums)):
        exps = _level_exponents(g, cum, scr_ref.at[n], reverse)
        qs = [jnp.where(m, q, 0.0).astype(BF16) for m in lane_masks] if n % 3 == 0 else [q.astype(BF16)]
        atts.append(_pair_weights(qs, k.astype(BF16), exps, mask_ref))

    for pr in range(SCAN_PAIRS_PER_STEP):
        for ci, off in enumerate(offsets):
            rows = slice(off, off + c)
            n0 = (pr * n_chunks + ci) * 3
            (q, k, _), cum, att = chains[n0], cums[n0], atts[n0]
            v = gv_ref[rows, pr * 2 * LANE:(pr + 1) * 2 * LANE].astype(BF16)
            cum_end = cum[end_row:end_row + 1, :]
            st = sg_ref[pr]
            inter = _dot_nt((q * jnp.exp2(cum)).astype(BF16), st.astype(BF16))
            o_gla = jnp.concatenate([_dot(att[0:c], v[:, 0:LANE]), _dot(att[c:2 * c], v[:, LANE:2 * LANE])],
                                    axis=-1) + inter
            kd = (k * jnp.exp2(cum_end - cum)).astype(BF16)
            sg_ref[pr] = st * jnp.exp2(cum_end) + jnp.where(state_mask, _dot_tn(v, kd), 0.0)
            o_hg = []
            for hh in range(2):
                head = slice((2 * pr + hh) * LANE, (2 * pr + hh + 1) * LANE)
                (q, k, _), cum, att = chains[n0 + 1 + hh], cums[n0 + 1 + hh], atts[n0 + 1 + hh]
                v = hi_ref[rows, head].astype(BF16)
                cum_end = cum[end_row:end_row + 1, :]
                st = sh_ref[2 * pr + hh]
                o_hg.append(_dot(att, v) + _dot_nt((q * jnp.exp2(cum)).astype(BF16), st.astype(BF16)))
                kd = (k * jnp.exp2(cum_end - cum)).astype(BF16)
                sh_ref[2 * pr + hh] = st * jnp.exp2(cum_end) + _dot_tn(v, kd)
            o_all = jnp.concatenate([o_gla] + o_hg, axis=-1)
            out_cols = slice(pr * 4 * LANE, (pr + 1) * 4 * LANE)
            if reverse:
                o_ref[rows, out_cols] = o_all
            else:
                o_all = o_all + orev_ref[rows, out_cols]
                for hh in range(2):
                    head = slice((2 * pr + hh) * LANE, (2 * pr + hh + 1) * LANE)
                    y = _readout(o_all[:, hh * LANE:(hh + 1) * LANE], gr_ref[rows, head], gn_ref[...])
                    ya_ref[rows, head] = y.astype(ya_ref.dtype)
                    y = _readout(o_all[:, (2 + hh) * LANE:(3 + hh) * LANE], hg_ref[rows, head], hn_ref[...])
                    yb_ref[rows, head] = y.astype(yb_ref.dtype)


def _gated_scans(p, cos_t, sin_t, wg, bg, lb, gla_norm, hg_norm, batch, n_lat_blk):
    assert CHUNK == HEAD_DIM == LANE and GLA_HEADS == HG_HEADS and GLA_HEADS % 2 == 0
    m = p.shape[0]
    pps = SCAN_PAIRS_PER_STEP
    n_groups = GLA_HEADS // 2 // pps
    assert GLA_HEADS % (2 * pps) == 0
    n_blk = n_lat_blk + 1
    col = P_COLS

    def run(reverse, extra_in, extra_specs, out_shape, out_specs):
        def pos(t):
            lat = (n_lat_blk - t) if reverse else (t - 1)
            return jnp.where(t == 0, n_lat_blk, lat)

        def row_blk(b, t):
            return jnp.where(t == 0, batch * n_lat_blk + b, b * n_lat_blk + pos(t))

        def cols(start, pair_width):
            width = pps * pair_width
            assert start % width == 0
            return pl.BlockSpec((ROW_BLK, width), lambda b, h, t: (row_blk(b, t), start // width + h))

        d = 1 if reverse else 0
        tri, mask = _scan_constants(reverse)
        hw = HG_HEADS * HEAD_DIM
        in_specs = [
            cols(col["gla_q"], 2 * GLA_DK), cols(col["gla_k"], 2 * GLA_DK), cols(col["gla_v"], 2 * HEAD_DIM),
            cols(col["hg_q"], 2 * HEAD_DIM), cols(col["hg_f"] + d * hw, 2 * HEAD_DIM),
            cols(col["hg_i"], 2 * HEAD_DIM),
            pl.BlockSpec((ROW_BLK, LANE), lambda b, h, t: (row_blk(b, t), col["a_low"] // LANE)),
            pl.BlockSpec((ROW_BLK, LANE), lambda b, h, t: (pos(t), 0)),
            pl.BlockSpec((ROW_BLK, LANE), lambda b, h, t: (pos(t), 0)),
            pl.BlockSpec((None, pps, LANE, LANE), lambda b, h, t: (d, h, 0, 0)),
            pl.BlockSpec((None, pps, 1, LANE), lambda b, h, t: (d, h, 0, 0)),
            pl.BlockSpec((None, pps, 1, 2 * LANE), lambda b, h, t: (d, h, 0, 0)),
            pl.BlockSpec((CHUNK, CHUNK), lambda b, h, t: (0, 0)),
            pl.BlockSpec((N_LEVELS + 1, 2 * CHUNK, CHUNK), lambda b, h, t: (0, 0, 0)),
        ] + [s(row_blk, cols) for s in extra_specs]
        return pl.pallas_call(
            functools.partial(_scan_kernel, reverse=reverse),
            grid=(batch, n_groups, n_blk),
            in_specs=in_specs,
            out_specs=[s(row_blk, cols) for s in out_specs],
            out_shape=out_shape,
            scratch_shapes=[
                pltpu.VMEM((pps * 3 * ROW_BLK // CHUNK, CHUNK, LANE), F32),
                pltpu.VMEM((pps, 2 * HEAD_DIM, LANE), F32),
                pltpu.VMEM((pps * 2, HEAD_DIM, HEAD_DIM), F32),
            ],
            compiler_params=pltpu.CompilerParams(vmem_limit_bytes=VMEM_LIMIT),
            name="gated_scan_rev" if reverse else "gated_scan_fwd",
        )(p, p, p, p, p, p, p, cos_t, sin_t, wg, bg, lb, tri, mask, *extra_in)

    def rows_spec(pair_width):
        return lambda row_blk, cols: pl.BlockSpec((ROW_BLK, pps * pair_width),
                                                  lambda b, h, t: (row_blk(b, t), h))

    def section_spec(name):
        return lambda row_blk, cols: cols(col[name], 2 * HEAD_DIM)

    def const_spec(row_blk, cols):
        return pl.BlockSpec((1, LANE), lambda b, h, t: (0, 0))

    (o_rev,) = run(True, (), (), [jax.ShapeDtypeStruct((m, (GLA_HEADS + HG_HEADS) * HEAD_DIM), F32)],
                   [rows_spec(4 * LANE)])
    return run(False, (p, p, o_rev, gla_norm.reshape(1, -1), hg_norm.reshape(1, -1)),
               (section_spec("gla_r"), section_spec("hg_g"), rows_spec(4 * LANE), const_spec, const_spec),
               [jax.ShapeDtypeStruct((m, GLA_HEADS * HEAD_DIM), BF16),
                jax.ShapeDtypeStruct((m, HG_HEADS * HEAD_DIM), BF16)],
               [rows_spec(2 * LANE), rows_spec(2 * LANE)])


def _head_norm(x, g):
    return x * lax.rsqrt(jnp.mean(x * x, axis=-1, keepdims=True) + EPS) * g


def _na_span_base(n, n_rows, xp=jnp):
    return xp.clip(n * NA_Q_ROWS - WIN_ROWS // 2, 0, n_rows - NA_SPAN_ROWS)


def _na_variants(n_rows):
    n_blocks = n_rows // NA_Q_ROWS
    assert n_rows % NA_Q_ROWS == 0 and n_blocks >= 4

    def rows(n):
        base = int(_na_span_base(n, n_rows, np))
        out = []
        for i in range(NA_Q_ROWS):
            r = n * NA_Q_ROWS + i
            r0 = min(max(r - WIN_ROWS // 2, 0), n_rows - WIN_ROWS)
            out.append([(base + kk - r + WIN_ROWS - 1, r0 <= base + kk < r0 + WIN_ROWS)
                        for kk in range(NA_SPAN_ROWS)])
        return out

    variants = [rows(0), rows(1), rows(n_blocks - 1)]
    assert all(rows(n) == variants[1] for n in range(1, n_blocks - 1))
    return variants


def _na_kernel(q_ref, kl_ref, vl_ref, kc_ref, vc_ref, qn_ref, kn_ref, tbl_ref, y_ref,
               kls_ref, vls_ref, kcs_ref, vcs_ref, bias_ref, *, n_rows, steps_per_batch, batch):
    t = pl.program_id(1)
    heads = [slice(hd * HEAD_DIM, (hd + 1) * HEAD_DIM) for hd in range(NA_HEADS_PER_STEP)]

    @pl.when((t > 0) & ((t - 1) % steps_per_batch == 0))
    def _():
        for cols in heads:
            kls_ref[:, cols] = _head_norm(kl_ref[:, cols], kn_ref[...]).astype(BF16)
        vls_ref[...] = vl_ref[...].astype(BF16)

    @pl.when(t == 0)
    def _():
        for cols in heads:
            kcs_ref[:, cols] = _head_norm(kc_ref[:, cols], kn_ref[...]).astype(BF16)
        vcs_ref[...] = vc_ref[...].astype(BF16)
        n_dr = tbl_ref.shape[1] - 1
        for hd in range(NA_HEADS_PER_STEP):
            for v, var in enumerate(_na_variants(n_rows)):
                for i, row in enumerate(var):
                    for g in range(NA_SPAN_ROWS // 2):
                        pair = [tbl_ref[hd, dr if ok else n_dr] for dr, ok in row[2 * g:2 * g + 2]]
                        assert all(0 <= dr < n_dr for dr, ok in row[2 * g:2 * g + 2] if ok)
                        bias_ref[hd, v, i * GRID_W:(i + 1) * GRID_W, g * LANE:(g + 1) * LANE] = (
                            jnp.concatenate(pair, axis=1))

    qs = [_head_norm(q_ref[:, cols], qn_ref[...] * (HEAD_DIM ** -0.5 * LOG2E)).astype(BF16) for cols in heads]

    @pl.when(t == 0)
    def _():
        for q, cols in zip(qs, heads):
            for b in range(batch):
                rows = slice(b * ROW_BLK, (b + 1) * ROW_BLK)
                s = _dot_nt(q[rows], kcs_ref[rows, cols])
                e = jnp.exp2(s - jnp.max(s, axis=-1, keepdims=True))
                o = _dot(e.astype(BF16), vcs_ref[rows, cols]) / jnp.sum(e, axis=-1, keepdims=True)
                y_ref[rows, cols] = o.astype(y_ref.dtype)

    @pl.when(t > 0)
    def _():
        ctx_rows = pl.ds(pl.multiple_of(((t - 1) // steps_per_batch) * ROW_BLK, ROW_BLK), ROW_BLK)
        for sub in range(TOK_BLK // ROW_BLK):
            rows = slice(sub * ROW_BLK, (sub + 1) * ROW_BLK)
            n = ((t - 1) % steps_per_batch) * (TOK_BLK // ROW_BLK) + sub
            base = _na_span_base(n, n_rows)
            variant = jnp.where(n == 0, 0, jnp.where(n == n_rows // NA_Q_ROWS - 1, 2, 1))
            span = pl.ds(pl.multiple_of(base * GRID_W, GRID_W), NA_SPAN_ROWS * GRID_W)
            for hd, (q, cols) in enumerate(zip(qs, heads)):
                s_win = _dot_nt(q[rows], kls_ref[span, cols]) + bias_ref[hd, variant]
                s_ctx = _dot_nt(q[rows], kcs_ref[ctx_rows, cols])
                mx = jnp.maximum(jnp.max(s_win, axis=-1, keepdims=True), jnp.max(s_ctx, axis=-1, keepdims=True))
                e_win = jnp.exp2(s_win - mx)
                e_ctx = jnp.exp2(s_ctx - mx)
                den = jnp.sum(e_win, axis=-1, keepdims=True) + jnp.sum(e_ctx, axis=-1, keepdims=True)
                o = (_dot(e_win.astype(BF16), vls_ref[span, cols])
                     + _dot(e_ctx.astype(BF16), vcs_ref[ctx_rows, cols]))
                y_ref[rows, cols] = (o / den).astype(y_ref.dtype)


def _na_bias_table(rpb):
    n_heads, n_dr, n_dc = rpb.shape
    c = np.arange(GRID_W)[:, None]
    kc = np.arange(GRID_W)[None, :]
    c0 = np.clip(c - WIN_COLS // 2, 0, GRID_W - WIN_COLS)
    col_valid = (kc >= c0) & (kc < c0 + WIN_COLS)
    pick_dc = (np.arange(n_dc)[:, None, None] == (kc - c + WIN_COLS - 1)[None]) & col_valid[None]
    col = jnp.einsum("hrb,bcq->hrcq", rpb.astype(F32), jnp.asarray(pick_dc, F32), precision=lax.Precision.HIGHEST)
    col = col + jnp.asarray(np.where(col_valid, 0.0, NEG_BIG), F32)
    return jnp.concatenate([col, jnp.full((n_heads, 1, GRID_W, GRID_W), NEG_BIG, F32)], axis=1) * LOG2E


def _neighbourhood_attention(p, q_norm, k_norm, bias_tbl, layer, batch, n_lat_blk):
    m = p.shape[0]
    lat_rows = n_lat_blk * ROW_BLK
    assert lat_rows % TOK_BLK == 0 and batch * ROW_BLK == TOK_BLK
    steps_per_batch = lat_rows // TOK_BLK
    width = NA_HEADS_PER_STEP * HEAD_DIM
    qb, kb, vb = (P_COLS[name] // width for name in ("na_q", "na_k", "na_v"))
    assert NA_HEADS % NA_HEADS_PER_STEP == 0 and all(P_COLS[name] % width == 0 for name in ("na_q", "na_k", "na_v"))
    ctx_blk = batch * steps_per_batch
    groups = NA_HEADS // NA_HEADS_PER_STEP

    def q_blk(t):
        return jnp.where(t == 0, ctx_blk, t - 1)

    def kv_blk(t):
        return jnp.maximum(t - 1, 0) // steps_per_batch

    kern = functools.partial(_na_kernel, n_rows=lat_rows // GRID_W, steps_per_batch=steps_per_batch, batch=batch)
    return pl.pallas_call(
        kern,
        grid=(groups, 1 + batch * steps_per_batch),
        in_specs=[
            pl.BlockSpec((TOK_BLK, width), lambda h, t: (q_blk(t), qb + h)),
            pl.BlockSpec((lat_rows, width), lambda h, t: (kv_blk(t), kb + h)),
            pl.BlockSpec((lat_rows, width), lambda h, t: (kv_blk(t), vb + h)),
            pl.BlockSpec((TOK_BLK, width), lambda h, t: (ctx_blk, kb + h)),
            pl.BlockSpec((TOK_BLK, width), lambda h, t: (ctx_blk, vb + h)),
            pl.BlockSpec((1, LANE), lambda h, t: (0, 0)),
            pl.BlockSpec((1, LANE), lambda h, t: (0, 0)),
            pl.BlockSpec((NA_HEADS_PER_STEP,) + bias_tbl.shape[1:], lambda h, t: (layer * groups + h, 0, 0, 0)),
        ],
        out_specs=pl.BlockSpec((TOK_BLK, width), lambda h, t: (q_blk(t), h)),
        out_shape=jax.ShapeDtypeStruct((m, NA_HEADS * HEAD_DIM), BF16),
        scratch_shapes=[
            pltpu.VMEM((lat_rows, width), BF16),
            pltpu.VMEM((lat_rows, width), BF16),
            pltpu.VMEM((TOK_BLK, width), BF16),
            pltpu.VMEM((TOK_BLK, width), BF16),
            pltpu.VMEM((NA_HEADS_PER_STEP, 3, ROW_BLK, NA_SPAN_ROWS * GRID_W), F32),
        ],
        compiler_params=pltpu.CompilerParams(vmem_limit_bytes=VMEM_LIMIT),
        name="neighbourhood_attention",
    )(p, p, p, p, p, q_norm.reshape(1, -1), k_norm.reshape(1, -1), bias_tbl)


def _outproj_kernel(*refs, n_lat, n_y):
    xl_ref, xc_ref = refs[:2]
    y_refs = refs[2:2 + n_y]
    mod_ref, g_ref = refs[2 + n_y:4 + n_y]
    w_refs = refs[4 + n_y:4 + 2 * n_y]
    o_ref, h_ref = refs[4 + 2 * n_y:]
    i = pl.program_id(0)
    acc = _dot(y_refs[0][...], w_refs[0][...])
    for y_ref, w_ref in zip(y_refs[1:], w_refs[1:]):
        acc = acc + _dot(y_ref[...], w_ref[...])
    upd = mod_ref[2:3, :] * acc

    def emit(x_ref):
        x1 = x_ref[...] + upd
        o_ref[...] = x1
        h_ref[...] = _norm_mod(x1, g_ref[...], mod_ref[3:4, :], mod_ref[4:5, :]).astype(h_ref.dtype)

    pl.when(i < n_lat)(lambda: emit(xl_ref))
    pl.when(i >= n_lat)(lambda: emit(xc_ref))


def _out_projection(tok, ys, modt, g_mlp, w_out, rows_per_mod, with_ctx):
    d = w_out.shape[1]
    tm = TOK_BLK
    per = rows_per_mod // tm
    n_blk = tok.n_lat + (1 if with_ctx else 0)
    y_specs, w_specs, row0 = [], [], 0
    for y in ys:
        width = y.shape[1]
        assert row0 % width == 0
        y_specs.append(pl.BlockSpec((tm, width), lambda i: (i, 0)))
        w_specs.append(pl.BlockSpec((width, d), lambda i, blk=row0 // width: (blk, 0)))
        row0 += width
    assert row0 == w_out.shape[0]
    return pl.pallas_call(
        functools.partial(_outproj_kernel, n_lat=tok.n_lat, n_y=len(ys)),
        grid=(n_blk,),
        in_specs=tok.specs(d) + y_specs + [
            pl.BlockSpec((None, 6, d), lambda i: (i // per, 0, 0)),
            pl.BlockSpec((1, d), lambda i: (0, 0)),
        ] + w_specs,
        out_specs=[pl.BlockSpec((tm, d), lambda i: (i, 0)), pl.BlockSpec((tm, d), lambda i: (i, 0))],
        out_shape=[jax.ShapeDtypeStruct((n_blk * tm, d), F32), jax.ShapeDtypeStruct((n_blk * tm, d), BF16)],
        compiler_params=pltpu.CompilerParams(vmem_limit_bytes=VMEM_LIMIT),
        name="out_projection",
    )(tok.lat, tok.ctx, *ys, modt, g_mlp.reshape(1, d), *([w_out] * len(ys)))


def _mlp_kernel(*refs, emit_next):
    if emit_next:
        h_ref, x_ref, mod_ref, w1_ref, w2_ref, gn_ref, modn_ref, o_ref, hn_ref, acc_ref = refs
    else:
        h_ref, x_ref, mod_ref, w1_ref, w2_ref, o_ref, acc_ref = refs
    j = pl.program_id(1)

    @pl.when(j == 0)
    def _():
        acc_ref[...] = jnp.zeros_like(acc_ref)

    a = jnp.maximum(_dot(h_ref[...], w1_ref[...]), 0.0)
    acc_ref[...] += _dot((a * a).astype(BF16), w2_ref[...])

    @pl.when(j == pl.num_programs(1) - 1)
    def _():
        x2 = x_ref[...] + mod_ref[5:6, :] * acc_ref[...]
        o_ref[...] = x2
        if emit_next:
            hn_ref[...] = _norm_mod(x2, gn_ref[...], modn_ref[0:1, :], modn_ref[1:2, :]).astype(hn_ref.dtype)


def _mlp(h, x1, modt, w1, w2, rows_per_mod, next_norm):
    m, d = x1.shape
    n_j, _, tf = w1.shape
    tm = TOK_BLK
    per = rows_per_mod // tm
    emit_next = next_norm is not None

    def rows(i, j):
        return (i, 0)

    def mod_rows(i, j):
        return (i // per, 0, 0)

    in_specs = [
        pl.BlockSpec((tm, d), rows),
        pl.BlockSpec((tm, d), rows),
        pl.BlockSpec((None, 6, d), mod_rows),
        pl.BlockSpec((None, d, tf), lambda i, j: (j, 0, 0)),
        pl.BlockSpec((tf, d), lambda i, j: (j, 0)),
    ]
    args = [h, x1, modt, w1, w2]
    out_specs = [pl.BlockSpec((tm, d), rows)]
    out_shape = [jax.ShapeDtypeStruct((m, d), F32)]
    if emit_next:
        g_next, modt_next = next_norm
        in_specs += [pl.BlockSpec((1, d), lambda i, j: (0, 0)), pl.BlockSpec((None, 6, d), mod_rows)]
        args += [g_next.reshape(1, d), modt_next]
        out_specs.append(pl.BlockSpec((tm, d), rows))
        out_shape.append(jax.ShapeDtypeStruct((m, d), BF16))
    return pl.pallas_call(
        functools.partial(_mlp_kernel, emit_next=emit_next),
        grid=(m // tm, n_j),
        in_specs=in_specs,
        out_specs=out_specs,
        out_shape=out_shape,
        scratch_shapes=[pltpu.VMEM((tm, d), F32)],
        compiler_params=pltpu.CompilerParams(vmem_limit_bytes=VMEM_LIMIT_BIG),
        name="mlp",
    )(*args)


def _relayout_kernel(wt_ref, o_ref):
    al, aw = P_SRC["a_low"]
    x = wt_ref[...]
    pad = jnp.zeros((LANE - aw, x.shape[1]), x.dtype)
    cols = jnp.concatenate([x[:al], x[al + aw:], x[al:al + aw], pad], axis=0)
    o_ref[...] = cols.T.astype(o_ref.dtype)


def _permute_w_in(w_in):
    depth, d, n = w_in.shape
    tc = 256
    return pl.pallas_call(
        _relayout_kernel,
        grid=(depth, d // tc),
        in_specs=[pl.BlockSpec((None, n, tc), lambda l, i: (l, 0, i))],
        out_specs=pl.BlockSpec((None, tc, P_WIDTH), lambda l, i: (l, i, 0)),
        out_shape=jax.ShapeDtypeStruct((depth, d, P_WIDTH), BF16),
        compiler_params=pltpu.CompilerParams(vmem_limit_bytes=VMEM_LIMIT_BIG),
        name="w_in_relayout",
    )(jnp.swapaxes(w_in, 1, 2))


def _gla_gate_weights(w_a2, b_a):
    n_pairs = GLA_HEADS // 2
    wg = jnp.zeros((2, n_pairs, LANE, 2 * GLA_DK), F32)
    for dd in range(2):
        blk = w_a2[dd].reshape(GLA_RANK, n_pairs, 2 * GLA_DK).transpose(1, 0, 2)
        wg = wg.at[dd, :, dd * GLA_RANK:(dd + 1) * GLA_RANK, :].set(blk)
    bg = b_a.reshape(2, n_pairs, 1, 2 * GLA_DK)
    return wg.astype(BF16), bg.astype(F32)


def _rope_tables(seq, n_ctx):
    quarter = GLA_DK // 4
    inv_freq = ROPE_BASE ** (-np.arange(quarter, dtype=np.float64) / quarter)
    pos = np.arange(seq)
    lane = np.arange(LANE)
    is_col = (lane % GLA_DK) // (GLA_DK // 2) == 1
    first = (lane % (GLA_DK // 2)) < quarter
    p = np.where(is_col[None, :], (pos % GRID_W)[:, None], (pos // GRID_W)[:, None]).astype(np.float64)
    ang = p * inv_freq[lane % quarter][None, :]
    cos = np.cos(ang)
    sin = np.where(first[None, :], -np.sin(ang), np.sin(ang))
    cos = np.concatenate([cos, np.ones((n_ctx, LANE))], axis=0)
    sin = np.concatenate([sin, np.zeros((n_ctx, LANE))], axis=0)
    return jnp.asarray(cos, F32), jnp.asarray(sin, F32)


def kernel(x, c, ctx, c_ctx, w_mod, b_mod, attn_norm, w_in, gla_w_a2, gla_b_a, gla_norm, hg_lower_bounds, hg_norm, na_q_norm, na_k_norm, na_rpb, w_out, mlp_norm, w_mlp1, w_mlp2):
    batch, seq, d = x.shape
    n_ctx = ctx.shape[1]
    depth = w_mod.shape[0]
    assert seq % ROW_BLK == 0 and n_ctx == ROW_BLK and batch * n_ctx == TOK_BLK and d % LANE == 0
    assert seq % TOK_BLK == 0 and seq % GRID_W == 0
    n_lat_blk = seq // ROW_BLK
    lat_rows = batch * seq
    n_lat_tok = lat_rows // TOK_BLK

    tok = _Tokens(x.reshape(lat_rows, d), ctx.reshape(batch * n_ctx, d), n_lat_tok, 0)

    cc = jnp.zeros((8, d), F32).at[0:batch].set(c).at[batch].set(c_ctx)
    mod = _modulation(cc, w_mod, b_mod)
    modt = mod[:, 0:batch + 1].reshape(depth, batch + 1, 6, d)

    lb_p = jax.nn.softmax(hg_lower_bounds.astype(F32), axis=0)
    lower = jnp.cumsum(lb_p, axis=0) - lb_p[0]
    lower = lower.reshape(depth, 2, HG_HEADS // 2, 1, 2 * HEAD_DIM)
    cos_t, sin_t = _rope_tables(seq, n_ctx)

    bias_tbl = _na_bias_table(na_rpb.reshape((depth * NA_HEADS,) + na_rpb.shape[2:]))
    w_in_b = _permute_w_in(w_in)

    h = _first_norm(tok, modt[0], attn_norm[0], seq)
    for l in range(depth):
        last = l == depth - 1
        p, w1_b, w2_b, w_out_b = _in_projection(h, w_in_b, w_mlp1, w_mlp2, w_out, l)
        wg, bg = _gla_gate_weights(gla_w_a2[l], gla_b_a[l])
        ya, yb = _gated_scans(p, cos_t, sin_t, wg, bg, lower[l], gla_norm[l], hg_norm[l], batch, n_lat_blk)
        yc = _neighbourhood_attention(p, na_q_norm[l], na_k_norm[l], bias_tbl, l, batch, n_lat_blk)
        x1, h2 = _out_projection(tok, (ya, yb, yc), modt[l], mlp_norm[l], w_out_b, seq, with_ctx=not last)
        if last:
            (xu,) = _mlp(h2, x1, modt[l], w1_b, w2_b, seq, None)
        else:
            xu, h = _mlp(h2, x1, modt[l], w1_b, w2_b, seq, (attn_norm[l + 1], modt[l + 1]))
        tok = _Tokens(xu, xu, n_lat_tok, n_lat_tok)
    return xu.reshape(batch, seq, d)
```

```python
import functools

import numpy as np
import jax
import jax.numpy as jnp
from jax import lax
from jax.experimental import pallas as pl
from jax.experimental.pallas import tpu as pltpu

F32 = jnp.float32
BF16 = jnp.bfloat16

EPS = 1e-6
GRID_W = 64
HEAD_DIM = 128
GLA_HEADS = 4
GLA_DK = 64
GLA_RANK = 16
GLA_TAU = 16.0
HG_HEADS = 4
NA_HEADS = 8
WIN_ROWS = 8
WIN_COLS = 16
ROPE_BASE = 10000.0
NEG_BIG = -1e30
LOG2E = 1.4426950408889634

LANE = 128
ROW_BLK = 256
CHUNK = 128
N_LEVELS = 7
TOK_BLK = 512
MLP_TF = 1024
VMEM_LIMIT = 48 * 1024 * 1024
VMEM_LIMIT_BIG = 56 * 1024 * 1024

NA_Q_ROWS = ROW_BLK // GRID_W
NA_SPAN_ROWS = 12
NA_HEADS_PER_STEP = 2
SCAN_PAIRS_PER_STEP = 2


def _projection_layout():
    gw, hw, nw = GLA_HEADS * HEAD_DIM, HG_HEADS * HEAD_DIM, NA_HEADS * HEAD_DIM
    sections = (("gla_q", GLA_HEADS * GLA_DK), ("gla_k", GLA_HEADS * GLA_DK), ("gla_v", gw), ("gla_r", gw),
                ("a_low", 2 * GLA_RANK), ("hg_q", hw), ("hg_f", 2 * hw), ("hg_i", hw), ("hg_g", hw),
                ("na_q", nw), ("na_k", nw), ("na_v", nw))
    src, off = {}, 0
    for name, width in sections:
        src[name] = (off, width)
        off += width
    dst, off = {}, 0
    for name, width in sections:
        if name != "a_low":
            dst[name] = off
            off += width
    dst["a_low"] = off
    return src, dst, off + LANE


P_SRC, P_COLS, P_WIDTH = _projection_layout()


def _dot(a, b):
    return jnp.dot(a, b, preferred_element_type=F32)


def _dot_nt(a, b):
    return lax.dot_general(a, b, (((1,), (1,)), ((), ())), preferred_element_type=F32)


def _dot_tn(a, b):
    return lax.dot_general(a, b, (((0,), (0,)), ((), ())), preferred_element_type=F32)


def _sigmoid(x):
    return 1.0 / (1.0 + jnp.exp(-x))


def _silu(x):
    return x * _sigmoid(x)


def _split3(x):
    hi = x.astype(BF16)
    r1 = x - hi.astype(F32)
    mid = r1.astype(BF16)
    lo = (r1 - mid.astype(F32)).astype(BF16)
    return jnp.concatenate([hi, mid, lo], axis=-1)


def _sum3(y, w):
    return y[:, 0:w] + y[:, w:2 * w] + y[:, 2 * w:3 * w]


def _mod_kernel(c_ref, w_ref, b_ref, o_ref):
    s = _silu(c_ref[...]).astype(BF16)
    o_ref[...] = _dot(s, w_ref[...].astype(BF16)) + b_ref[...]


def _modulation(cc, w_mod, b_mod):
    depth, d, n = w_mod.shape
    tn = 1024
    return pl.pallas_call(
        _mod_kernel,
        grid=(depth, n // tn),
        in_specs=[
            pl.BlockSpec((8, d), lambda l, j: (0, 0)),
            pl.BlockSpec((None, d, tn), lambda l, j: (l, 0, j)),
            pl.BlockSpec((None, 1, tn), lambda l, j: (l, 0, j)),
        ],
        out_specs=pl.BlockSpec((None, 8, tn), lambda l, j: (l, 0, j)),
        out_shape=jax.ShapeDtypeStruct((depth, 8, n), F32),
        compiler_params=pltpu.CompilerParams(vmem_limit_bytes=VMEM_LIMIT),
        name="modulation",
    )(cc, w_mod, b_mod.reshape(depth, 1, n))


def _norm_mod(x, g, shift, scale):
    y = x * lax.rsqrt(jnp.mean(x * x, axis=-1, keepdims=True) + EPS) * g
    return y * (1.0 + scale) + shift


class _Tokens:
    def __init__(self, lat, ctx, n_lat, ctx_blk):
        self.lat, self.ctx, self.n_lat, self.ctx_blk = lat, ctx, n_lat, ctx_blk

    def specs(self, d):
        n_lat, ctx_blk = self.n_lat, self.ctx_blk
        return [pl.BlockSpec((TOK_BLK, d), lambda i, *_: (jnp.minimum(i, n_lat - 1), 0)),
                pl.BlockSpec((TOK_BLK, d), lambda i, *_: (ctx_blk, 0))]


def _first_norm_kernel(xl_ref, xc_ref, mod_ref, g_ref, h_ref, *, n_lat):
    i = pl.program_id(0)

    def emit(x_ref):
        h_ref[...] = _norm_mod(x_ref[...], g_ref[...], mod_ref[0:1, :], mod_ref[1:2, :]).astype(h_ref.dtype)

    pl.when(i < n_lat)(lambda: emit(xl_ref))
    pl.when(i >= n_lat)(lambda: emit(xc_ref))


def _first_norm(tok, modt, g, rows_per_mod):
    d = tok.lat.shape[1]
    per = rows_per_mod // TOK_BLK
    n_blk = tok.n_lat + 1
    return pl.pallas_call(
        functools.partial(_first_norm_kernel, n_lat=tok.n_lat),
        grid=(n_blk,),
        in_specs=tok.specs(d) + [
            pl.BlockSpec((None, 6, d), lambda i: (i // per, 0, 0)),
            pl.BlockSpec((1, d), lambda i: (0, 0)),
        ],
        out_specs=pl.BlockSpec((TOK_BLK, d), lambda i: (i, 0)),
        out_shape=jax.ShapeDtypeStruct((n_blk * TOK_BLK, d), BF16),
        compiler_params=pltpu.CompilerParams(vmem_limit_bytes=VMEM_LIMIT),
        name="first_norm",
    )(tok.lat, tok.ctx, modt, g.reshape(1, d))


def _inproj_kernel(h_ref, w_ref, w1_ref, w2_ref, wo_ref, o_ref, w1o_ref, w2o_ref, woo_ref, *, n_cast):
    o_ref[...] = _dot(h_ref[...], w_ref[...])

    @pl.when(pl.program_id(0) * pl.num_programs(1) + pl.program_id(1) < n_cast)
    def _():
        w1o_ref[...] = w1_ref[...].astype(w1o_ref.dtype)
        w2o_ref[...] = w2_ref[...].astype(w2o_ref.dtype)
        woo_ref[...] = wo_ref[...].astype(woo_ref.dtype)


def _in_projection(h, w, w1, w2, w_out, layer):
    m, d = h.shape
    n = w.shape[2]
    ff = w1.shape[2]
    d_out = w_out.shape[2]
    tn = n // 3
    n_i = m // TOK_BLK
    tc = 256
    n_cast = ff // tc
    to = w_out.shape[1] // n_cast
    per_slab = MLP_TF // tc
    assert n_cast <= 3 * n_i and w_out.shape[1] % n_cast == 0 and to % 16 == 0 and MLP_TF % tc == 0

    def cast_blk(j, i):
        return jnp.minimum(j * n_i + i, n_cast - 1)

    return pl.pallas_call(
        functools.partial(_inproj_kernel, n_cast=n_cast),
        grid=(n // tn, n_i),
        in_specs=[
            pl.BlockSpec((TOK_BLK, d), lambda j, i: (i, 0)),
            pl.BlockSpec((None, d, tn), lambda j, i: (layer, 0, j)),
            pl.BlockSpec((None, d, tc), lambda j, i: (layer, 0, cast_blk(j, i))),
            pl.BlockSpec((None, tc, d), lambda j, i: (layer, cast_blk(j, i), 0)),
            pl.BlockSpec((None, to, d_out), lambda j, i: (layer, cast_blk(j, i), 0)),
        ],
        out_specs=[
            pl.BlockSpec((TOK_BLK, tn), lambda j, i: (i, j)),
            pl.BlockSpec((None, d, tc), lambda j, i: (cast_blk(j, i) // per_slab, 0, cast_blk(j, i) % per_slab)),
            pl.BlockSpec((tc, d), lambda j, i: (cast_blk(j, i), 0)),
            pl.BlockSpec((to, d_out), lambda j, i: (cast_blk(j, i), 0)),
        ],
        out_shape=[
            jax.ShapeDtypeStruct((m, n), F32),
            jax.ShapeDtypeStruct((ff // MLP_TF, d, MLP_TF), BF16),
            jax.ShapeDtypeStruct((ff, d), BF16),
            jax.ShapeDtypeStruct(w_out.shape[1:], BF16),
        ],
        compiler_params=pltpu.CompilerParams(vmem_limit_bytes=VMEM_LIMIT_BIG),
        name="in_projection",
    )(h, w, w1, w2, w_out)


def _scan_constants(reverse):
    c = CHUNK
    i = np.arange(c)[:, None]
    j = np.arange(c)[None, :]
    tri = (j >= i) if reverse else (j <= i)
    mask = np.zeros((N_LEVELS + 1, c, c), np.float32)
    for l in range(N_LEVELS):
        level = ((i ^ j) >> l) == 1
        mask[l] = level & ((i < j) if reverse else (i > j))
    mask[N_LEVELS] = np.eye(c, dtype=np.float32)
    return jnp.asarray(tri, BF16), jnp.asarray(np.concatenate([mask, mask], axis=1), BF16)


def _level_exponents(g, cum, scr_ref, reverse):
    c = CHUNK
    scr_ref[...] = cum
    row = lax.broadcasted_iota(jnp.int32, (c, LANE), 0)
    up = pltpu.roll(g, c - 1, 0)
    dn = pltpu.roll(g, 1, 0)
    m2 = row & 1
    m4 = row & 3
    if reverse:
        e0 = jnp.where(m2 == 0, g, 0.0)
        e1 = jnp.where(m4 == 0, g + up, jnp.where(m4 == 1, g, jnp.where(m4 == 2, 0.0, dn)))
    else:
        e0 = jnp.where(m2 == 1, g, 0.0)
        e1 = jnp.where(m4 == 0, up, jnp.where(m4 == 1, 0.0, jnp.where(m4 == 2, g, g + dn)))
    out = [e0, e1]
    for l in range(2, N_LEVELS):
        s = 1 << l
        pieces = []
        for blk in range(c // (2 * s)):
            lo = blk * 2 * s
            ref = scr_ref[lo + (s if reverse else s - 1):lo + (s if reverse else s - 1) + 1, :]
            if s < 8:
                pieces.append(-jnp.abs(cum[lo:lo + 2 * s] - ref))
            else:
                first, second = cum[lo:lo + s], cum[lo + s:lo + 2 * s]
                pieces += [first - ref, ref - second] if reverse else [ref - first, second - ref]
        out.append(pieces[0] if len(pieces) == 1 else jnp.concatenate(pieces, axis=0))
    return out


def _pair_weights(qs, k, exps, mask_ref):
    n = len(qs) * CHUNK

    def lhs(w):
        parts = [q if w is None else q * w for q in qs]
        return parts[0] if len(parts) == 1 else jnp.concatenate(parts, axis=0)

    att = mask_ref[N_LEVELS, 0:n, :] * _dot_nt(lhs(None), k).astype(BF16)
    for l, e in enumerate(exps):
        w = jnp.exp2(e).astype(BF16)
        att = att + mask_ref[l, 0:n, :] * _dot_nt(lhs(w), k * w).astype(BF16)
    return att


def _readout(o, gate, g):
    y = o * lax.rsqrt(jnp.mean(o * o, axis=-1, keepdims=True) + EPS) * g
    return y * _silu(gate)


class _Cols:
    def __init__(self, ref, start):
        self.ref, self.start = ref, start

    def __getitem__(self, idx):
        rows, cols = idx
        return self.ref[rows, self.start + cols.start:self.start + cols.stop]


def _scan_kernel(*refs, reverse):
    p_ref, al_ref, cs_ref, wg_ref, bg_ref, lb_ref, tri_ref, mask_ref = refs[:8]
    if reverse:
        o_ref, scr_ref, sg_ref, sh_ref = refs[8:]
    else:
        orev_ref, gn_ref, hn_ref, ya_ref, yb_ref, scr_ref, sg_ref, sh_ref = refs[8:]
    c = CHUNK
    gq_ref, gk_ref, gv_ref, gr_ref, hq_ref, hi_ref, hg_ref = (
        _Cols(p_ref, P_COLS[name]) for name in ("gla_q", "gla_k", "gla_v", "gla_r", "hg_q", "hg_i", "hg_g"))
    hf_ref = _Cols(p_ref, P_COLS["hg_f"] + (HG_HEADS * HEAD_DIM if reverse else 0))
    cos_ref, sin_ref = _Cols(cs_ref, 0), _Cols(cs_ref, LANE)

    @pl.when(pl.program_id(1) == 0)
    def _():
        sg_ref[...] = jnp.zeros_like(sg_ref)
        sh_ref[...] = jnp.zeros_like(sh_ref)

    lane = lax.broadcasted_iota(jnp.int32, (c, LANE), 1)
    first_half = (lane % (GLA_DK // 2)) < GLA_DK // 4
    head_a = lane < GLA_DK
    lane_masks = (head_a, jnp.logical_not(head_a))
    state_mask = jnp.concatenate([head_a, jnp.logical_not(head_a)], axis=0)
    end_row = 0 if reverse else c - 1
    n_chunks = ROW_BLK // c
    offsets = [(n_chunks - 1 - ci) * c if reverse else ci * c for ci in range(n_chunks)]

    def rope(x, rows):
        swapped = jnp.where(first_half, pltpu.roll(x, LANE - GLA_DK // 4, 1), pltpu.roll(x, GLA_DK // 4, 1))
        return x * cos_ref[rows, 0:LANE] + swapped * sin_ref[rows, 0:LANE]

    chains = []
    for pr in range(SCAN_PAIRS_PER_STEP):
        gla = slice(pr * LANE, (pr + 1) * LANE)
        for off in offsets:
            rows = slice(off, off + c)
            q = rope(gq_ref[rows, gla], rows) * (GLA_DK ** -0.5)
            k = rope(gk_ref[rows, gla], rows)
            logit = _dot(al_ref[rows, :].astype(BF16), wg_ref[pr]) + bg_ref[pr]
            soft = jnp.log2(1.0 + jnp.exp2(jnp.abs(logit) * -LOG2E))
            g = (jnp.minimum(logit, 0.0) * LOG2E - soft) * (1.0 / GLA_TAU)
            chains.append((q, k, g))
            for hh in range(2):
                head = slice((2 * pr + hh) * LANE, (2 * pr + hh + 1) * LANE)
                lb = lb_ref[pr][:, hh * LANE:(hh + 1) * LANE]
                logit = hf_ref[rows, head]
                e = jnp.exp2(jnp.abs(logit) * -LOG2E)
                r = 1.0 / (1.0 + e)
                sig = jnp.where(logit >= 0, r, e * r)
                sig_neg = jnp.where(logit >= 0, e * r, r)
                g = jnp.log2(lb + (1.0 - lb) * sig)
                k = (1.0 - lb) * sig_neg
                q = _silu(hq_ref[rows, head])
                chains.append((q, k, g))

    g3 = jnp.concatenate([_split3(g) for (_, _, g) in chains], axis=-1)
    cum_all = _dot(tri_ref[...], g3)
    cums = [_sum3(cum_all[:, 3 * LANE * n:3 * LANE * (n + 1)], LANE) for n in range(len(chains))]

    atts = []
    for n, ((q, k, g), cum) in enumerate(zip(chains, cums)):
        exps = _level_exponents(g, cum, scr_ref.at[n], reverse)
        qs = [jnp.where(m, q, 0.0).astype(BF16) for m in lane_masks] if n % 3 == 0 else [q.astype(BF16)]
        atts.append(_pair_weights(qs, k.astype(BF16), exps, mask_ref))

    for pr in range(SCAN_PAIRS_PER_STEP):
        for ci, off in enumerate(offsets):
            rows = slice(off, off + c)
            n0 = (pr * n_chunks + ci) * 3
            (q, k, _), cum, att = chains[n0], cums[n0], atts[n0]
            v = gv_ref[rows, pr * 2 * LANE:(pr + 1) * 2 * LANE].astype(BF16)
            cum_end = cum[end_row:end_row + 1, :]
            st = sg_ref[pr]
            inter = _dot_nt((q * jnp.exp2(cum)).astype(BF16), st.astype(BF16))
            o_gla = jnp.concatenate([_dot(att[0:c], v[:, 0:LANE]), _dot(att[c:2 * c], v[:, LANE:2 * LANE])],
                                    axis=-1) + inter
            kd = (k * jnp.exp2(cum_end - cum)).astype(BF16)
            sg_ref[pr] = st * jnp.exp2(cum_end) + jnp.where(state_mask, _dot_tn(v, kd), 0.0)
            o_hg = []
            for hh in range(2):
                head = slice((2 * pr + hh) * LANE, (2 * pr + hh + 1) * LANE)
                (q, k, _), cum, att = chains[n0 + 1 + hh], cums[n0 + 1 + hh], atts[n0 + 1 + hh]
                v = hi_ref[rows, head].astype(BF16)
                cum_end = cum[end_row:end_row + 1, :]
                st = sh_ref[2 * pr + hh]
                o_hg.append(_dot(att, v) + _dot_nt((q * jnp.exp2(cum)).astype(BF16), st.astype(BF16)))
                kd = (k * jnp.exp2(cum_end - cum)).astype(BF16)
                sh_ref[2 * pr + hh] = st * jnp.exp2(cum_end) + _dot_tn(v, kd)
            o_all = jnp.concatenate([o_gla] + o_hg, axis=-1)
            out_cols = slice(pr * 4 * LANE, (pr + 1) * 4 * LANE)
            if reverse:
                o_ref[rows, out_cols] = o_all
            else:
                o_all = o_all + orev_ref[rows, out_cols]
                for hh in range(2):
                    head = slice((2 * pr + hh) * LANE, (2 * pr + hh + 1) * LANE)
                    y = _readout(o_all[:, hh * LANE:(hh + 1) * LANE], gr_ref[rows, head], gn_ref[...])
                    ya_ref[rows, head] = y.astype(ya_ref.dtype)
                    y = _readout(o_all[:, (2 + hh) * LANE:(3 + hh) * LANE], hg_ref[rows, head], hn_ref[...])
                    yb_ref[rows, head] = y.astype(yb_ref.dtype)


def _gated_scans(p, rope_t, wg, bg, lb, gla_norm, hg_norm, batch, n_lat_blk):
    assert CHUNK == HEAD_DIM == LANE and GLA_HEADS == HG_HEADS
    assert GLA_HEADS == 2 * SCAN_PAIRS_PER_STEP
    m = p.shape[0]
    pps = SCAN_PAIRS_PER_STEP
    n_blk = n_lat_blk + 1
    scan_cols = P_COLS["na_q"]
    assert all(P_COLS[s] < scan_cols for s in ("gla_q", "gla_k", "gla_v", "gla_r", "hg_q", "hg_f", "hg_i", "hg_g"))

    def run(reverse, extra_in, extra_specs, out_shape, out_specs):
        def pos(t):
            lat = (n_lat_blk - t) if reverse else (t - 1)
            return jnp.where(t == 0, n_lat_blk, lat)

        def row_blk(b, t):
            return jnp.where(t == 0, batch * n_lat_blk + b, b * n_lat_blk + pos(t))

        def rows(width, col_blk=0):
            return pl.BlockSpec((ROW_BLK, width), lambda b, t: (row_blk(b, t), col_blk))

        def const(shape):
            return pl.BlockSpec(shape, lambda b, t: (0,) * len(shape))

        d = 1 if reverse else 0
        tri, mask = _scan_constants(reverse)
        in_specs = [
            rows(scan_cols),
            rows(LANE, P_COLS["a_low"] // LANE),
            pl.BlockSpec((ROW_BLK, 2 * LANE), lambda b, t: (pos(t), 0)),
            pl.BlockSpec((None, pps, LANE, LANE), lambda b, t: (d, 0, 0, 0)),
            pl.BlockSpec((None, pps, 1, LANE), lambda b, t: (d, 0, 0, 0)),
            pl.BlockSpec((None, pps, 1, 2 * LANE), lambda b, t: (d, 0, 0, 0)),
            const((CHUNK, CHUNK)),
            const((N_LEVELS + 1, 2 * CHUNK, CHUNK)),
        ] + [s(rows, const) for s in extra_specs]
        return pl.pallas_call(
            functools.partial(_scan_kernel, reverse=reverse),
            grid=(batch, n_blk),
            in_specs=in_specs,
            out_specs=[s(rows, const) for s in out_specs],
            out_shape=out_shape,
            scratch_shapes=[
                pltpu.VMEM((pps * 3 * ROW_BLK // CHUNK, CHUNK, LANE), F32),
                pltpu.VMEM((pps, 2 * HEAD_DIM, LANE), F32),
                pltpu.VMEM((pps * 2, HEAD_DIM, HEAD_DIM), F32),
            ],
            compiler_params=pltpu.CompilerParams(vmem_limit_bytes=VMEM_LIMIT),
            name="gated_scan_rev" if reverse else "gated_scan_fwd",
        )(p, p, rope_t, wg, bg, lb, tri, mask, *extra_in)

    def rows_spec(width):
        return lambda rows, const: rows(width)

    def const_spec(rows, const):
        return const((1, LANE))

    n_heads = GLA_HEADS + HG_HEADS
    (o_rev,) = run(True, (), (), [jax.ShapeDtypeStruct((m, n_heads * HEAD_DIM), F32)],
                   [rows_spec(n_heads * HEAD_DIM)])
    return run(False, (o_rev, gla_norm.reshape(1, -1), hg_norm.reshape(1, -1)),
               (rows_spec(n_heads * HEAD_DIM), const_spec, const_spec),
               [jax.ShapeDtypeStruct((m, GLA_HEADS * HEAD_DIM), BF16),
                jax.ShapeDtypeStruct((m, HG_HEADS * HEAD_DIM), BF16)],
               [rows_spec(GLA_HEADS * HEAD_DIM), rows_spec(HG_HEADS * HEAD_DIM)])


def _head_norm(x, g):
    return x * lax.rsqrt(jnp.mean(x * x, axis=-1, keepdims=True) + EPS) * g


def _na_span_base(n, n_rows, xp=jnp):
    return xp.clip(n * NA_Q_ROWS - WIN_ROWS // 2, 0, n_rows - NA_SPAN_ROWS)


def _na_variants(n_rows):
    n_blocks = n_rows // NA_Q_ROWS
    assert n_rows % NA_Q_ROWS == 0 and n_blocks >= 4

    def rows(n):
        base = int(_na_span_base(n, n_rows, np))
        out = []
        for i in range(NA_Q_ROWS):
            r = n * NA_Q_ROWS + i
            r0 = min(max(r - WIN_ROWS // 2, 0), n_rows - WIN_ROWS)
            out.append([(base + kk - r + WIN_ROWS - 1, r0 <= base + kk < r0 + WIN_ROWS)
                        for kk in range(NA_SPAN_ROWS)])
        return out

    variants = [rows(0), rows(1), rows(n_blocks - 1)]
    assert all(rows(n) == variants[1] for n in range(1, n_blocks - 1))
    return variants


def _na_kernel(q_ref, kl_ref, vl_ref, kc_ref, vc_ref, qn_ref, kn_ref, tbl_ref, y_ref,
               kls_ref, vls_ref, kcs_ref, vcs_ref, bias_ref, *, n_rows, steps_per_batch, batch):
    t = pl.program_id(1)
    heads = [slice(hd * HEAD_DIM, (hd + 1) * HEAD_DIM) for hd in range(NA_HEADS_PER_STEP)]

    @pl.when((t > 0) & ((t - 1) % steps_per_batch == 0))
    def _():
        for cols in heads:
            kls_ref[:, cols] = _head_norm(kl_ref[:, cols], kn_ref[...]).astype(BF16)
        vls_ref[...] = vl_ref[...].astype(BF16)

    @pl.when(t == 0)
    def _():
        for cols in heads:
            kcs_ref[:, cols] = _head_norm(kc_ref[:, cols], kn_ref[...]).astype(BF16)
        vcs_ref[...] = vc_ref[...].astype(BF16)
        n_dr = tbl_ref.shape[1] - 1
        for hd in range(NA_HEADS_PER_STEP):
            for v, var in enumerate(_na_variants(n_rows)):
                for i, row in enumerate(var):
                    for g in range(NA_SPAN_ROWS // 2):
                        pair = [tbl_ref[hd, dr if ok else n_dr] for dr, ok in row[2 * g:2 * g + 2]]
                        assert all(0 <= dr < n_dr for dr, ok in row[2 * g:2 * g + 2] if ok)
                        bias_ref[hd, v, i * GRID_W:(i + 1) * GRID_W, g * LANE:(g + 1) * LANE] = (
                            jnp.concatenate(pair, axis=1))

    qs = [_head_norm(q_ref[:, cols], qn_ref[...] * (HEAD_DIM ** -0.5 * LOG2E)).astype(BF16) for cols in heads]

    @pl.when(t == 0)
    def _():
        for q, cols in zip(qs, heads):
            for b in range(batch):
                rows = slice(b * ROW_BLK, (b + 1) * ROW_BLK)
                s = _dot_nt(q[rows], kcs_ref[rows, cols])
                e = jnp.exp2(s - jnp.max(s, axis=-1, keepdims=True))
                o = _dot(e.astype(BF16), vcs_ref[rows, cols]) / jnp.sum(e, axis=-1, keepdims=True)
                y_ref[rows, cols] = o.astype(y_ref.dtype)

    @pl.when(t > 0)
    def _():
        ctx_rows = pl.ds(pl.multiple_of(((t - 1) // steps_per_batch) * ROW_BLK, ROW_BLK), ROW_BLK)
        for sub in range(TOK_BLK // ROW_BLK):
            rows = slice(sub * ROW_BLK, (sub + 1) * ROW_BLK)
            n = ((t - 1) % steps_per_batch) * (TOK_BLK // ROW_BLK) + sub
            base = _na_span_base(n, n_rows)
            variant = jnp.where(n == 0, 0, jnp.where(n == n_rows // NA_Q_ROWS - 1, 2, 1))
            span = pl.ds(pl.multiple_of(base * GRID_W, GRID_W), NA_SPAN_ROWS * GRID_W)
            for hd, (q, cols) in enumerate(zip(qs, heads)):
                s_win = _dot_nt(q[rows], kls_ref[span, cols]) + bias_ref[hd, variant]
                s_ctx = _dot_nt(q[rows], kcs_ref[ctx_rows, cols])
                mx = jnp.maximum(jnp.max(s_win, axis=-1, keepdims=True), jnp.max(s_ctx, axis=-1, keepdims=True))
                e_win = jnp.exp2(s_win - mx)
                e_ctx = jnp.exp2(s_ctx - mx)
                den = jnp.sum(e_win, axis=-1, keepdims=True) + jnp.sum(e_ctx, axis=-1, keepdims=True)
                o = (_dot(e_win.astype(BF16), vls_ref[span, cols])
                     + _dot(e_ctx.astype(BF16), vcs_ref[ctx_rows, cols]))
                y_ref[rows, cols] = (o / den).astype(y_ref.dtype)


def _na_bias_table(rpb):
    n_heads, n_dr, n_dc = rpb.shape
    c = np.arange(GRID_W)[:, None]
    kc = np.arange(GRID_W)[None, :]
    c0 = np.clip(c - WIN_COLS // 2, 0, GRID_W - WIN_COLS)
    col_valid = (kc >= c0) & (kc < c0 + WIN_COLS)
    pick_dc = (np.arange(n_dc)[:, None, None] == (kc - c + WIN_COLS - 1)[None]) & col_valid[None]
    col = jnp.einsum("hrb,bcq->hrcq", rpb.astype(F32), jnp.asarray(pick_dc, F32), precision=lax.Precision.HIGHEST)
    col = col + jnp.asarray(np.where(col_valid, 0.0, NEG_BIG), F32)
    return jnp.concatenate([col, jnp.full((n_heads, 1, GRID_W, GRID_W), NEG_BIG, F32)], axis=1) * LOG2E


def _neighbourhood_attention(p, q_norm, k_norm, bias_tbl, layer, batch, n_lat_blk):
    m = p.shape[0]
    lat_rows = n_lat_blk * ROW_BLK
    assert lat_rows % TOK_BLK == 0 and batch * ROW_BLK == TOK_BLK
    steps_per_batch = lat_rows // TOK_BLK
    width = NA_HEADS_PER_STEP * HEAD_DIM
    qb, kb, vb = (P_COLS[name] // width for name in ("na_q", "na_k", "na_v"))
    assert NA_HEADS % NA_HEADS_PER_STEP == 0 and all(P_COLS[name] % width == 0 for name in ("na_q", "na_k", "na_v"))
    ctx_blk = batch * steps_per_batch
    groups = NA_HEADS // NA_HEADS_PER_STEP

    def q_blk(t):
        return jnp.where(t == 0, ctx_blk, t - 1)

    def kv_blk(t):
        return jnp.maximum(t - 1, 0) // steps_per_batch

    kern = functools.partial(_na_kernel, n_rows=lat_rows // GRID_W, steps_per_batch=steps_per_batch, batch=batch)
    return pl.pallas_call(
        kern,
        grid=(groups, 1 + batch * steps_per_batch),
        in_specs=[
            pl.BlockSpec((TOK_BLK, width), lambda h, t: (q_blk(t), qb + h)),
            pl.BlockSpec((lat_rows, width), lambda h, t: (kv_blk(t), kb + h)),
            pl.BlockSpec((lat_rows, width), lambda h, t: (kv_blk(t), vb + h)),
            pl.BlockSpec((TOK_BLK, width), lambda h, t: (ctx_blk, kb + h)),
            pl.BlockSpec((TOK_BLK, width), lambda h, t: (ctx_blk, vb + h)),
            pl.BlockSpec((1, LANE), lambda h, t: (0, 0)),
            pl.BlockSpec((1, LANE), lambda h, t: (0, 0)),
            pl.BlockSpec((NA_HEADS_PER_STEP,) + bias_tbl.shape[1:], lambda h, t: (layer * groups + h, 0, 0, 0)),
        ],
        out_specs=pl.BlockSpec((TOK_BLK, width), lambda h, t: (q_blk(t), h)),
        out_shape=jax.ShapeDtypeStruct((m, NA_HEADS * HEAD_DIM), BF16),
        scratch_shapes=[
            pltpu.VMEM((lat_rows, width), BF16),
            pltpu.VMEM((lat_rows, width), BF16),
            pltpu.VMEM((TOK_BLK, width), BF16),
            pltpu.VMEM((TOK_BLK, width), BF16),
            pltpu.VMEM((NA_HEADS_PER_STEP, 3, ROW_BLK, NA_SPAN_ROWS * GRID_W), F32),
        ],
        compiler_params=pltpu.CompilerParams(vmem_limit_bytes=VMEM_LIMIT),
        name="neighbourhood_attention",
    )(p, p, p, p, p, q_norm.reshape(1, -1), k_norm.reshape(1, -1), bias_tbl)


def _outproj_kernel(*refs, n_lat, n_y):
    xl_ref, xc_ref = refs[:2]
    y_refs = refs[2:2 + n_y]
    mod_ref, g_ref = refs[2 + n_y:4 + n_y]
    w_refs = refs[4 + n_y:4 + 2 * n_y]
    o_ref, h_ref = refs[4 + 2 * n_y:]
    i = pl.program_id(0)
    acc = _dot(y_refs[0][...], w_refs[0][...])
    for y_ref, w_ref in zip(y_refs[1:], w_refs[1:]):
        acc = acc + _dot(y_ref[...], w_ref[...])
    upd = mod_ref[2:3, :] * acc

    def emit(x_ref):
        x1 = x_ref[...] + upd
        o_ref[...] = x1
        h_ref[...] = _norm_mod(x1, g_ref[...], mod_ref[3:4, :], mod_ref[4:5, :]).astype(h_ref.dtype)

    pl.when(i < n_lat)(lambda: emit(xl_ref))
    pl.when(i >= n_lat)(lambda: emit(xc_ref))


def _out_projection(tok, ys, modt, g_mlp, w_out, rows_per_mod, with_ctx):
    d = w_out.shape[1]
    tm = TOK_BLK
    per = rows_per_mod // tm
    n_blk = tok.n_lat + (1 if with_ctx else 0)
    y_specs, w_specs, row0 = [], [], 0
    for y in ys:
        width = y.shape[1]
        assert row0 % width == 0
        y_specs.append(pl.BlockSpec((tm, width), lambda i: (i, 0)))
        w_specs.append(pl.BlockSpec((width, d), lambda i, blk=row0 // width: (blk, 0)))
        row0 += width
    assert row0 == w_out.shape[0]
    return pl.pallas_call(
        functools.partial(_outproj_kernel, n_lat=tok.n_lat, n_y=len(ys)),
        grid=(n_blk,),
        in_specs=tok.specs(d) + y_specs + [
            pl.BlockSpec((None, 6, d), lambda i: (i // per, 0, 0)),
            pl.BlockSpec((1, d), lambda i: (0, 0)),
        ] + w_specs,
        out_specs=[pl.BlockSpec((tm, d), lambda i: (i, 0)), pl.BlockSpec((tm, d), lambda i: (i, 0))],
        out_shape=[jax.ShapeDtypeStruct((n_blk * tm, d), F32), jax.ShapeDtypeStruct((n_blk * tm, d), BF16)],
        compiler_params=pltpu.CompilerParams(vmem_limit_bytes=VMEM_LIMIT),
        name="out_projection",
    )(tok.lat, tok.ctx, *ys, modt, g_mlp.reshape(1, d), *([w_out] * len(ys)))


def _mlp_kernel(*refs, emit_next):
    if emit_next:
        h_ref, x_ref, mod_ref, w1_ref, w2_ref, gn_ref, modn_ref, o_ref, hn_ref, acc_ref = refs
    else:
        h_ref, x_ref, mod_ref, w1_ref, w2_ref, o_ref, acc_ref = refs
    j = pl.program_id(1)

    @pl.when(j == 0)
    def _():
        acc_ref[...] = jnp.zeros_like(acc_ref)

    a = jnp.maximum(_dot(h_ref[...], w1_ref[...]), 0.0)
    acc_ref[...] += _dot((a * a).astype(BF16), w2_ref[...])

    @pl.when(j == pl.num_programs(1) - 1)
    def _():
        x2 = x_ref[...] + mod_ref[5:6, :] * acc_ref[...]
        o_ref[...] = x2
        if emit_next:
            hn_ref[...] = _norm_mod(x2, gn_ref[...], modn_ref[0:1, :], modn_ref[1:2, :]).astype(hn_ref.dtype)


def _mlp(h, x1, modt, w1, w2, rows_per_mod, next_norm):
    m, d = x1.shape
    n_j, _, tf = w1.shape
    tm = TOK_BLK
    per = rows_per_mod // tm
    emit_next = next_norm is not None

    def rows(i, j):
        return (i, 0)

    def mod_rows(i, j):
        return (i // per, 0, 0)

    in_specs = [
        pl.BlockSpec((tm, d), rows),
        pl.BlockSpec((tm, d), rows),
        pl.BlockSpec((None, 6, d), mod_rows),
        pl.BlockSpec((None, d, tf), lambda i, j: (j, 0, 0)),
        pl.BlockSpec((tf, d), lambda i, j: (j, 0)),
    ]
    args = [h, x1, modt, w1, w2]
    out_specs = [pl.BlockSpec((tm, d), rows)]
    out_shape = [jax.ShapeDtypeStruct((m, d), F32)]
    if emit_next:
        g_next, modt_next = next_norm
        in_specs += [pl.BlockSpec((1, d), lambda i, j: (0, 0)), pl.BlockSpec((None, 6, d), mod_rows)]
        args += [g_next.reshape(1, d), modt_next]
        out_specs.append(pl.BlockSpec((tm, d), rows))
        out_shape.append(jax.ShapeDtypeStruct((m, d), BF16))
    return pl.pallas_call(
        functools.partial(_mlp_kernel, emit_next=emit_next),
        grid=(m // tm, n_j),
        in_specs=in_specs,
        out_specs=out_specs,
        out_shape=out_shape,
        scratch_shapes=[pltpu.VMEM((tm, d), F32)],
        compiler_params=pltpu.CompilerParams(vmem_limit_bytes=VMEM_LIMIT_BIG),
        name="mlp",
    )(*args)


def _relayout_kernel(wt_ref, o_ref):
    al, aw = P_SRC["a_low"]
    x = wt_ref[...]
    pad = jnp.zeros((LANE - aw, x.shape[1]), x.dtype)
    cols = jnp.concatenate([x[:al], x[al + aw:], x[al:al + aw], pad], axis=0)
    o_ref[...] = cols.T.astype(o_ref.dtype)


def _permute_w_in(w_in):
    depth, d, n = w_in.shape
    tc = 256
    return pl.pallas_call(
        _relayout_kernel,
        grid=(depth, d // tc),
        in_specs=[pl.BlockSpec((None, n, tc), lambda l, i: (l, 0, i))],
        out_specs=pl.BlockSpec((None, tc, P_WIDTH), lambda l, i: (l, i, 0)),
        out_shape=jax.ShapeDtypeStruct((depth, d, P_WIDTH), BF16),
        compiler_params=pltpu.CompilerParams(vmem_limit_bytes=VMEM_LIMIT_BIG),
        name="w_in_relayout",
    )(jnp.swapaxes(w_in, 1, 2))


def _gla_gate_weights(w_a2, b_a):
    n_pairs = GLA_HEADS // 2
    wg = jnp.zeros((2, n_pairs, LANE, 2 * GLA_DK), F32)
    for dd in range(2):
        blk = w_a2[dd].reshape(GLA_RANK, n_pairs, 2 * GLA_DK).transpose(1, 0, 2)
        wg = wg.at[dd, :, dd * GLA_RANK:(dd + 1) * GLA_RANK, :].set(blk)
    bg = b_a.reshape(2, n_pairs, 1, 2 * GLA_DK)
    return wg.astype(BF16), bg.astype(F32)


def _rope_tables(seq, n_ctx):
    quarter = GLA_DK // 4
    inv_freq = ROPE_BASE ** (-np.arange(quarter, dtype=np.float64) / quarter)
    pos = np.arange(seq)
    lane = np.arange(LANE)
    is_col = (lane % GLA_DK) // (GLA_DK // 2) == 1
    first = (lane % (GLA_DK // 2)) < quarter
    p = np.where(is_col[None, :], (pos % GRID_W)[:, None], (pos // GRID_W)[:, None]).astype(np.float64)
    ang = p * inv_freq[lane % quarter][None, :]
    cos = np.cos(ang)
    sin = np.where(first[None, :], -np.sin(ang), np.sin(ang))
    cos = np.concatenate([cos, np.ones((n_ctx, LANE))], axis=0)
    sin = np.concatenate([sin, np.zeros((n_ctx, LANE))], axis=0)
    return jnp.asarray(np.concatenate([cos, sin], axis=1), F32)


def kernel(x, c, ctx, c_ctx, w_mod, b_mod, attn_norm, w_in, gla_w_a2, gla_b_a, gla_norm, hg_lower_bounds, hg_norm, na_q_norm, na_k_norm, na_rpb, w_out, mlp_norm, w_mlp1, w_mlp2):
    batch, seq, d = x.shape
    n_ctx = ctx.shape[1]
    depth = w_mod.shape[0]
    assert seq % ROW_BLK == 0 and n_ctx == ROW_BLK and batch * n_ctx == TOK_BLK and d % LANE == 0
    assert seq % TOK_BLK == 0 and seq % GRID_W == 0
    n_lat_blk = seq // ROW_BLK
    lat_rows = batch * seq
    n_lat_tok = lat_rows // TOK_BLK

    tok = _Tokens(x.reshape(lat_rows, d), ctx.reshape(batch * n_ctx, d), n_lat_tok, 0)

    cc = jnp.zeros((8, d), F32).at[0:batch].set(c).at[batch].set(c_ctx)
    mod = _modulation(cc, w_mod, b_mod)
    modt = mod[:, 0:batch + 1].reshape(depth, batch + 1, 6, d)

    lb_p = jax.nn.softmax(hg_lower_bounds.astype(F32), axis=0)
    lower = jnp.cumsum(lb_p, axis=0) - lb_p[0]
    lower = lower.reshape(depth, 2, HG_HEADS // 2, 1, 2 * HEAD_DIM)
    rope_t = _rope_tables(seq, n_ctx)

    bias_tbl = _na_bias_table(na_rpb.reshape((depth * NA_HEADS,) + na_rpb.shape[2:]))
    w_in_b = _permute_w_in(w_in)

    h = _first_norm(tok, modt[0], attn_norm[0], seq)
    for l in range(depth):
        last = l == depth - 1
        p, w1_b, w2_b, w_out_b = _in_projection(h, w_in_b, w_mlp1, w_mlp2, w_out, l)
        wg, bg = _gla_gate_weights(gla_w_a2[l], gla_b_a[l])
        ya, yb = _gated_scans(p, rope_t, wg, bg, lower[l], gla_norm[l], hg_norm[l], batch, n_lat_blk)
        yc = _neighbourhood_attention(p, na_q_norm[l], na_k_norm[l], bias_tbl, l, batch, n_lat_blk)
        x1, h2 = _out_projection(tok, (ya, yb, yc), modt[l], mlp_norm[l], w_out_b, seq, with_ctx=not last)
        if last:
            (xu,) = _mlp(h2, x1, modt[l], w1_b, w2_b, seq, None)
        else:
            xu, h = _mlp(h2, x1, modt[l], w1_b, w2_b, seq, (attn_norm[l + 1], modt[l + 1]))
        tok = _Tokens(xu, xu, n_lat_tok, n_lat_tok)
    return xu.reshape(batch, seq, d)
```

```python
import functools

import numpy as np
import jax
import jax.numpy as jnp
from jax import lax
from jax.experimental import pallas as pl
from jax.experimental.pallas import tpu as pltpu

F32 = jnp.float32
BF16 = jnp.bfloat16

EPS = 1e-6
GRID_W = 64
HEAD_DIM = 128
GLA_HEADS = 4
GLA_DK = 64
GLA_RANK = 16
GLA_TAU = 16.0
HG_HEADS = 4
NA_HEADS = 8
WIN_ROWS = 8
WIN_COLS = 16
ROPE_BASE = 10000.0
NEG_BIG = -1e30
LOG2E = 1.4426950408889634

LANE = 128
ROW_BLK = 256
CHUNK = 128
N_LEVELS = 7
TOK_BLK = 512
MLP_TF = 1024
VMEM_LIMIT = 48 * 1024 * 1024
VMEM_LIMIT_BIG = 56 * 1024 * 1024

NA_Q_ROWS = ROW_BLK // GRID_W
NA_SPAN_ROWS = 12
NA_HEADS_PER_STEP = 4
SCAN_PAIRS_PER_STEP = 2


def _projection_layout():
    gw, hw, nw = GLA_HEADS * HEAD_DIM, HG_HEADS * HEAD_DIM, NA_HEADS * HEAD_DIM
    sections = (("gla_q", GLA_HEADS * GLA_DK), ("gla_k", GLA_HEADS * GLA_DK), ("gla_v", gw), ("gla_r", gw),
                ("a_low", 2 * GLA_RANK), ("hg_q", hw), ("hg_f", 2 * hw), ("hg_i", hw), ("hg_g", hw),
                ("na_q", nw), ("na_k", nw), ("na_v", nw))
    src, off = {}, 0
    for name, width in sections:
        src[name] = (off, width)
        off += width
    dst, off = {}, 0
    for name, width in sections:
        if name != "a_low":
            dst[name] = off
            off += width
    dst["a_low"] = off
    return src, dst, off + LANE


P_SRC, P_COLS, P_WIDTH = _projection_layout()


def _dot(a, b):
    return jnp.dot(a, b, preferred_element_type=F32)


def _dot_nt(a, b):
    return lax.dot_general(a, b, (((1,), (1,)), ((), ())), preferred_element_type=F32)


def _dot_tn(a, b):
    return lax.dot_general(a, b, (((0,), (0,)), ((), ())), preferred_element_type=F32)


def _sigmoid(x):
    return 1.0 / (1.0 + jnp.exp(-x))


def _silu(x):
    return x * _sigmoid(x)


def _split3(x):
    hi = x.astype(BF16)
    r1 = x - hi.astype(F32)
    mid = r1.astype(BF16)
    lo = (r1 - mid.astype(F32)).astype(BF16)
    return jnp.concatenate([hi, mid, lo], axis=-1)


def _sum3(y, w):
    return y[:, 0:w] + y[:, w:2 * w] + y[:, 2 * w:3 * w]


def _mod_kernel(c_ref, w_ref, b_ref, o_ref):
    s = _silu(c_ref[...]).astype(BF16)
    o_ref[...] = _dot(s, w_ref[...].astype(BF16)) + b_ref[...]


def _modulation(cc, w_mod, b_mod):
    depth, d, n = w_mod.shape
    tn = 1024
    return pl.pallas_call(
        _mod_kernel,
        grid=(depth, n // tn),
        in_specs=[
            pl.BlockSpec((8, d), lambda l, j: (0, 0)),
            pl.BlockSpec((None, d, tn), lambda l, j: (l, 0, j)),
            pl.BlockSpec((None, 1, tn), lambda l, j: (l, 0, j)),
        ],
        out_specs=pl.BlockSpec((None, 8, tn), lambda l, j: (l, 0, j)),
        out_shape=jax.ShapeDtypeStruct((depth, 8, n), F32),
        compiler_params=pltpu.CompilerParams(vmem_limit_bytes=VMEM_LIMIT),
        name="modulation",
    )(cc, w_mod, b_mod.reshape(depth, 1, n))


def _norm_mod(x, g, shift, scale):
    y = x * lax.rsqrt(jnp.mean(x * x, axis=-1, keepdims=True) + EPS) * g
    return y * (1.0 + scale) + shift


class _Tokens:
    def __init__(self, lat, ctx, n_lat, ctx_blk):
        self.lat, self.ctx, self.n_lat, self.ctx_blk = lat, ctx, n_lat, ctx_blk

    def specs(self, d):
        n_lat, ctx_blk = self.n_lat, self.ctx_blk
        return [pl.BlockSpec((TOK_BLK, d), lambda i, *_: (jnp.minimum(i, n_lat - 1), 0)),
                pl.BlockSpec((TOK_BLK, d), lambda i, *_: (ctx_blk, 0))]


def _first_norm_kernel(xl_ref, xc_ref, mod_ref, g_ref, h_ref, *, n_lat):
    i = pl.program_id(0)

    def emit(x_ref):
        h_ref[...] = _norm_mod(x_ref[...], g_ref[...], mod_ref[0:1, :], mod_ref[1:2, :]).astype(h_ref.dtype)

    pl.when(i < n_lat)(lambda: emit(xl_ref))
    pl.when(i >= n_lat)(lambda: emit(xc_ref))


def _first_norm(tok, modt, g, rows_per_mod):
    d = tok.lat.shape[1]
    per = rows_per_mod // TOK_BLK
    n_blk = tok.n_lat + 1
    return pl.pallas_call(
        functools.partial(_first_norm_kernel, n_lat=tok.n_lat),
        grid=(n_blk,),
        in_specs=tok.specs(d) + [
            pl.BlockSpec((None, 6, d), lambda i: (i // per, 0, 0)),
            pl.BlockSpec((1, d), lambda i: (0, 0)),
        ],
        out_specs=pl.BlockSpec((TOK_BLK, d), lambda i: (i, 0)),
        out_shape=jax.ShapeDtypeStruct((n_blk * TOK_BLK, d), BF16),
        compiler_params=pltpu.CompilerParams(vmem_limit_bytes=VMEM_LIMIT),
        name="first_norm",
    )(tok.lat, tok.ctx, modt, g.reshape(1, d))


def _inproj_kernel(h_ref, w_ref, w1_ref, w2_ref, wo_ref, o_ref, w1o_ref, w2o_ref, woo_ref, *, n_cast):
    o_ref[...] = _dot(h_ref[...], w_ref[...])

    @pl.when(pl.program_id(0) * pl.num_programs(1) + pl.program_id(1) < n_cast)
    def _():
        w1o_ref[...] = w1_ref[...].astype(w1o_ref.dtype)
        w2o_ref[...] = w2_ref[...].astype(w2o_ref.dtype)
        woo_ref[...] = wo_ref[...].astype(woo_ref.dtype)


def _in_projection(h, w, w1, w2, w_out, layer):
    m, d = h.shape
    n = w.shape[2]
    ff = w1.shape[2]
    d_out = w_out.shape[2]
    tn = n // 3
    n_i = m // TOK_BLK
    tc = 256
    n_cast = ff // tc
    to = w_out.shape[1] // n_cast
    per_slab = MLP_TF // tc
    assert n_cast <= 3 * n_i and w_out.shape[1] % n_cast == 0 and to % 16 == 0 and MLP_TF % tc == 0

    def cast_blk(j, i):
        return jnp.minimum(j * n_i + i, n_cast - 1)

    return pl.pallas_call(
        functools.partial(_inproj_kernel, n_cast=n_cast),
        grid=(n // tn, n_i),
        in_specs=[
            pl.BlockSpec((TOK_BLK, d), lambda j, i: (i, 0)),
            pl.BlockSpec((None, d, tn), lambda j, i: (layer, 0, j)),
            pl.BlockSpec((None, d, tc), lambda j, i: (layer, 0, cast_blk(j, i))),
            pl.BlockSpec((None, tc, d), lambda j, i: (layer, cast_blk(j, i), 0)),
            pl.BlockSpec((None, to, d_out), lambda j, i: (layer, cast_blk(j, i), 0)),
        ],
        out_specs=[
            pl.BlockSpec((TOK_BLK, tn), lambda j, i: (i, j)),
            pl.BlockSpec((None, d, tc), lambda j, i: (cast_blk(j, i) // per_slab, 0, cast_blk(j, i) % per_slab)),
            pl.BlockSpec((tc, d), lambda j, i: (cast_blk(j, i), 0)),
            pl.BlockSpec((to, d_out), lambda j, i: (cast_blk(j, i), 0)),
        ],
        out_shape=[
            jax.ShapeDtypeStruct((m, n), F32),
            jax.ShapeDtypeStruct((ff // MLP_TF, d, MLP_TF), BF16),
            jax.ShapeDtypeStruct((ff, d), BF16),
            jax.ShapeDtypeStruct(w_out.shape[1:], BF16),
        ],
        compiler_params=pltpu.CompilerParams(vmem_limit_bytes=VMEM_LIMIT_BIG),
        name="in_projection",
    )(h, w, w1, w2, w_out)


def _scan_constants(reverse):
    c = CHUNK
    i = np.arange(c)[:, None]
    j = np.arange(c)[None, :]
    tri = (j >= i) if reverse else (j <= i)
    mask = np.zeros((N_LEVELS + 1, c, c), np.float32)
    for l in range(N_LEVELS):
        level = ((i ^ j) >> l) == 1
        mask[l] = level & ((i < j) if reverse else (i > j))
    mask[N_LEVELS] = np.eye(c, dtype=np.float32)
    return jnp.asarray(tri, BF16), jnp.asarray(np.concatenate([mask, mask], axis=1), BF16)


def _level_exponents(g, cum, scr_ref, reverse):
    c = CHUNK
    scr_ref[...] = cum
    row = lax.broadcasted_iota(jnp.int32, (c, LANE), 0)
    up = pltpu.roll(g, c - 1, 0)
    dn = pltpu.roll(g, 1, 0)
    m2 = row & 1
    m4 = row & 3
    if reverse:
        e0 = jnp.where(m2 == 0, g, 0.0)
        e1 = jnp.where(m4 == 0, g + up, jnp.where(m4 == 1, g, jnp.where(m4 == 2, 0.0, dn)))
    else:
        e0 = jnp.where(m2 == 1, g, 0.0)
        e1 = jnp.where(m4 == 0, up, jnp.where(m4 == 1, 0.0, jnp.where(m4 == 2, g, g + dn)))
    out = [e0, e1]
    for l in range(2, N_LEVELS):
        s = 1 << l
        pieces = []
        for blk in range(c // (2 * s)):
            lo = blk * 2 * s
            ref = scr_ref[lo + (s if reverse else s - 1):lo + (s if reverse else s - 1) + 1, :]
            if s < 8:
                pieces.append(-jnp.abs(cum[lo:lo + 2 * s] - ref))
            else:
                first, second = cum[lo:lo + s], cum[lo + s:lo + 2 * s]
                pieces += [first - ref, ref - second] if reverse else [ref - first, second - ref]
        out.append(pieces[0] if len(pieces) == 1 else jnp.concatenate(pieces, axis=0))
    return out


def _pair_weights(qs, k, exps, mask_ref):
    n = len(qs) * CHUNK

    def lhs(w):
        parts = [q if w is None else q * w for q in qs]
        return parts[0] if len(parts) == 1 else jnp.concatenate(parts, axis=0)

    att = mask_ref[N_LEVELS, 0:n, :] * _dot_nt(lhs(None), k).astype(BF16)
    for l, e in enumerate(exps):
        w = jnp.exp2(e).astype(BF16)
        att = att + mask_ref[l, 0:n, :] * _dot_nt(lhs(w), k * w).astype(BF16)
    return att


def _readout(o, gate, g):
    y = o * lax.rsqrt(jnp.mean(o * o, axis=-1, keepdims=True) + EPS) * g
    return y * _silu(gate)


class _Cols:
    def __init__(self, ref, start):
        self.ref, self.start = ref, start

    def __getitem__(self, idx):
        rows, cols = idx
        return self.ref[rows, self.start + cols.start:self.start + cols.stop]


def _scan_kernel(*refs, reverse):
    p_ref, al_ref, cs_ref, wg_ref, bg_ref, lb_ref, tri_ref, mask_ref = refs[:8]
    if reverse:
        o_ref, scr_ref, sg_ref, sh_ref = refs[8:]
    else:
        orev_ref, gn_ref, hn_ref, ya_ref, yb_ref, scr_ref, sg_ref, sh_ref = refs[8:]
    c = CHUNK
    gq_ref, gk_ref, gv_ref, gr_ref, hq_ref, hi_ref, hg_ref = (
        _Cols(p_ref, P_COLS[name]) for name in ("gla_q", "gla_k", "gla_v", "gla_r", "hg_q", "hg_i", "hg_g"))
    hf_ref = _Cols(p_ref, P_COLS["hg_f"] + (HG_HEADS * HEAD_DIM if reverse else 0))
    cos_ref, sin_ref = _Cols(cs_ref, 0), _Cols(cs_ref, LANE)

    @pl.when(pl.program_id(1) == 0)
    def _():
        sg_ref[...] = jnp.zeros_like(sg_ref)
        sh_ref[...] = jnp.zeros_like(sh_ref)

    lane = lax.broadcasted_iota(jnp.int32, (c, LANE), 1)
    first_half = (lane % (GLA_DK // 2)) < GLA_DK // 4
    head_a = lane < GLA_DK
    lane_masks = (head_a, jnp.logical_not(head_a))
    state_mask = jnp.concatenate([head_a, jnp.logical_not(head_a)], axis=0)
    end_row = 0 if reverse else c - 1
    n_chunks = ROW_BLK // c
    offsets = [(n_chunks - 1 - ci) * c if reverse else ci * c for ci in range(n_chunks)]

    def rope(x, rows):
        swapped = jnp.where(first_half, pltpu.roll(x, LANE - GLA_DK // 4, 1), pltpu.roll(x, GLA_DK // 4, 1))
        return x * cos_ref[rows, 0:LANE] + swapped * sin_ref[rows, 0:LANE]

    chains = []
    for pr in range(SCAN_PAIRS_PER_STEP):
        gla = slice(pr * LANE, (pr + 1) * LANE)
        for off in offsets:
            rows = slice(off, off + c)
            q = rope(gq_ref[rows, gla], rows) * (GLA_DK ** -0.5)
            k = rope(gk_ref[rows, gla], rows)
            logit = _dot(al_ref[rows, :].astype(BF16), wg_ref[pr]) + bg_ref[pr]
            soft = jnp.log2(1.0 + jnp.exp2(jnp.abs(logit) * -LOG2E))
            g = (jnp.minimum(logit, 0.0) * LOG2E - soft) * (1.0 / GLA_TAU)
            chains.append((q, k, g))
            for hh in range(2):
                head = slice((2 * pr + hh) * LANE, (2 * pr + hh + 1) * LANE)
                lb = lb_ref[pr][:, hh * LANE:(hh + 1) * LANE]
                logit = hf_ref[rows, head]
                e = jnp.exp2(jnp.abs(logit) * -LOG2E)
                r = 1.0 / (1.0 + e)
                sig = jnp.where(logit >= 0, r, e * r)
                sig_neg = jnp.where(logit >= 0, e * r, r)
                g = jnp.log2(lb + (1.0 - lb) * sig)
                k = (1.0 - lb) * sig_neg
                q = _silu(hq_ref[rows, head])
                chains.append((q, k, g))

    g3 = jnp.concatenate([_split3(g) for (_, _, g) in chains], axis=-1)
    cum_all = _dot(tri_ref[...], g3)
    cums = [_sum3(cum_all[:, 3 * LANE * n:3 * LANE * (n + 1)], LANE) for n in range(len(chains))]

    atts = []
    for n, ((q, k, g), cum) in enumerate(zip(chains, cums)):
        exps = _level_exponents(g, cum, scr_ref.at[n], reverse)
        qs = [jnp.where(m, q, 0.0).astype(BF16) for m in lane_masks] if n % 3 == 0 else [q.astype(BF16)]
        atts.append(_pair_weights(qs, k.astype(BF16), exps, mask_ref))

    for pr in range(SCAN_PAIRS_PER_STEP):
        for ci, off in enumerate(offsets):
            rows = slice(off, off + c)
            n0 = (pr * n_chunks + ci) * 3
            (q, k, _), cum, att = chains[n0], cums[n0], atts[n0]
            v = gv_ref[rows, pr * 2 * LANE:(pr + 1) * 2 * LANE].astype(BF16)
            cum_end = cum[end_row:end_row + 1, :]
            st = sg_ref[pr]
            inter = _dot_nt((q * jnp.exp2(cum)).astype(BF16), st.astype(BF16))
            o_gla = jnp.concatenate([_dot(att[0:c], v[:, 0:LANE]), _dot(att[c:2 * c], v[:, LANE:2 * LANE])],
                                    axis=-1) + inter
            kd = (k * jnp.exp2(cum_end - cum)).astype(BF16)
            sg_ref[pr] = st * jnp.exp2(cum_end) + jnp.where(state_mask, _dot_tn(v, kd), 0.0)
            o_hg = []
            for hh in range(2):
                head = slice((2 * pr + hh) * LANE, (2 * pr + hh + 1) * LANE)
                (q, k, _), cum, att = chains[n0 + 1 + hh], cums[n0 + 1 + hh], atts[n0 + 1 + hh]
                v = hi_ref[rows, head].astype(BF16)
                cum_end = cum[end_row:end_row + 1, :]
                st = sh_ref[2 * pr + hh]
                o_hg.append(_dot(att, v) + _dot_nt((q * jnp.exp2(cum)).astype(BF16), st.astype(BF16)))
                kd = (k * jnp.exp2(cum_end - cum)).astype(BF16)
                sh_ref[2 * pr + hh] = st * jnp.exp2(cum_end) + _dot_tn(v, kd)
            o_all = jnp.concatenate([o_gla] + o_hg, axis=-1)
            out_cols = slice(pr * 4 * LANE, (pr + 1) * 4 * LANE)
            if reverse:
                o_ref[rows, out_cols] = o_all
            else:
                o_all = o_all + orev_ref[rows, out_cols]
                for hh in range(2):
                    head = slice((2 * pr + hh) * LANE, (2 * pr + hh + 1) * LANE)
                    y = _readout(o_all[:, hh * LANE:(hh + 1) * LANE], gr_ref[rows, head], gn_ref[...])
                    ya_ref[rows, head] = y.astype(ya_ref.dtype)
                    y = _readout(o_all[:, (2 + hh) * LANE:(3 + hh) * LANE], hg_ref[rows, head], hn_ref[...])
                    yb_ref[rows, head] = y.astype(yb_ref.dtype)


def _gated_scans(p, rope_t, wg, bg, lb, gla_norm, hg_norm, batch, n_lat_blk):
    assert CHUNK == HEAD_DIM == LANE and GLA_HEADS == HG_HEADS
    assert GLA_HEADS == 2 * SCAN_PAIRS_PER_STEP
    m = p.shape[0]
    pps = SCAN_PAIRS_PER_STEP
    n_blk = n_lat_blk + 1
    scan_cols = P_COLS["na_q"]
    assert all(P_COLS[s] < scan_cols for s in ("gla_q", "gla_k", "gla_v", "gla_r", "hg_q", "hg_f", "hg_i", "hg_g"))

    def run(reverse, extra_in, extra_specs, out_shape, out_specs):
        def pos(t):
            lat = (n_lat_blk - t) if reverse else (t - 1)
            return jnp.where(t == 0, n_lat_blk, lat)

        def row_blk(b, t):
            return jnp.where(t == 0, batch * n_lat_blk + b, b * n_lat_blk + pos(t))

        def rows(width, col_blk=0):
            return pl.BlockSpec((ROW_BLK, width), lambda b, t: (row_blk(b, t), col_blk))

        def const(shape):
            return pl.BlockSpec(shape, lambda b, t: (0,) * len(shape))

        d = 1 if reverse else 0
        tri, mask = _scan_constants(reverse)
        in_specs = [
            rows(scan_cols),
            rows(LANE, P_COLS["a_low"] // LANE),
            pl.BlockSpec((ROW_BLK, 2 * LANE), lambda b, t: (pos(t), 0)),
            pl.BlockSpec((None, pps, LANE, LANE), lambda b, t: (d, 0, 0, 0)),
            pl.BlockSpec((None, pps, 1, LANE), lambda b, t: (d, 0, 0, 0)),
            pl.BlockSpec((None, pps, 1, 2 * LANE), lambda b, t: (d, 0, 0, 0)),
            const((CHUNK, CHUNK)),
            const((N_LEVELS + 1, 2 * CHUNK, CHUNK)),
        ] + [s(rows, const) for s in extra_specs]
        return pl.pallas_call(
            functools.partial(_scan_kernel, reverse=reverse),
            grid=(batch, n_blk),
            in_specs=in_specs,
            out_specs=[s(rows, const) for s in out_specs],
            out_shape=out_shape,
            scratch_shapes=[
                pltpu.VMEM((pps * 3 * ROW_BLK // CHUNK, CHUNK, LANE), F32),
                pltpu.VMEM((pps, 2 * HEAD_DIM, LANE), F32),
                pltpu.VMEM((pps * 2, HEAD_DIM, HEAD_DIM), F32),
            ],
            compiler_params=pltpu.CompilerParams(vmem_limit_bytes=VMEM_LIMIT),
            name="gated_scan_rev" if reverse else "gated_scan_fwd",
        )(p, p, rope_t, wg, bg, lb, tri, mask, *extra_in)

    def rows_spec(width):
        return lambda rows, const: rows(width)

    def const_spec(rows, const):
        return const((1, LANE))

    n_heads = GLA_HEADS + HG_HEADS
    (o_rev,) = run(True, (), (), [jax.ShapeDtypeStruct((m, n_heads * HEAD_DIM), F32)],
                   [rows_spec(n_heads * HEAD_DIM)])
    return run(False, (o_rev, gla_norm.reshape(1, -1), hg_norm.reshape(1, -1)),
               (rows_spec(n_heads * HEAD_DIM), const_spec, const_spec),
               [jax.ShapeDtypeStruct((m, GLA_HEADS * HEAD_DIM), BF16),
                jax.ShapeDtypeStruct((m, HG_HEADS * HEAD_DIM), BF16)],
               [rows_spec(GLA_HEADS * HEAD_DIM), rows_spec(HG_HEADS * HEAD_DIM)])


def _head_norm(x, g):
    return x * lax.rsqrt(jnp.mean(x * x, axis=-1, keepdims=True) + EPS) * g


def _na_span_base(n, n_rows, xp=jnp):
    return xp.clip(n * NA_Q_ROWS - WIN_ROWS // 2, 0, n_rows - NA_SPAN_ROWS)


def _na_variants(n_rows):
    n_blocks = n_rows // NA_Q_ROWS
    assert n_rows % NA_Q_ROWS == 0 and n_blocks >= 4

    def rows(n):
        base = int(_na_span_base(n, n_rows, np))
        out = []
        for i in range(NA_Q_ROWS):
            r = n * NA_Q_ROWS + i
            r0 = min(max(r - WIN_ROWS // 2, 0), n_rows - WIN_ROWS)
            out.append([(base + kk - r + WIN_ROWS - 1, r0 <= base + kk < r0 + WIN_ROWS)
                        for kk in range(NA_SPAN_ROWS)])
        return out

    variants = [rows(0), rows(1), rows(n_blocks - 1)]
    assert all(rows(n) == variants[1] for n in range(1, n_blocks - 1))
    return variants


def _na_kernel(q_ref, kl_ref, vl_ref, kc_ref, vc_ref, qn_ref, kn_ref, tbl_ref, y_ref,
               kls_ref, vls_ref, kcs_ref, vcs_ref, bias_ref, *, n_rows, steps_per_batch, batch):
    t = pl.program_id(1)
    heads = [slice(hd * HEAD_DIM, (hd + 1) * HEAD_DIM) for hd in range(NA_HEADS_PER_STEP)]

    @pl.when((t > 0) & ((t - 1) % steps_per_batch == 0))
    def _():
        for cols in heads:
            kls_ref[:, cols] = _head_norm(kl_ref[:, cols], kn_ref[...]).astype(BF16)
        vls_ref[...] = vl_ref[...].astype(BF16)

    @pl.when(t == 0)
    def _():
        for cols in heads:
            kcs_ref[:, cols] = _head_norm(kc_ref[:, cols], kn_ref[...]).astype(BF16)
        vcs_ref[...] = vc_ref[...].astype(BF16)
        n_dr = tbl_ref.shape[1] - 1
        for hd in range(NA_HEADS_PER_STEP):
            for v, var in enumerate(_na_variants(n_rows)):
                for i, row in enumerate(var):
                    for g in range(NA_SPAN_ROWS // 2):
                        pair = [tbl_ref[hd, dr if ok else n_dr] for dr, ok in row[2 * g:2 * g + 2]]
                        assert all(0 <= dr < n_dr for dr, ok in row[2 * g:2 * g + 2] if ok)
                        bias_ref[hd, v, i * GRID_W:(i + 1) * GRID_W, g * LANE:(g + 1) * LANE] = (
                            jnp.concatenate(pair, axis=1))

    qs = [_head_norm(q_ref[:, cols], qn_ref[...] * (HEAD_DIM ** -0.5 * LOG2E)).astype(BF16) for cols in heads]

    @pl.when(t == 0)
    def _():
        for q, cols in zip(qs, heads):
            for b in range(batch):
                rows = slice(b * ROW_BLK, (b + 1) * ROW_BLK)
                s = _dot_nt(q[rows], kcs_ref[rows, cols])
                e = jnp.exp2(s - jnp.max(s, axis=-1, keepdims=True))
                o = _dot(e.astype(BF16), vcs_ref[rows, cols]) / jnp.sum(e, axis=-1, keepdims=True)
                y_ref[rows, cols] = o.astype(y_ref.dtype)

    @pl.when(t > 0)
    def _():
        ctx_rows = pl.ds(pl.multiple_of(((t - 1) // steps_per_batch) * ROW_BLK, ROW_BLK), ROW_BLK)
        for sub in range(TOK_BLK // ROW_BLK):
            rows = slice(sub * ROW_BLK, (sub + 1) * ROW_BLK)
            n = ((t - 1) % steps_per_batch) * (TOK_BLK // ROW_BLK) + sub
            base = _na_span_base(n, n_rows)
            variant = jnp.where(n == 0, 0, jnp.where(n == n_rows // NA_Q_ROWS - 1, 2, 1))
            span = pl.ds(pl.multiple_of(base * GRID_W, GRID_W), NA_SPAN_ROWS * GRID_W)
            for hd, (q, cols) in enumerate(zip(qs, heads)):
                s_win = _dot_nt(q[rows], kls_ref[span, cols]) + bias_ref[hd, variant]
                s_ctx = _dot_nt(q[rows], kcs_ref[ctx_rows, cols])
                mx = jnp.maximum(jnp.max(s_win, axis=-1, keepdims=True), jnp.max(s_ctx, axis=-1, keepdims=True))
                e_win = jnp.exp2(s_win - mx)
                e_ctx = jnp.exp2(s_ctx - mx)
                den = jnp.sum(e_win, axis=-1, keepdims=True) + jnp.sum(e_ctx, axis=-1, keepdims=True)
                o = (_dot(e_win.astype(BF16), vls_ref[span, cols])
                     + _dot(e_ctx.astype(BF16), vcs_ref[ctx_rows, cols]))
                y_ref[rows, cols] = (o / den).astype(y_ref.dtype)


def _na_bias_table(rpb):
    n_heads, n_dr, n_dc = rpb.shape
    c = np.arange(GRID_W)[:, None]
    kc = np.arange(GRID_W)[None, :]
    c0 = np.clip(c - WIN_COLS // 2, 0, GRID_W - WIN_COLS)
    col_valid = (kc >= c0) & (kc < c0 + WIN_COLS)
    pick_dc = (np.arange(n_dc)[:, None, None] == (kc - c + WIN_COLS - 1)[None]) & col_valid[None]
    col = jnp.einsum("hrb,bcq->hrcq", rpb.astype(F32), jnp.asarray(pick_dc, F32), precision=lax.Precision.HIGHEST)
    col = col + jnp.asarray(np.where(col_valid, 0.0, NEG_BIG), F32)
    return jnp.concatenate([col, jnp.full((n_heads, 1, GRID_W, GRID_W), NEG_BIG, F32)], axis=1) * LOG2E


def _neighbourhood_attention(p, q_norm, k_norm, bias_tbl, layer, batch, n_lat_blk):
    m = p.shape[0]
    lat_rows = n_lat_blk * ROW_BLK
    assert lat_rows % TOK_BLK == 0 and batch * ROW_BLK == TOK_BLK
    steps_per_batch = lat_rows // TOK_BLK
    width = NA_HEADS_PER_STEP * HEAD_DIM
    qb, kb, vb = (P_COLS[name] // width for name in ("na_q", "na_k", "na_v"))
    assert NA_HEADS % NA_HEADS_PER_STEP == 0 and all(P_COLS[name] % width == 0 for name in ("na_q", "na_k", "na_v"))
    ctx_blk = batch * steps_per_batch
    groups = NA_HEADS // NA_HEADS_PER_STEP

    def q_blk(t):
        return jnp.where(t == 0, ctx_blk, t - 1)

    def kv_blk(t):
        return jnp.maximum(t - 1, 0) // steps_per_batch

    kern = functools.partial(_na_kernel, n_rows=lat_rows // GRID_W, steps_per_batch=steps_per_batch, batch=batch)
    return pl.pallas_call(
        kern,
        grid=(groups, 1 + batch * steps_per_batch),
        in_specs=[
            pl.BlockSpec((TOK_BLK, width), lambda h, t: (q_blk(t), qb + h)),
            pl.BlockSpec((lat_rows, width), lambda h, t: (kv_blk(t), kb + h), pipeline_mode=pl.Buffered(1)),
            pl.BlockSpec((lat_rows, width), lambda h, t: (kv_blk(t), vb + h), pipeline_mode=pl.Buffered(1)),
            pl.BlockSpec((TOK_BLK, width), lambda h, t: (ctx_blk, kb + h)),
            pl.BlockSpec((TOK_BLK, width), lambda h, t: (ctx_blk, vb + h)),
            pl.BlockSpec((1, LANE), lambda h, t: (0, 0)),
            pl.BlockSpec((1, LANE), lambda h, t: (0, 0)),
            pl.BlockSpec((NA_HEADS_PER_STEP,) + bias_tbl.shape[1:], lambda h, t: (layer * groups + h, 0, 0, 0)),
        ],
        out_specs=pl.BlockSpec((TOK_BLK, width), lambda h, t: (q_blk(t), h)),
        out_shape=jax.ShapeDtypeStruct((m, NA_HEADS * HEAD_DIM), BF16),
        scratch_shapes=[
            pltpu.VMEM((lat_rows, width), BF16),
            pltpu.VMEM((lat_rows, width), BF16),
            pltpu.VMEM((TOK_BLK, width), BF16),
            pltpu.VMEM((TOK_BLK, width), BF16),
            pltpu.VMEM((NA_HEADS_PER_STEP, 3, ROW_BLK, NA_SPAN_ROWS * GRID_W), F32),
        ],
        compiler_params=pltpu.CompilerParams(vmem_limit_bytes=VMEM_LIMIT_BIG),
        name="neighbourhood_attention",
    )(p, p, p, p, p, q_norm.reshape(1, -1), k_norm.reshape(1, -1), bias_tbl)


def _outproj_kernel(*refs, n_lat, n_y):
    xl_ref, xc_ref = refs[:2]
    y_refs = refs[2:2 + n_y]
    mod_ref, g_ref = refs[2 + n_y:4 + n_y]
    w_refs = refs[4 + n_y:4 + 2 * n_y]
    o_ref, h_ref = refs[4 + 2 * n_y:]
    i = pl.program_id(0)
    acc = _dot(y_refs[0][...], w_refs[0][...])
    for y_ref, w_ref in zip(y_refs[1:], w_refs[1:]):
        acc = acc + _dot(y_ref[...], w_ref[...])
    upd = mod_ref[2:3, :] * acc

    def emit(x_ref):
        x1 = x_ref[...] + upd
        o_ref[...] = x1
        h_ref[...] = _norm_mod(x1, g_ref[...], mod_ref[3:4, :], mod_ref[4:5, :]).astype(h_ref.dtype)

    pl.when(i < n_lat)(lambda: emit(xl_ref))
    pl.when(i >= n_lat)(lambda: emit(xc_ref))


def _out_projection(tok, ys, modt, g_mlp, w_out, rows_per_mod, with_ctx):
    d = w_out.shape[1]
    tm = TOK_BLK
    per = rows_per_mod // tm
    n_blk = tok.n_lat + (1 if with_ctx else 0)
    y_specs, w_specs, row0 = [], [], 0
    for y in ys:
        width = y.shape[1]
        assert row0 % width == 0
        y_specs.append(pl.BlockSpec((tm, width), lambda i: (i, 0)))
        w_specs.append(pl.BlockSpec((width, d), lambda i, blk=row0 // width: (blk, 0)))
        row0 += width
    assert row0 == w_out.shape[0]
    return pl.pallas_call(
        functools.partial(_outproj_kernel, n_lat=tok.n_lat, n_y=len(ys)),
        grid=(n_blk,),
        in_specs=tok.specs(d) + y_specs + [
            pl.BlockSpec((None, 6, d), lambda i: (i // per, 0, 0)),
            pl.BlockSpec((1, d), lambda i: (0, 0)),
        ] + w_specs,
        out_specs=[pl.BlockSpec((tm, d), lambda i: (i, 0)), pl.BlockSpec((tm, d), lambda i: (i, 0))],
        out_shape=[jax.ShapeDtypeStruct((n_blk * tm, d), F32), jax.ShapeDtypeStruct((n_blk * tm, d), BF16)],
        compiler_params=pltpu.CompilerParams(vmem_limit_bytes=VMEM_LIMIT),
        name="out_projection",
    )(tok.lat, tok.ctx, *ys, modt, g_mlp.reshape(1, d), *([w_out] * len(ys)))


def _mlp_kernel(*refs, emit_next):
    if emit_next:
        h_ref, x_ref, mod_ref, w1_ref, w2_ref, gn_ref, modn_ref, o_ref, hn_ref, acc_ref = refs
    else:
        h_ref, x_ref, mod_ref, w1_ref, w2_ref, o_ref, acc_ref = refs
    j = pl.program_id(1)

    @pl.when(j == 0)
    def _():
        acc_ref[...] = jnp.zeros_like(acc_ref)

    a = jnp.maximum(_dot(h_ref[...], w1_ref[...]), 0.0)
    acc_ref[...] += _dot((a * a).astype(BF16), w2_ref[...])

    @pl.when(j == pl.num_programs(1) - 1)
    def _():
        x2 = x_ref[...] + mod_ref[5:6, :] * acc_ref[...]
        o_ref[...] = x2
        if emit_next:
            hn_ref[...] = _norm_mod(x2, gn_ref[...], modn_ref[0:1, :], modn_ref[1:2, :]).astype(hn_ref.dtype)


def _mlp(h, x1, modt, w1, w2, rows_per_mod, next_norm):
    m, d = x1.shape
    n_j, _, tf = w1.shape
    tm = TOK_BLK
    per = rows_per_mod // tm
    emit_next = next_norm is not None

    def rows(i, j):
        return (i, 0)

    def mod_rows(i, j):
        return (i // per, 0, 0)

    in_specs = [
        pl.BlockSpec((tm, d), rows),
        pl.BlockSpec((tm, d), rows),
        pl.BlockSpec((None, 6, d), mod_rows),
        pl.BlockSpec((None, d, tf), lambda i, j: (j, 0, 0)),
        pl.BlockSpec((tf, d), lambda i, j: (j, 0)),
    ]
    args = [h, x1, modt, w1, w2]
    out_specs = [pl.BlockSpec((tm, d), rows)]
    out_shape = [jax.ShapeDtypeStruct((m, d), F32)]
    if emit_next:
        g_next, modt_next = next_norm
        in_specs += [pl.BlockSpec((1, d), lambda i, j: (0, 0)), pl.BlockSpec((None, 6, d), mod_rows)]
        args += [g_next.reshape(1, d), modt_next]
        out_specs.append(pl.BlockSpec((tm, d), rows))
        out_shape.append(jax.ShapeDtypeStruct((m, d), BF16))
    return pl.pallas_call(
        functools.partial(_mlp_kernel, emit_next=emit_next),
        grid=(m // tm, n_j),
        in_specs=in_specs,
        out_specs=out_specs,
        out_shape=out_shape,
        scratch_shapes=[pltpu.VMEM((tm, d), F32)],
        compiler_params=pltpu.CompilerParams(vmem_limit_bytes=VMEM_LIMIT_BIG),
        name="mlp",
    )(*args)


def _relayout_kernel(wt_ref, o_ref):
    al, aw = P_SRC["a_low"]
    x = wt_ref[...]
    pad = jnp.zeros((LANE - aw, x.shape[1]), x.dtype)
    cols = jnp.concatenate([x[:al], x[al + aw:], x[al:al + aw], pad], axis=0)
    o_ref[...] = cols.T.astype(o_ref.dtype)


def _permute_w_in(w_in):
    depth, d, n = w_in.shape
    tc = 256
    return pl.pallas_call(
        _relayout_kernel,
        grid=(depth, d // tc),
        in_specs=[pl.BlockSpec((None, n, tc), lambda l, i: (l, 0, i))],
        out_specs=pl.BlockSpec((None, tc, P_WIDTH), lambda l, i: (l, i, 0)),
        out_shape=jax.ShapeDtypeStruct((depth, d, P_WIDTH), BF16),
        compiler_params=pltpu.CompilerParams(vmem_limit_bytes=VMEM_LIMIT_BIG),
        name="w_in_relayout",
    )(jnp.swapaxes(w_in, 1, 2))


def _gla_gate_weights(w_a2, b_a):
    n_pairs = GLA_HEADS // 2
    wg = jnp.zeros((2, n_pairs, LANE, 2 * GLA_DK), F32)
    for dd in range(2):
        blk = w_a2[dd].reshape(GLA_RANK, n_pairs, 2 * GLA_DK).transpose(1, 0, 2)
        wg = wg.at[dd, :, dd * GLA_RANK:(dd + 1) * GLA_RANK, :].set(blk)
    bg = b_a.reshape(2, n_pairs, 1, 2 * GLA_DK)
    return wg.astype(BF16), bg.astype(F32)


def _rope_tables(seq, n_ctx):
    quarter = GLA_DK // 4
    inv_freq = ROPE_BASE ** (-np.arange(quarter, dtype=np.float64) / quarter)
    pos = np.arange(seq)
    lane = np.arange(LANE)
    is_col = (lane % GLA_DK) // (GLA_DK // 2) == 1
    first = (lane % (GLA_DK // 2)) < quarter
    p = np.where(is_col[None, :], (pos % GRID_W)[:, None], (pos // GRID_W)[:, None]).astype(np.float64)
    ang = p * inv_freq[lane % quarter][None, :]
    cos = np.cos(ang)
    sin = np.where(first[None, :], -np.sin(ang), np.sin(ang))
    cos = np.concatenate([cos, np.ones((n_ctx, LANE))], axis=0)
    sin = np.concatenate([sin, np.zeros((n_ctx, LANE))], axis=0)
    return jnp.asarray(np.concatenate([cos, sin], axis=1), F32)


def kernel(x, c, ctx, c_ctx, w_mod, b_mod, attn_norm, w_in, gla_w_a2, gla_b_a, gla_norm, hg_lower_bounds, hg_norm, na_q_norm, na_k_norm, na_rpb, w_out, mlp_norm, w_mlp1, w_mlp2):
    batch, seq, d = x.shape
    n_ctx = ctx.shape[1]
    depth = w_mod.shape[0]
    assert seq % ROW_BLK == 0 and n_ctx == ROW_BLK and batch * n_ctx == TOK_BLK and d % LANE == 0
    assert seq % TOK_BLK == 0 and seq % GRID_W == 0
    n_lat_blk = seq // ROW_BLK
    lat_rows = batch * seq
    n_lat_tok = lat_rows // TOK_BLK

    tok = _Tokens(x.reshape(lat_rows, d), ctx.reshape(batch * n_ctx, d), n_lat_tok, 0)

    cc = jnp.zeros((8, d), F32).at[0:batch].set(c).at[batch].set(c_ctx)
    mod = _modulation(cc, w_mod, b_mod)
    modt = mod[:, 0:batch + 1].reshape(depth, batch + 1, 6, d)

    lb_p = jax.nn.softmax(hg_lower_bounds.astype(F32), axis=0)
    lower = jnp.cumsum(lb_p, axis=0) - lb_p[0]
    lower = lower.reshape(depth, 2, HG_HEADS // 2, 1, 2 * HEAD_DIM)
    rope_t = _rope_tables(seq, n_ctx)

    bias_tbl = _na_bias_table(na_rpb.reshape((depth * NA_HEADS,) + na_rpb.shape[2:]))
    w_in_b = _permute_w_in(w_in)

    h = _first_norm(tok, modt[0], attn_norm[0], seq)
    for l in range(depth):
        last = l == depth - 1
        p, w1_b, w2_b, w_out_b = _in_projection(h, w_in_b, w_mlp1, w_mlp2, w_out, l)
        wg, bg = _gla_gate_weights(gla_w_a2[l], gla_b_a[l])
        ya, yb = _gated_scans(p, rope_t, wg, bg, lower[l], gla_norm[l], hg_norm[l], batch, n_lat_blk)
        yc = _neighbourhood_attention(p, na_q_norm[l], na_k_norm[l], bias_tbl, l, batch, n_lat_blk)
        x1, h2 = _out_projection(tok, (ya, yb, yc), modt[l], mlp_norm[l], w_out_b, seq, with_ctx=not last)
        if last:
            (xu,) = _mlp(h2, x1, modt[l], w1_b, w2_b, seq, None)
        else:
            xu, h = _mlp(h2, x1, modt[l], w1_b, w2_b, seq, (attn_norm[l + 1], modt[l + 1]))
        tok = _Tokens(xu, xu, n_lat_tok, n_lat_tok)
    return xu.reshape(batch, seq, d)
```

```python
import functools

import numpy as np
import jax
import jax.numpy as jnp
from jax import lax
from jax.experimental import pallas as pl
from jax.experimental.pallas import tpu as pltpu

F32 = jnp.float32
BF16 = jnp.bfloat16

EPS = 1e-6
GRID_W = 64
HEAD_DIM = 128
GLA_HEADS = 4
GLA_DK = 64
GLA_RANK = 16
GLA_TAU = 16.0
HG_HEADS = 4
NA_HEADS = 8
WIN_ROWS = 8
WIN_COLS = 16
ROPE_BASE = 10000.0
NEG_BIG = -1e30
LOG2E = 1.4426950408889634

LANE = 128
ROW_BLK = 256
CHUNK = 128
N_LEVELS = 7
TOK_BLK = 512
MLP_TF = 1024
EPILOGUE_ROWS = 16
VMEM_LIMIT = 48 * 1024 * 1024
VMEM_LIMIT_BIG = 56 * 1024 * 1024

NA_Q_ROWS = ROW_BLK // GRID_W
NA_SPAN_ROWS = 12
NA_KEY_CHUNK = 256
NA_HEADS_PER_STEP = 2
SCAN_PAIRS_PER_STEP = 2


def _projection_layout():
    gw, hw, nw = GLA_HEADS * HEAD_DIM, HG_HEADS * HEAD_DIM, NA_HEADS * HEAD_DIM
    sections = (("gla_q", GLA_HEADS * GLA_DK), ("gla_k", GLA_HEADS * GLA_DK), ("gla_v", gw), ("gla_r", gw),
                ("a_low", 2 * GLA_RANK), ("hg_q", hw), ("hg_f", 2 * hw), ("hg_i", hw), ("hg_g", hw),
                ("na_q", nw), ("na_k", nw), ("na_v", nw))
    src, off = {}, 0
    for name, width in sections:
        src[name] = (off, width)
        off += width
    dst, off = {}, 0
    for name, width in sections:
        if name != "a_low":
            dst[name] = off
            off += width
    dst["a_low"] = off
    return src, dst, off + LANE


P_SRC, P_COLS, P_WIDTH = _projection_layout()


def _dot(a, b):
    return jnp.dot(a, b, preferred_element_type=F32)


def _dot_nt(a, b):
    return lax.dot_general(a, b, (((1,), (1,)), ((), ())), preferred_element_type=F32)


def _dot_tn(a, b):
    return lax.dot_general(a, b, (((0,), (0,)), ((), ())), preferred_element_type=F32)


def _sigmoid(x):
    return 1.0 / (1.0 + jnp.exp(-x))


def _silu(x):
    return x * _sigmoid(x)


def _split3(x):
    hi = x.astype(BF16)
    r1 = x - hi.astype(F32)
    mid = r1.astype(BF16)
    lo = (r1 - mid.astype(F32)).astype(BF16)
    return jnp.concatenate([hi, mid, lo], axis=-1)


def _sum3(y, w):
    return y[:, 0:w] + y[:, w:2 * w] + y[:, 2 * w:3 * w]


def _mod_kernel(c_ref, w_ref, b_ref, o_ref):
    s = _silu(c_ref[...]).astype(BF16)
    o_ref[...] = _dot(s, w_ref[...].astype(BF16)) + b_ref[...]


def _modulation(cc, w_mod, b_mod):
    depth, d, n = w_mod.shape
    tn = 1024
    return pl.pallas_call(
        _mod_kernel,
        grid=(depth, n // tn),
        in_specs=[
            pl.BlockSpec((8, d), lambda l, j: (0, 0)),
            pl.BlockSpec((None, d, tn), lambda l, j: (l, 0, j)),
            pl.BlockSpec((None, 1, tn), lambda l, j: (l, 0, j)),
        ],
        out_specs=pl.BlockSpec((None, 8, tn), lambda l, j: (l, 0, j)),
        out_shape=jax.ShapeDtypeStruct((depth, 8, n), F32),
        compiler_params=pltpu.CompilerParams(vmem_limit_bytes=VMEM_LIMIT),
        name="modulation",
    )(cc, w_mod, b_mod.reshape(depth, 1, n))


def _norm_mod(x, g, shift, scale):
    y = x * lax.rsqrt(jnp.mean(x * x, axis=-1, keepdims=True) + EPS) * g
    return y * (1.0 + scale) + shift


class _Tokens:
    def __init__(self, lat, ctx, n_lat, ctx_blk):
        self.lat, self.ctx, self.n_lat, self.ctx_blk = lat, ctx, n_lat, ctx_blk

    def specs(self, d):
        n_lat, ctx_blk = self.n_lat, self.ctx_blk
        return [pl.BlockSpec((TOK_BLK, d), lambda i, *_: (jnp.minimum(i, n_lat - 1), 0)),
                pl.BlockSpec((TOK_BLK, d), lambda i, *_: (ctx_blk, 0))]


def _first_norm_kernel(xl_ref, xc_ref, mod_ref, g_ref, h_ref, *, n_lat):
    i = pl.program_id(0)

    def emit(x_ref):
        for r0 in range(0, TOK_BLK, EPILOGUE_ROWS):
            rs = slice(r0, r0 + EPILOGUE_ROWS)
            h_ref[rs, :] = _norm_mod(x_ref[rs, :], g_ref[...], mod_ref[0:1, :], mod_ref[1:2, :]).astype(h_ref.dtype)

    pl.when(i < n_lat)(lambda: emit(xl_ref))
    pl.when(i >= n_lat)(lambda: emit(xc_ref))


def _first_norm(tok, modt, g, rows_per_mod):
    d = tok.lat.shape[1]
    per = rows_per_mod // TOK_BLK
    n_blk = tok.n_lat + 1
    return pl.pallas_call(
        functools.partial(_first_norm_kernel, n_lat=tok.n_lat),
        grid=(n_blk,),
        in_specs=tok.specs(d) + [
            pl.BlockSpec((None, 6, d), lambda i: (i // per, 0, 0)),
            pl.BlockSpec((1, d), lambda i: (0, 0)),
        ],
        out_specs=pl.BlockSpec((TOK_BLK, d), lambda i: (i, 0)),
        out_shape=jax.ShapeDtypeStruct((n_blk * TOK_BLK, d), BF16),
        compiler_params=pltpu.CompilerParams(vmem_limit_bytes=VMEM_LIMIT),
        name="first_norm",
    )(tok.lat, tok.ctx, modt, g.reshape(1, d))


def _inproj_kernel(h_ref, w_ref, w1_ref, w2_ref, wo_ref, o_ref, w1o_ref, w2o_ref, woo_ref, *, n_cast):
    o_ref[...] = _dot(h_ref[...], w_ref[...])

    @pl.when(pl.program_id(0) * pl.num_programs(1) + pl.program_id(1) < n_cast)
    def _():
        w1o_ref[...] = w1_ref[...].astype(w1o_ref.dtype)
        w2o_ref[...] = w2_ref[...].astype(w2o_ref.dtype)
        woo_ref[...] = wo_ref[...].astype(woo_ref.dtype)


def _in_projection(h, w, w1, w2, w_out, layer):
    m, d = h.shape
    n = w.shape[2]
    ff = w1.shape[2]
    d_out = w_out.shape[2]
    tn = n // 3
    n_i = m // TOK_BLK
    tc = 256
    n_cast = ff // tc
    to = w_out.shape[1] // n_cast
    per_slab = MLP_TF // tc
    assert n_cast <= 3 * n_i and w_out.shape[1] % n_cast == 0 and to % 16 == 0 and MLP_TF % tc == 0

    def cast_blk(j, i):
        return jnp.minimum(j * n_i + i, n_cast - 1)

    return pl.pallas_call(
        functools.partial(_inproj_kernel, n_cast=n_cast),
        grid=(n // tn, n_i),
        in_specs=[
            pl.BlockSpec((TOK_BLK, d), lambda j, i: (i, 0)),
            pl.BlockSpec((None, d, tn), lambda j, i: (layer, 0, j)),
            pl.BlockSpec((None, d, tc), lambda j, i: (layer, 0, cast_blk(j, i))),
            pl.BlockSpec((None, tc, d), lambda j, i: (layer, cast_blk(j, i), 0)),
            pl.BlockSpec((None, to, d_out), lambda j, i: (layer, cast_blk(j, i), 0)),
        ],
        out_specs=[
            pl.BlockSpec((TOK_BLK, tn), lambda j, i: (i, j)),
            pl.BlockSpec((None, d, tc), lambda j, i: (cast_blk(j, i) // per_slab, 0, cast_blk(j, i) % per_slab)),
            pl.BlockSpec((tc, d), lambda j, i: (cast_blk(j, i), 0)),
            pl.BlockSpec((to, d_out), lambda j, i: (cast_blk(j, i), 0)),
        ],
        out_shape=[
            jax.ShapeDtypeStruct((m, n), F32),
            jax.ShapeDtypeStruct((ff // MLP_TF, d, MLP_TF), BF16),
            jax.ShapeDtypeStruct((ff, d), BF16),
            jax.ShapeDtypeStruct(w_out.shape[1:], BF16),
        ],
        compiler_params=pltpu.CompilerParams(vmem_limit_bytes=VMEM_LIMIT_BIG),
        name="in_projection",
    )(h, w, w1, w2, w_out)


def _scan_constants(reverse):
    c = CHUNK
    i = np.arange(c)[:, None]
    j = np.arange(c)[None, :]
    tri = (j >= i) if reverse else (j <= i)
    mask = np.zeros((N_LEVELS + 1, c, c), np.float32)
    for l in range(N_LEVELS):
        level = ((i ^ j) >> l) == 1
        mask[l] = level & ((i < j) if reverse else (i > j))
    mask[N_LEVELS] = np.eye(c, dtype=np.float32)
    return jnp.asarray(tri, BF16), jnp.asarray(np.concatenate([mask, mask], axis=1), BF16)


def _level_exponents(g, cum, scr_ref, reverse):
    c = CHUNK
    scr_ref[...] = cum
    row = lax.broadcasted_iota(jnp.int32, (c, LANE), 0)
    up = pltpu.roll(g, c - 1, 0)
    dn = pltpu.roll(g, 1, 0)
    m2 = row & 1
    m4 = row & 3
    if reverse:
        e0 = jnp.where(m2 == 0, g, 0.0)
        e1 = jnp.where(m4 == 0, g + up, jnp.where(m4 == 1, g, jnp.where(m4 == 2, 0.0, dn)))
    else:
        e0 = jnp.where(m2 == 1, g, 0.0)
        e1 = jnp.where(m4 == 0, up, jnp.where(m4 == 1, 0.0, jnp.where(m4 == 2, g, g + dn)))
    out = [e0, e1]
    for l in range(2, N_LEVELS):
        s = 1 << l
        pieces = []
        for blk in range(c // (2 * s)):
            lo = blk * 2 * s
            ref = scr_ref[lo + (s if reverse else s - 1):lo + (s if reverse else s - 1) + 1, :]
            if s < 8:
                pieces.append(-jnp.abs(cum[lo:lo + 2 * s] - ref))
            else:
                first, second = cum[lo:lo + s], cum[lo + s:lo + 2 * s]
                pieces += [first - ref, ref - second] if reverse else [ref - first, second - ref]
        out.append(pieces[0] if len(pieces) == 1 else jnp.concatenate(pieces, axis=0))
    return out


def _pair_weights(qs, k, exps, mask_ref):
    n = len(qs) * CHUNK

    def lhs(w):
        parts = [q if w is None else q * w for q in qs]
        return parts[0] if len(parts) == 1 else jnp.concatenate(parts, axis=0)

    att = mask_ref[N_LEVELS, 0:n, :] * _dot_nt(lhs(None), k).astype(BF16)
    for l, e in enumerate(exps):
        w = jnp.exp2(e).astype(BF16)
        att = att + mask_ref[l, 0:n, :] * _dot_nt(lhs(w), k * w).astype(BF16)
    return att


def _readout(o, gate, g):
    y = o * lax.rsqrt(jnp.mean(o * o, axis=-1, keepdims=True) + EPS) * g
    return y * _silu(gate)


class _Cols:
    def __init__(self, ref, start):
        self.ref, self.start = ref, start

    def __getitem__(self, idx):
        rows, cols = idx
        return self.ref[rows, self.start + cols.start:self.start + cols.stop]


def _scan_kernel(*refs, reverse):
    p_ref, al_ref, cs_ref, wg_ref, bg_ref, lb_ref, tri_ref, mask_ref = refs[:8]
    if reverse:
        o_ref, scr_ref, sg_ref, sh_ref = refs[8:]
    else:
        orev_ref, gn_ref, hn_ref, ya_ref, yb_ref, scr_ref, sg_ref, sh_ref = refs[8:]
    c = CHUNK
    gq_ref, gk_ref, gv_ref, gr_ref, hq_ref, hi_ref, hg_ref = (
        _Cols(p_ref, P_COLS[name]) for name in ("gla_q", "gla_k", "gla_v", "gla_r", "hg_q", "hg_i", "hg_g"))
    hf_ref = _Cols(p_ref, P_COLS["hg_f"] + (HG_HEADS * HEAD_DIM if reverse else 0))
    cos_ref, sin_ref = _Cols(cs_ref, 0), _Cols(cs_ref, LANE)

    @pl.when(pl.program_id(1) == 0)
    def _():
        sg_ref[...] = jnp.zeros_like(sg_ref)
        sh_ref[...] = jnp.zeros_like(sh_ref)

    lane = lax.broadcasted_iota(jnp.int32, (c, LANE), 1)
    first_half = (lane % (GLA_DK // 2)) < GLA_DK // 4
    head_a = lane < GLA_DK
    lane_masks = (head_a, jnp.logical_not(head_a))
    state_mask = jnp.concatenate([head_a, jnp.logical_not(head_a)], axis=0)
    end_row = 0 if reverse else c - 1
    n_chunks = ROW_BLK // c
    offsets = [(n_chunks - 1 - ci) * c if reverse else ci * c for ci in range(n_chunks)]

    def rope(x, rows):
        swapped = jnp.where(first_half, pltpu.roll(x, LANE - GLA_DK // 4, 1), pltpu.roll(x, GLA_DK // 4, 1))
        return x * cos_ref[rows, 0:LANE] + swapped * sin_ref[rows, 0:LANE]

    chains = []
    for pr in range(SCAN_PAIRS_PER_STEP):
        gla = slice(pr * LANE, (pr + 1) * LANE)
        for off in offsets:
            rows = slice(off, off + c)
            q = rope(gq_ref[rows, gla], rows) * (GLA_DK ** -0.5)
            k = rope(gk_ref[rows, gla], rows)
            logit = _dot(al_ref[rows, :].astype(BF16), wg_ref[pr]) + bg_ref[pr]
            soft = jnp.log2(1.0 + jnp.exp2(jnp.abs(logit) * -LOG2E))
            g = (jnp.minimum(logit, 0.0) * LOG2E - soft) * (1.0 / GLA_TAU)
            chains.append((q, k, g))
            for hh in range(2):
                head = slice((2 * pr + hh) * LANE, (2 * pr + hh + 1) * LANE)
                lb = lb_ref[pr][:, hh * LANE:(hh + 1) * LANE]
                logit = hf_ref[rows, head]
                e = jnp.exp2(jnp.abs(logit) * -LOG2E)
                r = 1.0 / (1.0 + e)
                sig = jnp.where(logit >= 0, r, e * r)
                sig_neg = jnp.where(logit >= 0, e * r, r)
                g = jnp.log2(lb + (1.0 - lb) * sig)
                k = (1.0 - lb) * sig_neg
                q = _silu(hq_ref[rows, head])
                chains.append((q, k, g))

    g3 = jnp.concatenate([_split3(g) for (_, _, g) in chains], axis=-1)
    cum_all = _dot(tri_ref[...], g3)
    cums = [_sum3(cum_all[:, 3 * LANE * n:3 * LANE * (n + 1)], LANE) for n in range(len(chains))]

    atts = []
    for n, ((q, k, g), cum) in enumerate(zip(chains, cums)):
        exps = _level_exponents(g, cum, scr_ref.at[n], reverse)
        qs = [jnp.where(m, q, 0.0).astype(BF16) for m in lane_masks] if n % 3 == 0 else [q.astype(BF16)]
        atts.append(_pair_weights(qs, k.astype(BF16), exps, mask_ref))

    for pr in range(SCAN_PAIRS_PER_STEP):
        for ci, off in enumerate(offsets):
            rows = slice(off, off + c)
            n0 = (pr * n_chunks + ci) * 3
            (q, k, _), cum, att = chains[n0], cums[n0], atts[n0]
            v = gv_ref[rows, pr * 2 * LANE:(pr + 1) * 2 * LANE].astype(BF16)
            cum_end = cum[end_row:end_row + 1, :]
            st = sg_ref[pr]
            inter = _dot_nt((q * jnp.exp2(cum)).astype(BF16), st.astype(BF16))
            o_gla = jnp.concatenate([_dot(att[0:c], v[:, 0:LANE]), _dot(att[c:2 * c], v[:, LANE:2 * LANE])],
                                    axis=-1) + inter
            kd = (k * jnp.exp2(cum_end - cum)).astype(BF16)
            sg_ref[pr] = st * jnp.exp2(cum_end) + jnp.where(state_mask, _dot_tn(v, kd), 0.0)
            o_hg = []
            for hh in range(2):
                head = slice((2 * pr + hh) * LANE, (2 * pr + hh + 1) * LANE)
                (q, k, _), cum, att = chains[n0 + 1 + hh], cums[n0 + 1 + hh], atts[n0 + 1 + hh]
                v = hi_ref[rows, head].astype(BF16)
                cum_end = cum[end_row:end_row + 1, :]
                st = sh_ref[2 * pr + hh]
                o_hg.append(_dot(att, v) + _dot_nt((q * jnp.exp2(cum)).astype(BF16), st.astype(BF16)))
                kd = (k * jnp.exp2(cum_end - cum)).astype(BF16)
                sh_ref[2 * pr + hh] = st * jnp.exp2(cum_end) + _dot_tn(v, kd)
            o_all = jnp.concatenate([o_gla] + o_hg, axis=-1)
            out_cols = slice(pr * 4 * LANE, (pr + 1) * 4 * LANE)
            if reverse:
                o_ref[rows, out_cols] = o_all
            else:
                o_all = o_all + orev_ref[rows, out_cols]
                for hh in range(2):
                    head = slice((2 * pr + hh) * LANE, (2 * pr + hh + 1) * LANE)
                    y = _readout(o_all[:, hh * LANE:(hh + 1) * LANE], gr_ref[rows, head], gn_ref[...])
                    ya_ref[rows, head] = y.astype(ya_ref.dtype)
                    y = _readout(o_all[:, (2 + hh) * LANE:(3 + hh) * LANE], hg_ref[rows, head], hn_ref[...])
                    yb_ref[rows, head] = y.astype(yb_ref.dtype)


def _gated_scans(p, rope_t, wg, bg, lb, gla_norm, hg_norm, batch, n_lat_blk):
    assert CHUNK == HEAD_DIM == LANE and GLA_HEADS == HG_HEADS
    assert GLA_HEADS == 2 * SCAN_PAIRS_PER_STEP
    m = p.shape[0]
    pps = SCAN_PAIRS_PER_STEP
    n_blk = n_lat_blk + 1
    scan_cols = P_COLS["na_q"]
    assert all(P_COLS[s] < scan_cols for s in ("gla_q", "gla_k", "gla_v", "gla_r", "hg_q", "hg_f", "hg_i", "hg_g"))

    def run(reverse, extra_in, extra_specs, out_shape, out_specs):
        def pos(t):
            lat = (n_lat_blk - t) if reverse else (t - 1)
            return jnp.where(t == 0, n_lat_blk, lat)

        def row_blk(b, t):
            return jnp.where(t == 0, batch * n_lat_blk + b, b * n_lat_blk + pos(t))

        def rows(width, col_blk=0):
            return pl.BlockSpec((ROW_BLK, width), lambda b, t: (row_blk(b, t), col_blk))

        def const(shape):
            return pl.BlockSpec(shape, lambda b, t: (0,) * len(shape))

        d = 1 if reverse else 0
        tri, mask = _scan_constants(reverse)
        in_specs = [
            rows(scan_cols),
            rows(LANE, P_COLS["a_low"] // LANE),
            pl.BlockSpec((ROW_BLK, 2 * LANE), lambda b, t: (pos(t), 0)),
            pl.BlockSpec((None, pps, LANE, LANE), lambda b, t: (d, 0, 0, 0)),
            pl.BlockSpec((None, pps, 1, LANE), lambda b, t: (d, 0, 0, 0)),
            pl.BlockSpec((None, pps, 1, 2 * LANE), lambda b, t: (d, 0, 0, 0)),
            const((CHUNK, CHUNK)),
            const((N_LEVELS + 1, 2 * CHUNK, CHUNK)),
        ] + [s(rows, const) for s in extra_specs]
        return pl.pallas_call(
            functools.partial(_scan_kernel, reverse=reverse),
            grid=(batch, n_blk),
            in_specs=in_specs,
            out_specs=[s(rows, const) for s in out_specs],
            out_shape=out_shape,
            scratch_shapes=[
                pltpu.VMEM((pps * 3 * ROW_BLK // CHUNK, CHUNK, LANE), F32),
                pltpu.VMEM((pps, 2 * HEAD_DIM, LANE), F32),
                pltpu.VMEM((pps * 2, HEAD_DIM, HEAD_DIM), F32),
            ],
            compiler_params=pltpu.CompilerParams(vmem_limit_bytes=VMEM_LIMIT),
            name="gated_scan_rev" if reverse else "gated_scan_fwd",
        )(p, p, rope_t, wg, bg, lb, tri, mask, *extra_in)

    def rows_spec(width):
        return lambda rows, const: rows(width)

    def const_spec(rows, const):
        return const((1, LANE))

    n_heads = GLA_HEADS + HG_HEADS
    (o_rev,) = run(True, (), (), [jax.ShapeDtypeStruct((m, n_heads * HEAD_DIM), F32)],
                   [rows_spec(n_heads * HEAD_DIM)])
    return run(False, (o_rev, gla_norm.reshape(1, -1), hg_norm.reshape(1, -1)),
               (rows_spec(n_heads * HEAD_DIM), const_spec, const_spec),
               [jax.ShapeDtypeStruct((m, GLA_HEADS * HEAD_DIM), BF16),
                jax.ShapeDtypeStruct((m, HG_HEADS * HEAD_DIM), BF16)],
               [rows_spec(GLA_HEADS * HEAD_DIM), rows_spec(HG_HEADS * HEAD_DIM)])


def _head_norm(x, g):
    return x * lax.rsqrt(jnp.mean(x * x, axis=-1, keepdims=True) + EPS) * g


def _na_span_base(n, n_rows, xp=jnp):
    return xp.clip(n * NA_Q_ROWS - WIN_ROWS // 2, 0, n_rows - NA_SPAN_ROWS)


def _na_variants(n_rows):
    n_blocks = n_rows // NA_Q_ROWS
    assert n_rows % NA_Q_ROWS == 0 and n_blocks >= 4

    def rows(n):
        base = int(_na_span_base(n, n_rows, np))
        out = []
        for i in range(NA_Q_ROWS):
            r = n * NA_Q_ROWS + i
            r0 = min(max(r - WIN_ROWS // 2, 0), n_rows - WIN_ROWS)
            out.append([(base + kk - r + WIN_ROWS - 1, r0 <= base + kk < r0 + WIN_ROWS)
                        for kk in range(NA_SPAN_ROWS)])
        return out

    variants = [rows(0), rows(1), rows(n_blocks - 1)]
    assert all(rows(n) == variants[1] for n in range(1, n_blocks - 1))
    return variants


def _na_kernel(q_ref, kl_ref, vl_ref, kc_ref, vc_ref, qn_ref, kn_ref, tbl_ref, y_ref,
               kls_ref, vls_ref, kcs_ref, vcs_ref, bias_ref, *, n_rows, steps_per_batch, batch):
    t = pl.program_id(1)
    heads = [slice(hd * HEAD_DIM, (hd + 1) * HEAD_DIM) for hd in range(NA_HEADS_PER_STEP)]

    @pl.when((t > 0) & ((t - 1) % steps_per_batch == 0))
    def _():
        for cols in heads:
            kls_ref[:, cols] = _head_norm(kl_ref[:, cols], kn_ref[...]).astype(BF16)
        vls_ref[...] = vl_ref[...].astype(BF16)

    @pl.when(t == 0)
    def _():
        for cols in heads:
            kcs_ref[:, cols] = _head_norm(kc_ref[:, cols], kn_ref[...]).astype(BF16)
        vcs_ref[...] = vc_ref[...].astype(BF16)
        n_dr = tbl_ref.shape[1] - 1
        for hd in range(NA_HEADS_PER_STEP):
            for v, var in enumerate(_na_variants(n_rows)):
                for i, row in enumerate(var):
                    for g in range(NA_SPAN_ROWS // 2):
                        pair = [tbl_ref[hd, dr if ok else n_dr] for dr, ok in row[2 * g:2 * g + 2]]
                        assert all(0 <= dr < n_dr for dr, ok in row[2 * g:2 * g + 2] if ok)
                        bias_ref[hd, v, i * GRID_W:(i + 1) * GRID_W, g * LANE:(g + 1) * LANE] = (
                            jnp.concatenate(pair, axis=1))

    qs = [_head_norm(q_ref[:, cols], qn_ref[...] * (HEAD_DIM ** -0.5 * LOG2E)).astype(BF16) for cols in heads]

    @pl.when(t == 0)
    def _():
        for q, cols in zip(qs, heads):
            for b in range(batch):
                rows = slice(b * ROW_BLK, (b + 1) * ROW_BLK)
                s = _dot_nt(q[rows], kcs_ref[rows, cols])
                e = jnp.exp2(s - jnp.max(s, axis=-1, keepdims=True))
                o = _dot(e.astype(BF16), vcs_ref[rows, cols]) / jnp.sum(e, axis=-1, keepdims=True)
                y_ref[rows, cols] = o.astype(y_ref.dtype)

    @pl.when(t > 0)
    def _():
        ctx_rows = pl.ds(pl.multiple_of(((t - 1) // steps_per_batch) * ROW_BLK, ROW_BLK), ROW_BLK)
        for sub in range(TOK_BLK // ROW_BLK):
            rows = slice(sub * ROW_BLK, (sub + 1) * ROW_BLK)
            n = ((t - 1) % steps_per_batch) * (TOK_BLK // ROW_BLK) + sub
            base = _na_span_base(n, n_rows)
            variant = jnp.where(n == 0, 0, jnp.where(n == n_rows // NA_Q_ROWS - 1, 2, 1))
            for hd, (q, cols) in enumerate(zip(qs, heads)):
                qr = q[rows]
                s = _dot_nt(qr, kcs_ref[ctx_rows, cols])
                m = jnp.max(s, axis=-1, keepdims=True)
                e = jnp.exp2(s - m)
                den = jnp.sum(e, axis=-1, keepdims=True)
                o = _dot(e.astype(BF16), vcs_ref[ctx_rows, cols])
                for g in range(NA_SPAN_ROWS * GRID_W // NA_KEY_CHUNK):
                    keys = pl.ds(pl.multiple_of(base * GRID_W + g * NA_KEY_CHUNK, GRID_W), NA_KEY_CHUNK)
                    s = (_dot_nt(qr, kls_ref[keys, cols])
                         + bias_ref[hd, variant, :, g * NA_KEY_CHUNK:(g + 1) * NA_KEY_CHUNK])
                    m_new = jnp.maximum(m, jnp.max(s, axis=-1, keepdims=True))
                    alpha = jnp.exp2(m - m_new)
                    e = jnp.exp2(s - m_new)
                    den = alpha * den + jnp.sum(e, axis=-1, keepdims=True)
                    o = alpha * o + _dot(e.astype(BF16), vls_ref[keys, cols])
                    m = m_new
                y_ref[rows, cols] = (o / den).astype(y_ref.dtype)


def _na_bias_table(rpb):
    n_heads, n_dr, n_dc = rpb.shape
    c = np.arange(GRID_W)[:, None]
    kc = np.arange(GRID_W)[None, :]
    c0 = np.clip(c - WIN_COLS // 2, 0, GRID_W - WIN_COLS)
    col_valid = (kc >= c0) & (kc < c0 + WIN_COLS)
    pick_dc = (np.arange(n_dc)[:, None, None] == (kc - c + WIN_COLS - 1)[None]) & col_valid[None]
    col = jnp.einsum("hrb,bcq->hrcq", rpb.astype(F32), jnp.asarray(pick_dc, F32), precision=lax.Precision.HIGHEST)
    col = col + jnp.asarray(np.where(col_valid, 0.0, NEG_BIG), F32)
    return jnp.concatenate([col, jnp.full((n_heads, 1, GRID_W, GRID_W), NEG_BIG, F32)], axis=1) * LOG2E


def _neighbourhood_attention(p, q_norm, k_norm, bias_tbl, layer, batch, n_lat_blk):
    m = p.shape[0]
    lat_rows = n_lat_blk * ROW_BLK
    assert lat_rows % TOK_BLK == 0 and batch * ROW_BLK == TOK_BLK
    steps_per_batch = lat_rows // TOK_BLK
    width = NA_HEADS_PER_STEP * HEAD_DIM
    qb, kb, vb = (P_COLS[name] // width for name in ("na_q", "na_k", "na_v"))
    assert NA_HEADS % NA_HEADS_PER_STEP == 0 and all(P_COLS[name] % width == 0 for name in ("na_q", "na_k", "na_v"))
    ctx_blk = batch * steps_per_batch
    groups = NA_HEADS // NA_HEADS_PER_STEP

    def q_blk(t):
        return jnp.where(t == 0, ctx_blk, t - 1)

    def kv_blk(t):
        return jnp.maximum(t - 1, 0) // steps_per_batch

    kern = functools.partial(_na_kernel, n_rows=lat_rows // GRID_W, steps_per_batch=steps_per_batch, batch=batch)
    return pl.pallas_call(
        kern,
        grid=(groups, 1 + batch * steps_per_batch),
        in_specs=[
            pl.BlockSpec((TOK_BLK, width), lambda h, t: (q_blk(t), qb + h)),
            pl.BlockSpec((lat_rows, width), lambda h, t: (kv_blk(t), kb + h)),
            pl.BlockSpec((lat_rows, width), lambda h, t: (kv_blk(t), vb + h)),
            pl.BlockSpec((TOK_BLK, width), lambda h, t: (ctx_blk, kb + h)),
            pl.BlockSpec((TOK_BLK, width), lambda h, t: (ctx_blk, vb + h)),
            pl.BlockSpec((1, LANE), lambda h, t: (0, 0)),
            pl.BlockSpec((1, LANE), lambda h, t: (0, 0)),
            pl.BlockSpec((NA_HEADS_PER_STEP,) + bias_tbl.shape[1:], lambda h, t: (layer * groups + h, 0, 0, 0)),
        ],
        out_specs=pl.BlockSpec((TOK_BLK, width), lambda h, t: (q_blk(t), h)),
        out_shape=jax.ShapeDtypeStruct((m, NA_HEADS * HEAD_DIM), BF16),
        scratch_shapes=[
            pltpu.VMEM((lat_rows, width), BF16),
            pltpu.VMEM((lat_rows, width), BF16),
            pltpu.VMEM((TOK_BLK, width), BF16),
            pltpu.VMEM((TOK_BLK, width), BF16),
            pltpu.VMEM((NA_HEADS_PER_STEP, 3, ROW_BLK, NA_SPAN_ROWS * GRID_W), F32),
        ],
        compiler_params=pltpu.CompilerParams(vmem_limit_bytes=VMEM_LIMIT),
        name="neighbourhood_attention",
    )(p, p, p, p, p, q_norm.reshape(1, -1), k_norm.reshape(1, -1), bias_tbl)


def _outproj_kernel(*refs, n_lat, n_y):
    xl_ref, xc_ref = refs[:2]
    y_refs = refs[2:2 + n_y]
    mod_ref, g_ref = refs[2 + n_y:4 + n_y]
    w_refs = refs[4 + n_y:4 + 2 * n_y]
    o_ref, h_ref = refs[4 + 2 * n_y:]
    i = pl.program_id(0)
    acc = _dot(y_refs[0][...], w_refs[0][...])
    for y_ref, w_ref in zip(y_refs[1:], w_refs[1:]):
        acc = acc + _dot(y_ref[...], w_ref[...])
    upd = mod_ref[2:3, :] * acc

    def emit(x_ref):
        for r0 in range(0, TOK_BLK, EPILOGUE_ROWS):
            rs = slice(r0, r0 + EPILOGUE_ROWS)
            x1 = x_ref[rs, :] + upd[rs]
            o_ref[rs, :] = x1
            h_ref[rs, :] = _norm_mod(x1, g_ref[...], mod_ref[3:4, :], mod_ref[4:5, :]).astype(h_ref.dtype)

    pl.when(i < n_lat)(lambda: emit(xl_ref))
    pl.when(i >= n_lat)(lambda: emit(xc_ref))


def _out_projection(tok, ys, modt, g_mlp, w_out, rows_per_mod, with_ctx):
    d = w_out.shape[1]
    tm = TOK_BLK
    per = rows_per_mod // tm
    n_blk = tok.n_lat + (1 if with_ctx else 0)
    y_specs, w_specs, row0 = [], [], 0
    for y in ys:
        width = y.shape[1]
        assert row0 % width == 0
        y_specs.append(pl.BlockSpec((tm, width), lambda i: (i, 0)))
        w_specs.append(pl.BlockSpec((width, d), lambda i, blk=row0 // width: (blk, 0)))
        row0 += width
    assert row0 == w_out.shape[0]
    return pl.pallas_call(
        functools.partial(_outproj_kernel, n_lat=tok.n_lat, n_y=len(ys)),
        grid=(n_blk,),
        in_specs=tok.specs(d) + y_specs + [
            pl.BlockSpec((None, 6, d), lambda i: (i // per, 0, 0)),
            pl.BlockSpec((1, d), lambda i: (0, 0)),
        ] + w_specs,
        out_specs=[pl.BlockSpec((tm, d), lambda i: (i, 0)), pl.BlockSpec((tm, d), lambda i: (i, 0))],
        out_shape=[jax.ShapeDtypeStruct((n_blk * tm, d), F32), jax.ShapeDtypeStruct((n_blk * tm, d), BF16)],
        compiler_params=pltpu.CompilerParams(vmem_limit_bytes=VMEM_LIMIT),
        name="out_projection",
    )(tok.lat, tok.ctx, *ys, modt, g_mlp.reshape(1, d), *([w_out] * len(ys)))


def _mlp_kernel(*refs, emit_next):
    if emit_next:
        h_ref, x_ref, mod_ref, w1_ref, w2_ref, gn_ref, modn_ref, o_ref, hn_ref, acc_ref = refs
    else:
        h_ref, x_ref, mod_ref, w1_ref, w2_ref, o_ref, acc_ref = refs
    j = pl.program_id(1)

    @pl.when(j == 0)
    def _():
        acc_ref[...] = jnp.zeros_like(acc_ref)

    a = jnp.maximum(_dot(h_ref[...], w1_ref[...]), 0.0)
    acc_ref[...] += _dot((a * a).astype(BF16), w2_ref[...])

    @pl.when(j == pl.num_programs(1) - 1)
    def _():
        for r0 in range(0, TOK_BLK, EPILOGUE_ROWS):
            rs = slice(r0, r0 + EPILOGUE_ROWS)
            x2 = x_ref[rs, :] + mod_ref[5:6, :] * acc_ref[rs, :]
            o_ref[rs, :] = x2
            if emit_next:
                hn_ref[rs, :] = _norm_mod(x2, gn_ref[...], modn_ref[0:1, :], modn_ref[1:2, :]).astype(hn_ref.dtype)


def _mlp(h, x1, modt, w1, w2, rows_per_mod, next_norm):
    m, d = x1.shape
    n_j, _, tf = w1.shape
    tm = TOK_BLK
    per = rows_per_mod // tm
    emit_next = next_norm is not None

    def rows(i, j):
        return (i, 0)

    def mod_rows(i, j):
        return (i // per, 0, 0)

    in_specs = [
        pl.BlockSpec((tm, d), rows),
        pl.BlockSpec((tm, d), rows),
        pl.BlockSpec((None, 6, d), mod_rows),
        pl.BlockSpec((None, d, tf), lambda i, j: (j, 0, 0)),
        pl.BlockSpec((tf, d), lambda i, j: (j, 0)),
    ]
    args = [h, x1, modt, w1, w2]
    out_specs = [pl.BlockSpec((tm, d), rows)]
    out_shape = [jax.ShapeDtypeStruct((m, d), F32)]
    if emit_next:
        g_next, modt_next = next_norm
        in_specs += [pl.BlockSpec((1, d), lambda i, j: (0, 0)), pl.BlockSpec((None, 6, d), mod_rows)]
        args += [g_next.reshape(1, d), modt_next]
        out_specs.append(pl.BlockSpec((tm, d), rows))
        out_shape.append(jax.ShapeDtypeStruct((m, d), BF16))
    return pl.pallas_call(
        functools.partial(_mlp_kernel, emit_next=emit_next),
        grid=(m // tm, n_j),
        in_specs=in_specs,
        out_specs=out_specs,
        out_shape=out_shape,
        scratch_shapes=[pltpu.VMEM((tm, d), F32)],
        compiler_params=pltpu.CompilerParams(vmem_limit_bytes=VMEM_LIMIT_BIG),
        name="mlp",
    )(*args)


def _relayout_kernel(wt_ref, o_ref):
    al, aw = P_SRC["a_low"]
    x = wt_ref[...]
    pad = jnp.zeros((LANE - aw, x.shape[1]), x.dtype)
    cols = jnp.concatenate([x[:al], x[al + aw:], x[al:al + aw], pad], axis=0)
    o_ref[...] = cols.T.astype(o_ref.dtype)


def _permute_w_in(w_in):
    depth, d, n = w_in.shape
    tc = 256
    return pl.pallas_call(
        _relayout_kernel,
        grid=(depth, d // tc),
        in_specs=[pl.BlockSpec((None, n, tc), lambda l, i: (l, 0, i))],
        out_specs=pl.BlockSpec((None, tc, P_WIDTH), lambda l, i: (l, i, 0)),
        out_shape=jax.ShapeDtypeStruct((depth, d, P_WIDTH), BF16),
        compiler_params=pltpu.CompilerParams(vmem_limit_bytes=VMEM_LIMIT_BIG),
        name="w_in_relayout",
    )(jnp.swapaxes(w_in, 1, 2))


def _gla_gate_weights(w_a2, b_a):
    n_pairs = GLA_HEADS // 2
    wg = jnp.zeros((2, n_pairs, LANE, 2 * GLA_DK), F32)
    for dd in range(2):
        blk = w_a2[dd].reshape(GLA_RANK, n_pairs, 2 * GLA_DK).transpose(1, 0, 2)
        wg = wg.at[dd, :, dd * GLA_RANK:(dd + 1) * GLA_RANK, :].set(blk)
    bg = b_a.reshape(2, n_pairs, 1, 2 * GLA_DK)
    return wg.astype(BF16), bg.astype(F32)


def _rope_tables(seq, n_ctx):
    quarter = GLA_DK // 4
    inv_freq = ROPE_BASE ** (-np.arange(quarter, dtype=np.float64) / quarter)
    pos = np.arange(seq)
    lane = np.arange(LANE)
    is_col = (lane % GLA_DK) // (GLA_DK // 2) == 1
    first = (lane % (GLA_DK // 2)) < quarter
    p = np.where(is_col[None, :], (pos % GRID_W)[:, None], (pos // GRID_W)[:, None]).astype(np.float64)
    ang = p * inv_freq[lane % quarter][None, :]
    cos = np.cos(ang)
    sin = np.where(first[None, :], -np.sin(ang), np.sin(ang))
    cos = np.concatenate([cos, np.ones((n_ctx, LANE))], axis=0)
    sin = np.concatenate([sin, np.zeros((n_ctx, LANE))], axis=0)
    return jnp.asarray(np.concatenate([cos, sin], axis=1), F32)


def kernel(x, c, ctx, c_ctx, w_mod, b_mod, attn_norm, w_in, gla_w_a2, gla_b_a, gla_norm, hg_lower_bounds, hg_norm, na_q_norm, na_k_norm, na_rpb, w_out, mlp_norm, w_mlp1, w_mlp2):
    batch, seq, d = x.shape
    n_ctx = ctx.shape[1]
    depth = w_mod.shape[0]
    assert seq % ROW_BLK == 0 and n_ctx == ROW_BLK and batch * n_ctx == TOK_BLK and d % LANE == 0
    assert seq % TOK_BLK == 0 and seq % GRID_W == 0
    n_lat_blk = seq // ROW_BLK
    lat_rows = batch * seq
    n_lat_tok = lat_rows // TOK_BLK

    tok = _Tokens(x.reshape(lat_rows, d), ctx.reshape(batch * n_ctx, d), n_lat_tok, 0)

    cc = jnp.zeros((8, d), F32).at[0:batch].set(c).at[batch].set(c_ctx)
    mod = _modulation(cc, w_mod, b_mod)
    modt = mod[:, 0:batch + 1].reshape(depth, batch + 1, 6, d)

    lb_p = jax.nn.softmax(hg_lower_bounds.astype(F32), axis=0)
    lower = jnp.cumsum(lb_p, axis=0) - lb_p[0]
    lower = lower.reshape(depth, 2, HG_HEADS // 2, 1, 2 * HEAD_DIM)
    rope_t = _rope_tables(seq, n_ctx)

    bias_tbl = _na_bias_table(na_rpb.reshape((depth * NA_HEADS,) + na_rpb.shape[2:]))
    w_in_b = _permute_w_in(w_in)

    h = _first_norm(tok, modt[0], attn_norm[0], seq)
    for l in range(depth):
        last = l == depth - 1
        p, w1_b, w2_b, w_out_b = _in_projection(h, w_in_b, w_mlp1, w_mlp2, w_out, l)
        wg, bg = _gla_gate_weights(gla_w_a2[l], gla_b_a[l])
        ya, yb = _gated_scans(p, rope_t, wg, bg, lower[l], gla_norm[l], hg_norm[l], batch, n_lat_blk)
        yc = _neighbourhood_attention(p, na_q_norm[l], na_k_norm[l], bias_tbl, l, batch, n_lat_blk)
        x1, h2 = _out_projection(tok, (ya, yb, yc), modt[l], mlp_norm[l], w_out_b, seq, with_ctx=not last)
        if last:
            (xu,) = _mlp(h2, x1, modt[l], w1_b, w2_b, seq, None)
        else:
            xu, h = _mlp(h2, x1, modt[l], w1_b, w2_b, seq, (attn_norm[l + 1], modt[l + 1]))
        tok = _Tokens(xu, xu, n_lat_tok, n_lat_tok)
    return xu.reshape(batch, seq, d)
```

```python
import functools

import numpy as np
import jax
import jax.numpy as jnp
from jax import lax
from jax.experimental import pallas as pl
from jax.experimental.pallas import tpu as pltpu

F32 = jnp.float32
BF16 = jnp.bfloat16

EPS = 1e-6
GRID_W = 64
HEAD_DIM = 128
GLA_HEADS = 4
GLA_DK = 64
GLA_RANK = 16
GLA_TAU = 16.0
HG_HEADS = 4
NA_HEADS = 8
WIN_ROWS = 8
WIN_COLS = 16
ROPE_BASE = 10000.0
NEG_BIG = -1e30
LOG2E = 1.4426950408889634

LANE = 128
ROW_BLK = 256
CHUNK = 128
N_LEVELS = 7
TOK_BLK = 512
MLP_TF = 1024
VMEM_LIMIT = 48 * 1024 * 1024
VMEM_LIMIT_BIG = 56 * 1024 * 1024

NA_Q_ROWS = ROW_BLK // GRID_W
NA_SPAN_ROWS = 12
NA_KEY_CHUNK = 256
NA_HEADS_PER_STEP = 2
SCAN_PAIRS_PER_STEP = 2


def _projection_layout():
    gw, hw, nw = GLA_HEADS * HEAD_DIM, HG_HEADS * HEAD_DIM, NA_HEADS * HEAD_DIM
    sections = (("gla_q", GLA_HEADS * GLA_DK), ("gla_k", GLA_HEADS * GLA_DK), ("gla_v", gw), ("gla_r", gw),
                ("a_low", 2 * GLA_RANK), ("hg_q", hw), ("hg_f", 2 * hw), ("hg_i", hw), ("hg_g", hw),
                ("na_q", nw), ("na_k", nw), ("na_v", nw))
    src, off = {}, 0
    for name, width in sections:
        src[name] = (off, width)
        off += width
    dst, off = {}, 0
    for name, width in sections:
        if name != "a_low":
            dst[name] = off
            off += width
    dst["a_low"] = off
    return src, dst, off + LANE


P_SRC, P_COLS, P_WIDTH = _projection_layout()


def _dot(a, b):
    return jnp.dot(a, b, preferred_element_type=F32)


def _dot_nt(a, b):
    return lax.dot_general(a, b, (((1,), (1,)), ((), ())), preferred_element_type=F32)


def _dot_tn(a, b):
    return lax.dot_general(a, b, (((0,), (0,)), ((), ())), preferred_element_type=F32)


def _sigmoid(x):
    return 1.0 / (1.0 + jnp.exp(-x))


def _silu(x):
    return x * _sigmoid(x)


def _split3(x):
    hi = x.astype(BF16)
    r1 = x - hi.astype(F32)
    mid = r1.astype(BF16)
    lo = (r1 - mid.astype(F32)).astype(BF16)
    return jnp.concatenate([hi, mid, lo], axis=-1)


def _sum3(y, w):
    return y[:, 0:w] + y[:, w:2 * w] + y[:, 2 * w:3 * w]


def _mod_kernel(c_ref, w_ref, b_ref, o_ref):
    s = _silu(c_ref[...]).astype(BF16)
    o_ref[...] = _dot(s, w_ref[...].astype(BF16)) + b_ref[...]


def _modulation(cc, w_mod, b_mod):
    depth, d, n = w_mod.shape
    tn = 1024
    return pl.pallas_call(
        _mod_kernel,
        grid=(depth, n // tn),
        in_specs=[
            pl.BlockSpec((8, d), lambda l, j: (0, 0)),
            pl.BlockSpec((None, d, tn), lambda l, j: (l, 0, j)),
            pl.BlockSpec((None, 1, tn), lambda l, j: (l, 0, j)),
        ],
        out_specs=pl.BlockSpec((None, 8, tn), lambda l, j: (l, 0, j)),
        out_shape=jax.ShapeDtypeStruct((depth, 8, n), F32),
        compiler_params=pltpu.CompilerParams(vmem_limit_bytes=VMEM_LIMIT),
        name="modulation",
    )(cc, w_mod, b_mod.reshape(depth, 1, n))


def _norm_mod(x, g, shift, scale):
    y = x * lax.rsqrt(jnp.mean(x * x, axis=-1, keepdims=True) + EPS) * g
    return y * (1.0 + scale) + shift


class _Tokens:
    def __init__(self, lat, ctx, n_lat, ctx_blk):
        self.lat, self.ctx, self.n_lat, self.ctx_blk = lat, ctx, n_lat, ctx_blk

    def specs(self, d):
        n_lat, ctx_blk = self.n_lat, self.ctx_blk
        return [pl.BlockSpec((TOK_BLK, d), lambda i, *_: (jnp.minimum(i, n_lat - 1), 0)),
                pl.BlockSpec((TOK_BLK, d), lambda i, *_: (ctx_blk, 0))]


def _first_norm_kernel(xl_ref, xc_ref, mod_ref, g_ref, h_ref, *, n_lat):
    i = pl.program_id(0)

    def emit(x_ref):
        h_ref[...] = _norm_mod(x_ref[...], g_ref[...], mod_ref[0:1, :], mod_ref[1:2, :]).astype(h_ref.dtype)

    pl.when(i < n_lat)(lambda: emit(xl_ref))
    pl.when(i >= n_lat)(lambda: emit(xc_ref))


def _first_norm(tok, modt, g, rows_per_mod):
    d = tok.lat.shape[1]
    per = rows_per_mod // TOK_BLK
    n_blk = tok.n_lat + 1
    return pl.pallas_call(
        functools.partial(_first_norm_kernel, n_lat=tok.n_lat),
        grid=(n_blk,),
        in_specs=tok.specs(d) + [
            pl.BlockSpec((None, 6, d), lambda i: (i // per, 0, 0)),
            pl.BlockSpec((1, d), lambda i: (0, 0)),
        ],
        out_specs=pl.BlockSpec((TOK_BLK, d), lambda i: (i, 0)),
        out_shape=jax.ShapeDtypeStruct((n_blk * TOK_BLK, d), BF16),
        compiler_params=pltpu.CompilerParams(vmem_limit_bytes=VMEM_LIMIT),
        name="first_norm",
    )(tok.lat, tok.ctx, modt, g.reshape(1, d))


def _inproj_kernel(h_ref, w_ref, w1_ref, w2_ref, wo_ref, o_ref, w1o_ref, w2o_ref, woo_ref, *, n_cast):
    o_ref[...] = _dot(h_ref[...], w_ref[...])

    @pl.when(pl.program_id(0) * pl.num_programs(1) + pl.program_id(1) < n_cast)
    def _():
        w1o_ref[...] = w1_ref[...].astype(w1o_ref.dtype)
        w2o_ref[...] = w2_ref[...].astype(w2o_ref.dtype)
        woo_ref[...] = wo_ref[...].astype(woo_ref.dtype)


def _in_projection(h, w, w1, w2, w_out, layer):
    m, d = h.shape
    n = w.shape[2]
    ff = w1.shape[2]
    d_out = w_out.shape[2]
    tn = n // 3
    n_i = m // TOK_BLK
    tc = 256
    n_cast = ff // tc
    to = w_out.shape[1] // n_cast
    per_slab = MLP_TF // tc
    assert n_cast <= 3 * n_i and w_out.shape[1] % n_cast == 0 and to % 16 == 0 and MLP_TF % tc == 0

    def cast_blk(j, i):
        return jnp.minimum(j * n_i + i, n_cast - 1)

    return pl.pallas_call(
        functools.partial(_inproj_kernel, n_cast=n_cast),
        grid=(n // tn, n_i),
        in_specs=[
            pl.BlockSpec((TOK_BLK, d), lambda j, i: (i, 0)),
            pl.BlockSpec((None, d, tn), lambda j, i: (layer, 0, j)),
            pl.BlockSpec((None, d, tc), lambda j, i: (layer, 0, cast_blk(j, i))),
            pl.BlockSpec((None, tc, d), lambda j, i: (layer, cast_blk(j, i), 0)),
            pl.BlockSpec((None, to, d_out), lambda j, i: (layer, cast_blk(j, i), 0)),
        ],
        out_specs=[
            pl.BlockSpec((TOK_BLK, tn), lambda j, i: (i, j)),
            pl.BlockSpec((None, d, tc), lambda j, i: (cast_blk(j, i) // per_slab, 0, cast_blk(j, i) % per_slab)),
            pl.BlockSpec((tc, d), lambda j, i: (cast_blk(j, i), 0)),
            pl.BlockSpec((to, d_out), lambda j, i: (cast_blk(j, i), 0)),
        ],
        out_shape=[
            jax.ShapeDtypeStruct((m, n), F32),
            jax.ShapeDtypeStruct((ff // MLP_TF, d, MLP_TF), BF16),
            jax.ShapeDtypeStruct((ff, d), BF16),
            jax.ShapeDtypeStruct(w_out.shape[1:], BF16),
        ],
        compiler_params=pltpu.CompilerParams(vmem_limit_bytes=VMEM_LIMIT_BIG),
        name="in_projection",
    )(h, w, w1, w2, w_out)


def _scan_constants(reverse):
    c = CHUNK
    i = np.arange(c)[:, None]
    j = np.arange(c)[None, :]
    tri = (j >= i) if reverse else (j <= i)
    mask = np.zeros((N_LEVELS + 1, c, c), np.float32)
    for l in range(N_LEVELS):
        level = ((i ^ j) >> l) == 1
        mask[l] = level & ((i < j) if reverse else (i > j))
    mask[N_LEVELS] = np.eye(c, dtype=np.float32)
    return jnp.asarray(tri, BF16), jnp.asarray(np.concatenate([mask, mask], axis=1), BF16)


def _level_exponents(g, cum, scr_ref, reverse):
    c = CHUNK
    scr_ref[...] = cum
    row = lax.broadcasted_iota(jnp.int32, (c, LANE), 0)
    up = pltpu.roll(g, c - 1, 0)
    dn = pltpu.roll(g, 1, 0)
    m2 = row & 1
    m4 = row & 3
    if reverse:
        e0 = jnp.where(m2 == 0, g, 0.0)
        e1 = jnp.where(m4 == 0, g + up, jnp.where(m4 == 1, g, jnp.where(m4 == 2, 0.0, dn)))
    else:
        e0 = jnp.where(m2 == 1, g, 0.0)
        e1 = jnp.where(m4 == 0, up, jnp.where(m4 == 1, 0.0, jnp.where(m4 == 2, g, g + dn)))
    out = [e0, e1]
    for l in range(2, N_LEVELS):
        s = 1 << l
        pieces = []
        for blk in range(c // (2 * s)):
            lo = blk * 2 * s
            ref = scr_ref[lo + (s if reverse else s - 1):lo + (s if reverse else s - 1) + 1, :]
            if s < 8:
                pieces.append(-jnp.abs(cum[lo:lo + 2 * s] - ref))
            else:
                first, second = cum[lo:lo + s], cum[lo + s:lo + 2 * s]
                pieces += [first - ref, ref - second] if reverse else [ref - first, second - ref]
        out.append(pieces[0] if len(pieces) == 1 else jnp.concatenate(pieces, axis=0))
    return out


def _pair_weights(qs, k, exps, mask_ref):
    n = len(qs) * CHUNK

    def lhs(w):
        parts = [q if w is None else q * w for q in qs]
        return parts[0] if len(parts) == 1 else jnp.concatenate(parts, axis=0)

    att = mask_ref[N_LEVELS, 0:n, :] * _dot_nt(lhs(None), k).astype(BF16)
    for l, e in enumerate(exps):
        w = jnp.exp2(e).astype(BF16)
        att = att + mask_ref[l, 0:n, :] * _dot_nt(lhs(w), k * w).astype(BF16)
    return att


def _readout(o, gate, g):
    y = o * lax.rsqrt(jnp.mean(o * o, axis=-1, keepdims=True) + EPS) * g
    return y * _silu(gate)


class _Cols:
    def __init__(self, ref, start):
        self.ref, self.start = ref, start

    def __getitem__(self, idx):
        rows, cols = idx
        return self.ref[rows, self.start + cols.start:self.start + cols.stop]


def _scan_kernel(*refs, reverse):
    p_ref, al_ref, cs_ref, wg_ref, bg_ref, lb_ref, tri_ref, mask_ref = refs[:8]
    if reverse:
        o_ref, scr_ref, sg_ref, sh_ref = refs[8:]
    else:
        orev_ref, gn_ref, hn_ref, ya_ref, yb_ref, scr_ref, sg_ref, sh_ref = refs[8:]
    c = CHUNK
    gq_ref, gk_ref, gv_ref, gr_ref, hq_ref, hi_ref, hg_ref = (
        _Cols(p_ref, P_COLS[name]) for name in ("gla_q", "gla_k", "gla_v", "gla_r", "hg_q", "hg_i", "hg_g"))
    hf_ref = _Cols(p_ref, P_COLS["hg_f"] + (HG_HEADS * HEAD_DIM if reverse else 0))
    cos_ref, sin_ref = _Cols(cs_ref, 0), _Cols(cs_ref, LANE)

    @pl.when(pl.program_id(1) == 0)
    def _():
        sg_ref[...] = jnp.zeros_like(sg_ref)
        sh_ref[...] = jnp.zeros_like(sh_ref)

    lane = lax.broadcasted_iota(jnp.int32, (c, LANE), 1)
    first_half = (lane % (GLA_DK // 2)) < GLA_DK // 4
    head_a = lane < GLA_DK
    lane_masks = (head_a, jnp.logical_not(head_a))
    state_mask = jnp.concatenate([head_a, jnp.logical_not(head_a)], axis=0)
    end_row = 0 if reverse else c - 1
    n_chunks = ROW_BLK // c
    offsets = [(n_chunks - 1 - ci) * c if reverse else ci * c for ci in range(n_chunks)]

    def rope(x, rows):
        swapped = jnp.where(first_half, pltpu.roll(x, LANE - GLA_DK // 4, 1), pltpu.roll(x, GLA_DK // 4, 1))
        return x * cos_ref[rows, 0:LANE] + swapped * sin_ref[rows, 0:LANE]

    chains = []
    for pr in range(SCAN_PAIRS_PER_STEP):
        gla = slice(pr * LANE, (pr + 1) * LANE)
        for off in offsets:
            rows = slice(off, off + c)
            q = rope(gq_ref[rows, gla], rows) * (GLA_DK ** -0.5)
            k = rope(gk_ref[rows, gla], rows)
            logit = _dot(al_ref[rows, :].astype(BF16), wg_ref[pr]) + bg_ref[pr]
            soft = jnp.log2(1.0 + jnp.exp2(jnp.abs(logit) * -LOG2E))
            g = (jnp.minimum(logit, 0.0) * LOG2E - soft) * (1.0 / GLA_TAU)
            chains.append((q, k, g))
            for hh in range(2):
                head = slice((2 * pr + hh) * LANE, (2 * pr + hh + 1) * LANE)
                lb = lb_ref[pr][:, hh * LANE:(hh + 1) * LANE]
                logit = hf_ref[rows, head]
                e = jnp.exp2(jnp.abs(logit) * -LOG2E)
                r = 1.0 / (1.0 + e)
                sig = jnp.where(logit >= 0, r, e * r)
                sig_neg = jnp.where(logit >= 0, e * r, r)
                g = jnp.log2(lb + (1.0 - lb) * sig)
                k = (1.0 - lb) * sig_neg
                q = _silu(hq_ref[rows, head])
                chains.append((q, k, g))

    g3 = jnp.concatenate([_split3(g) for (_, _, g) in chains], axis=-1)
    cum_all = _dot(tri_ref[...], g3)
    cums = [_sum3(cum_all[:, 3 * LANE * n:3 * LANE * (n + 1)], LANE) for n in range(len(chains))]

    atts = []
    for n, ((q, k, g), cum) in enumerate(zip(chains, cums)):
        exps = _level_exponents(g, cum, scr_ref.at[n], reverse)
        qs = [jnp.where(m, q, 0.0).astype(BF16) for m in lane_masks] if n % 3 == 0 else [q.astype(BF16)]
        atts.append(_pair_weights(qs, k.astype(BF16), exps, mask_ref))

    for pr in range(SCAN_PAIRS_PER_STEP):
        for ci, off in enumerate(offsets):
            rows = slice(off, off + c)
            n0 = (pr * n_chunks + ci) * 3
            (q, k, _), cum, att = chains[n0], cums[n0], atts[n0]
            v = gv_ref[rows, pr * 2 * LANE:(pr + 1) * 2 * LANE].astype(BF16)
            cum_end = cum[end_row:end_row + 1, :]
            st = sg_ref[pr]
            inter = _dot_nt((q * jnp.exp2(cum)).astype(BF16), st.astype(BF16))
            o_gla = jnp.concatenate([_dot(att[0:c], v[:, 0:LANE]), _dot(att[c:2 * c], v[:, LANE:2 * LANE])],
                                    axis=-1) + inter
            kd = (k * jnp.exp2(cum_end - cum)).astype(BF16)
            sg_ref[pr] = st * jnp.exp2(cum_end) + jnp.where(state_mask, _dot_tn(v, kd), 0.0)
            o_hg = []
            for hh in range(2):
                head = slice((2 * pr + hh) * LANE, (2 * pr + hh + 1) * LANE)
                (q, k, _), cum, att = chains[n0 + 1 + hh], cums[n0 + 1 + hh], atts[n0 + 1 + hh]
                v = hi_ref[rows, head].astype(BF16)
                cum_end = cum[end_row:end_row + 1, :]
                st = sh_ref[2 * pr + hh]
                o_hg.append(_dot(att, v) + _dot_nt((q * jnp.exp2(cum)).astype(BF16), st.astype(BF16)))
                kd = (k * jnp.exp2(cum_end - cum)).astype(BF16)
                sh_ref[2 * pr + hh] = st * jnp.exp2(cum_end) + _dot_tn(v, kd)
            o_all = jnp.concatenate([o_gla] + o_hg, axis=-1)
            out_cols = slice(pr * 4 * LANE, (pr + 1) * 4 * LANE)
            if reverse:
                o_ref[rows, out_cols] = o_all
            else:
                o_all = o_all + orev_ref[rows, out_cols]
                for hh in range(2):
                    head = slice((2 * pr + hh) * LANE, (2 * pr + hh + 1) * LANE)
                    y = _readout(o_all[:, hh * LANE:(hh + 1) * LANE], gr_ref[rows, head], gn_ref[...])
                    ya_ref[rows, head] = y.astype(ya_ref.dtype)
                    y = _readout(o_all[:, (2 + hh) * LANE:(3 + hh) * LANE], hg_ref[rows, head], hn_ref[...])
                    yb_ref[rows, head] = y.astype(yb_ref.dtype)


def _gated_scans(p, rope_t, wg, bg, lb, gla_norm, hg_norm, batch, n_lat_blk):
    assert CHUNK == HEAD_DIM == LANE and GLA_HEADS == HG_HEADS
    assert GLA_HEADS == 2 * SCAN_PAIRS_PER_STEP
    m = p.shape[0]
    pps = SCAN_PAIRS_PER_STEP
    n_blk = n_lat_blk + 1
    scan_cols = P_COLS["na_q"]
    assert all(P_COLS[s] < scan_cols for s in ("gla_q", "gla_k", "gla_v", "gla_r", "hg_q", "hg_f", "hg_i", "hg_g"))

    def run(reverse, extra_in, extra_specs, out_shape, out_specs):
        def pos(t):
            lat = (n_lat_blk - t) if reverse else (t - 1)
            return jnp.where(t == 0, n_lat_blk, lat)

        def row_blk(b, t):
            return jnp.where(t == 0, batch * n_lat_blk + b, b * n_lat_blk + pos(t))

        def rows(width, col_blk=0):
            return pl.BlockSpec((ROW_BLK, width), lambda b, t: (row_blk(b, t), col_blk))

        def const(shape):
            return pl.BlockSpec(shape, lambda b, t: (0,) * len(shape))

        d = 1 if reverse else 0
        tri, mask = _scan_constants(reverse)
        in_specs = [
            rows(scan_cols),
            rows(LANE, P_COLS["a_low"] // LANE),
            pl.BlockSpec((ROW_BLK, 2 * LANE), lambda b, t: (pos(t), 0)),
            pl.BlockSpec((None, pps, LANE, LANE), lambda b, t: (d, 0, 0, 0)),
            pl.BlockSpec((None, pps, 1, LANE), lambda b, t: (d, 0, 0, 0)),
            pl.BlockSpec((None, pps, 1, 2 * LANE), lambda b, t: (d, 0, 0, 0)),
            const((CHUNK, CHUNK)),
            const((N_LEVELS + 1, 2 * CHUNK, CHUNK)),
        ] + [s(rows, const) for s in extra_specs]
        return pl.pallas_call(
            functools.partial(_scan_kernel, reverse=reverse),
            grid=(batch, n_blk),
            in_specs=in_specs,
            out_specs=[s(rows, const) for s in out_specs],
            out_shape=out_shape,
            scratch_shapes=[
                pltpu.VMEM((pps * 3 * ROW_BLK // CHUNK, CHUNK, LANE), F32),
                pltpu.VMEM((pps, 2 * HEAD_DIM, LANE), F32),
                pltpu.VMEM((pps * 2, HEAD_DIM, HEAD_DIM), F32),
            ],
            compiler_params=pltpu.CompilerParams(vmem_limit_bytes=VMEM_LIMIT),
            name="gated_scan_rev" if reverse else "gated_scan_fwd",
        )(p, p, rope_t, wg, bg, lb, tri, mask, *extra_in)

    def rows_spec(width):
        return lambda rows, const: rows(width)

    def const_spec(rows, const):
        return const((1, LANE))

    n_heads = GLA_HEADS + HG_HEADS
    (o_rev,) = run(True, (), (), [jax.ShapeDtypeStruct((m, n_heads * HEAD_DIM), F32)],
                   [rows_spec(n_heads * HEAD_DIM)])
    return run(False, (o_rev, gla_norm.reshape(1, -1), hg_norm.reshape(1, -1)),
               (rows_spec(n_heads * HEAD_DIM), const_spec, const_spec),
               [jax.ShapeDtypeStruct((m, GLA_HEADS * HEAD_DIM), BF16),
                jax.ShapeDtypeStruct((m, HG_HEADS * HEAD_DIM), BF16)],
               [rows_spec(GLA_HEADS * HEAD_DIM), rows_spec(HG_HEADS * HEAD_DIM)])


def _head_norm(x, g):
    return x * lax.rsqrt(jnp.mean(x * x, axis=-1, keepdims=True) + EPS) * g


def _na_span_base(n, n_rows, xp=jnp):
    return xp.clip(n * NA_Q_ROWS - WIN_ROWS // 2, 0, n_rows - NA_SPAN_ROWS)


def _na_variants(n_rows):
    n_blocks = n_rows // NA_Q_ROWS
    assert n_rows % NA_Q_ROWS == 0 and n_blocks >= 4

    def rows(n):
        base = int(_na_span_base(n, n_rows, np))
        out = []
        for i in range(NA_Q_ROWS):
            r = n * NA_Q_ROWS + i
            r0 = min(max(r - WIN_ROWS // 2, 0), n_rows - WIN_ROWS)
            out.append([(base + kk - r + WIN_ROWS - 1, r0 <= base + kk < r0 + WIN_ROWS)
                        for kk in range(NA_SPAN_ROWS)])
        return out

    variants = [rows(0), rows(1), rows(n_blocks - 1)]
    assert all(rows(n) == variants[1] for n in range(1, n_blocks - 1))
    return variants


def _na_kernel(q_ref, kl_ref, vl_ref, kc_ref, vc_ref, qn_ref, kn_ref, tbl_ref, y_ref,
               kls_ref, vls_ref, kcs_ref, vcs_ref, bias_ref, *, n_rows, steps_per_batch, batch):
    t = pl.program_id(1)
    heads = [slice(hd * HEAD_DIM, (hd + 1) * HEAD_DIM) for hd in range(NA_HEADS_PER_STEP)]

    @pl.when((t > 0) & ((t - 1) % steps_per_batch == 0))
    def _():
        for cols in heads:
            kls_ref[:, cols] = _head_norm(kl_ref[:, cols], kn_ref[...]).astype(BF16)
        vls_ref[...] = vl_ref[...].astype(BF16)

    @pl.when(t == 0)
    def _():
        for cols in heads:
            kcs_ref[:, cols] = _head_norm(kc_ref[:, cols], kn_ref[...]).astype(BF16)
        vcs_ref[...] = vc_ref[...].astype(BF16)
        n_dr = tbl_ref.shape[1] - 1
        for hd in range(NA_HEADS_PER_STEP):
            for v, var in enumerate(_na_variants(n_rows)):
                for i, row in enumerate(var):
                    for g in range(NA_SPAN_ROWS // 2):
                        pair = [tbl_ref[hd, dr if ok else n_dr] for dr, ok in row[2 * g:2 * g + 2]]
                        assert all(0 <= dr < n_dr for dr, ok in row[2 * g:2 * g + 2] if ok)
                        bias_ref[hd, v, i * GRID_W:(i + 1) * GRID_W, g * LANE:(g + 1) * LANE] = (
                            jnp.concatenate(pair, axis=1))

    qs = [_head_norm(q_ref[:, cols], qn_ref[...] * (HEAD_DIM ** -0.5 * LOG2E)).astype(BF16) for cols in heads]

    @pl.when(t == 0)
    def _():
        for q, cols in zip(qs, heads):
            for b in range(batch):
                rows = slice(b * ROW_BLK, (b + 1) * ROW_BLK)
                s = _dot_nt(q[rows], kcs_ref[rows, cols])
                e = jnp.exp2(s - jnp.max(s, axis=-1, keepdims=True))
                o = _dot(e.astype(BF16), vcs_ref[rows, cols]) / jnp.sum(e, axis=-1, keepdims=True)
                y_ref[rows, cols] = o.astype(y_ref.dtype)

    @pl.when(t > 0)
    def _():
        ctx_rows = pl.ds(pl.multiple_of(((t - 1) // steps_per_batch) * ROW_BLK, ROW_BLK), ROW_BLK)
        ctx_part = []
        for q, cols in zip(qs, heads):
            s = _dot_nt(q, kcs_ref[ctx_rows, cols])
            m = jnp.max(s, axis=-1, keepdims=True)
            e = jnp.exp2(s - m)
            ctx_part.append((m, jnp.sum(e, axis=-1, keepdims=True), _dot(e.astype(BF16), vcs_ref[ctx_rows, cols])))
        for sub in range(TOK_BLK // ROW_BLK):
            rows = slice(sub * ROW_BLK, (sub + 1) * ROW_BLK)
            n = ((t - 1) % steps_per_batch) * (TOK_BLK // ROW_BLK) + sub
            base = _na_span_base(n, n_rows)
            variant = jnp.where(n == 0, 0, jnp.where(n == n_rows // NA_Q_ROWS - 1, 2, 1))
            for hd, (q, cols) in enumerate(zip(qs, heads)):
                qr = q[rows]
                m, den, o = (part[rows] for part in ctx_part[hd])
                for g in range(NA_SPAN_ROWS * GRID_W // NA_KEY_CHUNK):
                    keys = pl.ds(pl.multiple_of(base * GRID_W + g * NA_KEY_CHUNK, GRID_W), NA_KEY_CHUNK)
                    s = (_dot_nt(qr, kls_ref[keys, cols])
                         + bias_ref[hd, variant, :, g * NA_KEY_CHUNK:(g + 1) * NA_KEY_CHUNK])
                    m_new = jnp.maximum(m, jnp.max(s, axis=-1, keepdims=True))
                    alpha = jnp.exp2(m - m_new)
                    e = jnp.exp2(s - m_new)
                    den = alpha * den + jnp.sum(e, axis=-1, keepdims=True)
                    o = alpha * o + _dot(e.astype(BF16), vls_ref[keys, cols])
                    m = m_new
                y_ref[rows, cols] = (o / den).astype(y_ref.dtype)


def _na_bias_table(rpb):
    n_heads, n_dr, n_dc = rpb.shape
    c = np.arange(GRID_W)[:, None]
    kc = np.arange(GRID_W)[None, :]
    c0 = np.clip(c - WIN_COLS // 2, 0, GRID_W - WIN_COLS)
    col_valid = (kc >= c0) & (kc < c0 + WIN_COLS)
    pick_dc = (np.arange(n_dc)[:, None, None] == (kc - c + WIN_COLS - 1)[None]) & col_valid[None]
    col = jnp.einsum("hrb,bcq->hrcq", rpb.astype(F32), jnp.asarray(pick_dc, F32), precision=lax.Precision.HIGHEST)
    col = col + jnp.asarray(np.where(col_valid, 0.0, NEG_BIG), F32)
    return jnp.concatenate([col, jnp.full((n_heads, 1, GRID_W, GRID_W), NEG_BIG, F32)], axis=1) * LOG2E


def _neighbourhood_attention(p, q_norm, k_norm, bias_tbl, layer, batch, n_lat_blk):
    m = p.shape[0]
    lat_rows = n_lat_blk * ROW_BLK
    assert lat_rows % TOK_BLK == 0 and batch * ROW_BLK == TOK_BLK
    steps_per_batch = lat_rows // TOK_BLK
    width = NA_HEADS_PER_STEP * HEAD_DIM
    qb, kb, vb = (P_COLS[name] // width for name in ("na_q", "na_k", "na_v"))
    assert NA_HEADS % NA_HEADS_PER_STEP == 0 and all(P_COLS[name] % width == 0 for name in ("na_q", "na_k", "na_v"))
    ctx_blk = batch * steps_per_batch
    groups = NA_HEADS // NA_HEADS_PER_STEP

    def q_blk(t):
        return jnp.where(t == 0, ctx_blk, t - 1)

    def kv_blk(t):
        return jnp.maximum(t - 1, 0) // steps_per_batch

    kern = functools.partial(_na_kernel, n_rows=lat_rows // GRID_W, steps_per_batch=steps_per_batch, batch=batch)
    return pl.pallas_call(
        kern,
        grid=(groups, 1 + batch * steps_per_batch),
        in_specs=[
            pl.BlockSpec((TOK_BLK, width), lambda h, t: (q_blk(t), qb + h)),
            pl.BlockSpec((lat_rows, width), lambda h, t: (kv_blk(t), kb + h)),
            pl.BlockSpec((lat_rows, width), lambda h, t: (kv_blk(t), vb + h)),
            pl.BlockSpec((TOK_BLK, width), lambda h, t: (ctx_blk, kb + h)),
            pl.BlockSpec((TOK_BLK, width), lambda h, t: (ctx_blk, vb + h)),
            pl.BlockSpec((1, LANE), lambda h, t: (0, 0)),
            pl.BlockSpec((1, LANE), lambda h, t: (0, 0)),
            pl.BlockSpec((NA_HEADS_PER_STEP,) + bias_tbl.shape[1:], lambda h, t: (layer * groups + h, 0, 0, 0)),
        ],
        out_specs=pl.BlockSpec((TOK_BLK, width), lambda h, t: (q_blk(t), h)),
        out_shape=jax.ShapeDtypeStruct((m, NA_HEADS * HEAD_DIM), BF16),
        scratch_shapes=[
            pltpu.VMEM((lat_rows, width), BF16),
            pltpu.VMEM((lat_rows, width), BF16),
            pltpu.VMEM((TOK_BLK, width), BF16),
            pltpu.VMEM((TOK_BLK, width), BF16),
            pltpu.VMEM((NA_HEADS_PER_STEP, 3, ROW_BLK, NA_SPAN_ROWS * GRID_W), F32),
        ],
        compiler_params=pltpu.CompilerParams(vmem_limit_bytes=VMEM_LIMIT),
        name="neighbourhood_attention",
    )(p, p, p, p, p, q_norm.reshape(1, -1), k_norm.reshape(1, -1), bias_tbl)


def _outproj_kernel(*refs, n_lat, n_y):
    xl_ref, xc_ref = refs[:2]
    y_refs = refs[2:2 + n_y]
    mod_ref, g_ref = refs[2 + n_y:4 + n_y]
    w_refs = refs[4 + n_y:4 + 2 * n_y]
    o_ref, h_ref = refs[4 + 2 * n_y:]
    i = pl.program_id(0)
    acc = _dot(y_refs[0][...], w_refs[0][...])
    for y_ref, w_ref in zip(y_refs[1:], w_refs[1:]):
        acc = acc + _dot(y_ref[...], w_ref[...])
    upd = mod_ref[2:3, :] * acc

    def emit(x_ref):
        x1 = x_ref[...] + upd
        o_ref[...] = x1
        h_ref[...] = _norm_mod(x1, g_ref[...], mod_ref[3:4, :], mod_ref[4:5, :]).astype(h_ref.dtype)

    pl.when(i < n_lat)(lambda: emit(xl_ref))
    pl.when(i >= n_lat)(lambda: emit(xc_ref))


def _out_projection(tok, ys, modt, g_mlp, w_out, rows_per_mod, with_ctx):
    d = w_out.shape[1]
    tm = TOK_BLK
    per = rows_per_mod // tm
    n_blk = tok.n_lat + (1 if with_ctx else 0)
    y_specs, w_specs, row0 = [], [], 0
    for y in ys:
        width = y.shape[1]
        assert row0 % width == 0
        y_specs.append(pl.BlockSpec((tm, width), lambda i: (i, 0)))
        w_specs.append(pl.BlockSpec((width, d), lambda i, blk=row0 // width: (blk, 0)))
        row0 += width
    assert row0 == w_out.shape[0]
    return pl.pallas_call(
        functools.partial(_outproj_kernel, n_lat=tok.n_lat, n_y=len(ys)),
        grid=(n_blk,),
        in_specs=tok.specs(d) + y_specs + [
            pl.BlockSpec((None, 6, d), lambda i: (i // per, 0, 0)),
            pl.BlockSpec((1, d), lambda i: (0, 0)),
        ] + w_specs,
        out_specs=[pl.BlockSpec((tm, d), lambda i: (i, 0)), pl.BlockSpec((tm, d), lambda i: (i, 0))],
        out_shape=[jax.ShapeDtypeStruct((n_blk * tm, d), F32), jax.ShapeDtypeStruct((n_blk * tm, d), BF16)],
        compiler_params=pltpu.CompilerParams(vmem_limit_bytes=VMEM_LIMIT),
        name="out_projection",
    )(tok.lat, tok.ctx, *ys, modt, g_mlp.reshape(1, d), *([w_out] * len(ys)))


def _mlp_kernel(*refs, emit_next):
    if emit_next:
        h_ref, x_ref, mod_ref, w1_ref, w2_ref, gn_ref, modn_ref, o_ref, hn_ref, acc_ref = refs
    else:
        h_ref, x_ref, mod_ref, w1_ref, w2_ref, o_ref, acc_ref = refs
    j = pl.program_id(1)

    @pl.when(j == 0)
    def _():
        acc_ref[...] = jnp.zeros_like(acc_ref)

    a = jnp.maximum(_dot(h_ref[...], w1_ref[...]), 0.0)
    acc_ref[...] += _dot((a * a).astype(BF16), w2_ref[...])

    @pl.when(j == pl.num_programs(1) - 1)
    def _():
        x2 = x_ref[...] + mod_ref[5:6, :] * acc_ref[...]
        o_ref[...] = x2
        if emit_next:
            hn_ref[...] = _norm_mod(x2, gn_ref[...], modn_ref[0:1, :], modn_ref[1:2, :]).astype(hn_ref.dtype)


def _mlp(h, x1, modt, w1, w2, rows_per_mod, next_norm):
    m, d = x1.shape
    n_j, _, tf = w1.shape
    tm = TOK_BLK
    per = rows_per_mod // tm
    emit_next = next_norm is not None

    def rows(i, j):
        return (i, 0)

    def mod_rows(i, j):
        return (i // per, 0, 0)

    in_specs = [
        pl.BlockSpec((tm, d), rows),
        pl.BlockSpec((tm, d), rows),
        pl.BlockSpec((None, 6, d), mod_rows),
        pl.BlockSpec((None, d, tf), lambda i, j: (j, 0, 0)),
        pl.BlockSpec((tf, d), lambda i, j: (j, 0)),
    ]
    args = [h, x1, modt, w1, w2]
    out_specs = [pl.BlockSpec((tm, d), rows)]
    out_shape = [jax.ShapeDtypeStruct((m, d), F32)]
    if emit_next:
        g_next, modt_next = next_norm
        in_specs += [pl.BlockSpec((1, d), lambda i, j: (0, 0)), pl.BlockSpec((None, 6, d), mod_rows)]
        args += [g_next.reshape(1, d), modt_next]
        out_specs.append(pl.BlockSpec((tm, d), rows))
        out_shape.append(jax.ShapeDtypeStruct((m, d), BF16))
    return pl.pallas_call(
        functools.partial(_mlp_kernel, emit_next=emit_next),
        grid=(m // tm, n_j),
        in_specs=in_specs,
        out_specs=out_specs,
        out_shape=out_shape,
        scratch_shapes=[pltpu.VMEM((tm, d), F32)],
        compiler_params=pltpu.CompilerParams(vmem_limit_bytes=VMEM_LIMIT_BIG),
        name="mlp",
    )(*args)


def _relayout_kernel(wt_ref, o_ref):
    al, aw = P_SRC["a_low"]
    x = wt_ref[...]
    pad = jnp.zeros((LANE - aw, x.shape[1]), x.dtype)
    cols = jnp.concatenate([x[:al], x[al + aw:], x[al:al + aw], pad], axis=0)
    o_ref[...] = cols.T.astype(o_ref.dtype)


def _permute_w_in(w_in):
    depth, d, n = w_in.shape
    tc = 256
    return pl.pallas_call(
        _relayout_kernel,
        grid=(depth, d // tc),
        in_specs=[pl.BlockSpec((None, n, tc), lambda l, i: (l, 0, i))],
        out_specs=pl.BlockSpec((None, tc, P_WIDTH), lambda l, i: (l, i, 0)),
        out_shape=jax.ShapeDtypeStruct((depth, d, P_WIDTH), BF16),
        compiler_params=pltpu.CompilerParams(vmem_limit_bytes=VMEM_LIMIT_BIG),
        name="w_in_relayout",
    )(jnp.swapaxes(w_in, 1, 2))


def _gla_gate_weights(w_a2, b_a):
    n_pairs = GLA_HEADS // 2
    wg = jnp.zeros((2, n_pairs, LANE, 2 * GLA_DK), F32)
    for dd in range(2):
        blk = w_a2[dd].reshape(GLA_RANK, n_pairs, 2 * GLA_DK).transpose(1, 0, 2)
        wg = wg.at[dd, :, dd * GLA_RANK:(dd + 1) * GLA_RANK, :].set(blk)
    bg = b_a.reshape(2, n_pairs, 1, 2 * GLA_DK)
    return wg.astype(BF16), bg.astype(F32)


def _rope_tables(seq, n_ctx):
    quarter = GLA_DK // 4
    inv_freq = ROPE_BASE ** (-np.arange(quarter, dtype=np.float64) / quarter)
    pos = np.arange(seq)
    lane = np.arange(LANE)
    is_col = (lane % GLA_DK) // (GLA_DK // 2) == 1
    first = (lane % (GLA_DK // 2)) < quarter
    p = np.where(is_col[None, :], (pos % GRID_W)[:, None], (pos // GRID_W)[:, None]).astype(np.float64)
    ang = p * inv_freq[lane % quarter][None, :]
    cos = np.cos(ang)
    sin = np.where(first[None, :], -np.sin(ang), np.sin(ang))
    cos = np.concatenate([cos, np.ones((n_ctx, LANE))], axis=0)
    sin = np.concatenate([sin, np.zeros((n_ctx, LANE))], axis=0)
    return jnp.asarray(np.concatenate([cos, sin], axis=1), F32)


def kernel(x, c, ctx, c_ctx, w_mod, b_mod, attn_norm, w_in, gla_w_a2, gla_b_a, gla_norm, hg_lower_bounds, hg_norm, na_q_norm, na_k_norm, na_rpb, w_out, mlp_norm, w_mlp1, w_mlp2):
    batch, seq, d = x.shape
    n_ctx = ctx.shape[1]
    depth = w_mod.shape[0]
    assert seq % ROW_BLK == 0 and n_ctx == ROW_BLK and batch * n_ctx == TOK_BLK and d % LANE == 0
    assert seq % TOK_BLK == 0 and seq % GRID_W == 0
    n_lat_blk = seq // ROW_BLK
    lat_rows = batch * seq
    n_lat_tok = lat_rows // TOK_BLK

    tok = _Tokens(x.reshape(lat_rows, d), ctx.reshape(batch * n_ctx, d), n_lat_tok, 0)

    cc = jnp.zeros((8, d), F32).at[0:batch].set(c).at[batch].set(c_ctx)
    mod = _modulation(cc, w_mod, b_mod)
    modt = mod[:, 0:batch + 1].reshape(depth, batch + 1, 6, d)

    lb_p = jax.nn.softmax(hg_lower_bounds.astype(F32), axis=0)
    lower = jnp.cumsum(lb_p, axis=0) - lb_p[0]
    lower = lower.reshape(depth, 2, HG_HEADS // 2, 1, 2 * HEAD_DIM)
    rope_t = _rope_tables(seq, n_ctx)

    bias_tbl = _na_bias_table(na_rpb.reshape((depth * NA_HEADS,) + na_rpb.shape[2:]))
    w_in_b = _permute_w_in(w_in)

    h = _first_norm(tok, modt[0], attn_norm[0], seq)
    for l in range(depth):
        last = l == depth - 1
        p, w1_b, w2_b, w_out_b = _in_projection(h, w_in_b, w_mlp1, w_mlp2, w_out, l)
        wg, bg = _gla_gate_weights(gla_w_a2[l], gla_b_a[l])
        ya, yb = _gated_scans(p, rope_t, wg, bg, lower[l], gla_norm[l], hg_norm[l], batch, n_lat_blk)
        yc = _neighbourhood_attention(p, na_q_norm[l], na_k_norm[l], bias_tbl, l, batch, n_lat_blk)
        x1, h2 = _out_projection(tok, (ya, yb, yc), modt[l], mlp_norm[l], w_out_b, seq, with_ctx=not last)
        if last:
            (xu,) = _mlp(h2, x1, modt[l], w1_b, w2_b, seq, None)
        else:
            xu, h = _mlp(h2, x1, modt[l], w1_b, w2_b, seq, (attn_norm[l + 1], modt[l + 1]))
        tok = _Tokens(xu, xu, n_lat_tok, n_lat_tok)
    return xu.reshape(batch, seq, d)
```
